```python
import math
import jax, jax.numpy as jnp
from jax import lax
import numpy as np

D_MODEL = 1024
BATCH = 16
SEQ = 2048
DEPTH = 2

GRID_W = 64
CTX_LEN = 256
N_EVEN = (DEPTH + 1) // 2
N_ODD = DEPTH // 2
EPS = 1e-6
CHUNK = 64
CONV_W = 4
MIX_WIDTH = D_MODEL

A_WIDTH = D_MODEL // 2
A_DK = 128
A_DV = 128
A_HEADS = A_WIDTH // A_DV
A_QK = A_HEADS * A_DK
A_QKV = 2 * A_QK + A_WIDTH
A_COLS = A_QKV + A_WIDTH + 4 * A_HEADS
B_WIDTH = D_MODEL // 2
B_BLOCKS = 8
B_BDIM = B_WIDTH // B_BLOCKS
LRU_C = 8.0
B_COLS = 2 * B_WIDTH
C_WIDTH = D_MODEL // 2
C_DV = 128
C_HEADS = C_WIDTH // C_DV
C_DQK = C_DV // 2
C_QK = C_HEADS * C_DQK
C_STATE_COLS = 2 * C_QK + C_WIDTH + 4 * C_HEADS
D_WIDTH = D_MODEL - C_WIDTH
POOL_SIZES = (2, 4, 8, 16)
D_GROUPS = len(POOL_SIZES)
D_GDIM = D_WIDTH // D_GROUPS
EVEN_IN = A_COLS + B_COLS
ODD_IN = C_STATE_COLS + C_WIDTH + D_WIDTH
PEER_HEADS = 8
N_KEYS = 128
N_EXPERTS = N_KEYS * N_KEYS
PEER_TOPK = 16
PEER_QDIM = 256
PEER_BLOCK = 128

kernel_name = 'hybrid_gdn_rglru_mlstm_pool_peer_dit'


def _split(t, sizes):
    cuts = [int(s) for s in np.cumsum(sizes)[:-1]]
    return jnp.split(t, cuts, axis=-1)


def _dirflip(t, d, axis):
    return t if d == 0 else jnp.flip(t, axis=axis)


def _rmsnorm(x, g):
    xf = x.astype(jnp.float32)
    y = xf * lax.rsqrt(jnp.mean(xf * xf, axis=-1, keepdims=True) + EPS)
    return (y * g.astype(jnp.float32)).astype(x.dtype)


def _modulate(h, shift, scale):
    return h * (1 + scale) + shift


def _l2norm(t):
    return t * lax.rsqrt(jnp.sum(t * t, axis=-1, keepdims=True) + EPS)


def _dwconv(x, w):
    return lax.conv_general_dilated(x, w[:, None, :].astype(x.dtype), (1,), ((CONV_W // 2 - 1, CONV_W // 2),),
                                    dimension_numbers=('NWC', 'WIO', 'NWC'), feature_group_count=x.shape[-1])


def _to_col_major(h, rows):
    b, l, d = h.shape
    return h.reshape(b, rows, GRID_W, d).transpose(0, 2, 1, 3).reshape(b, l, d)


def _from_col_major(h, rows):
    b, l, d = h.shape
    return h.reshape(b, GRID_W, rows, d).transpose(0, 2, 1, 3).reshape(b, l, d)


def _gdn_chunk(q, k, v, beta, g, s0):
    bsz, nh, L, dk = q.shape
    dv = v.shape[-1]
    nc = L // CHUNK
    q, k, v = (t.reshape(bsz, nh, nc, CHUNK, t.shape[-1]) for t in (q, k, v))
    beta = beta.reshape(bsz, nh, nc, CHUNK)
    gc = jnp.cumsum(g.reshape(bsz, nh, nc, CHUNK), axis=-1)
    incl = jnp.tril(jnp.ones((CHUNK, CHUNK), dtype=bool))
    strict = jnp.tril(jnp.ones((CHUNK, CHUNK), dtype=bool), -1)
    gamma = jnp.exp(jnp.where(incl, gc[..., :, None] - gc[..., None, :], -jnp.inf))
    kk = jnp.einsum('bhncd,bhnsd->bhncs', k, k)
    m_low = jnp.where(strict, beta[..., None] * kk * gamma, 0.0)
    rhs = jnp.concatenate([beta[..., None] * v, beta[..., None] * k * jnp.exp(gc)[..., None]], axis=-1)
    sol = lax.linalg.triangular_solve(m_low, rhs, left_side=True, lower=True, unit_diagonal=True)
    u, w = sol[..., :dv], sol[..., dv:]
    qk = jnp.einsum('bhncd,bhnsd->bhncs', q, k) * gamma
    q_dec = q * jnp.exp(gc)[..., None]
    k_dec = k * jnp.exp(gc[..., -1:] - gc)[..., None]
    g_last = jnp.exp(gc[..., -1])

    def step(s, xs):
        u_c, w_c, qk_c, qd_c, kd_c, gl_c = xs
        v_new = u_c - jnp.einsum('bhcd,bhde->bhce', w_c, s)
        o_c = jnp.einsum('bhcd,bhde->bhce', qd_c, s) + jnp.einsum('bhcs,bhse->bhce', qk_c, v_new)
        s = s * gl_c[..., None, None] + jnp.einsum('bhcd,bhce->bhde', kd_c, v_new)
        return s, o_c

    xs = tuple(jnp.moveaxis(t, 2, 0) for t in (u, w, qk, q_dec, k_dec, g_last))
    s_fin, o = lax.scan(step, s0, xs)
    return s_fin, jnp.moveaxis(o, 0, 2).reshape(bsz, nh, L, dv)


def _gdn_prep(p, conv_w, alog, dtb):
    bsz, L, _ = p.shape
    qkv, z, alpha, beta = _split(p, (A_QKV, A_WIDTH, 2 * A_HEADS, 2 * A_HEADS))
    qkv = jax.nn.silu(_dwconv(qkv, conv_w)).astype(jnp.float32)
    q, k, v = _split(qkv, (A_QK, A_QK, A_WIDTH))
    q = _l2norm(q.reshape(bsz, L, A_HEADS, A_DK).transpose(0, 2, 1, 3)) * A_DK ** -0.5
    k = _l2norm(k.reshape(bsz, L, A_HEADS, A_DK).transpose(0, 2, 1, 3))
    v = v.reshape(bsz, L, A_HEADS, A_DV).transpose(0, 2, 1, 3)
    g = -jnp.exp(alog) * jax.nn.softplus(alpha.astype(jnp.float32).reshape(bsz, L, 2, A_HEADS) + dtb)
    bt = jax.nn.sigmoid(beta.astype(jnp.float32).reshape(bsz, L, 2, A_HEADS))
    return q, k, v, g.transpose(2, 0, 3, 1), bt.transpose(2, 0, 3, 1), z


def _gdn_out(o, z, norm_w):
    bsz, nh, L, dv = o.shape
    o = o.transpose(0, 2, 1, 3)
    o = o * lax.rsqrt(jnp.mean(o * o, axis=-1, keepdims=True) + EPS) * norm_w.astype(jnp.float32)
    zg = jax.nn.silu(z.astype(jnp.float32)).reshape(bsz, L, nh, dv)
    return (o * zg).reshape(bsz, L, nh * dv).astype(z.dtype)


def _gdn_mixer(pc, pl, conv_w, alog, dtb, norm_w):
    qc, kc, vc, gc, bc, zc = _gdn_prep(pc, conv_w, alog, dtb)
    ql, kl, vl, gl, bl, zl = _gdn_prep(pl, conv_w, alog, dtb)
    zero = jnp.zeros((pl.shape[0], A_HEADS, A_DK, A_DV), jnp.float32)
    oc, ol = [], []
    for d in range(2):
        s_c, o_c = _gdn_chunk(*(_dirflip(t, d, 2) for t in (qc, kc, vc, bc[d], gc[d])), zero)
        _, o_l = _gdn_chunk(*(_dirflip(t, d, 2) for t in (ql, kl, vl, bl[d], gl[d])), s_c)
        oc.append(_dirflip(o_c, d, 2))
        ol.append(_dirflip(o_l, d, 2))
    return _gdn_out(oc[0] + oc[1], zc, norm_w), _gdn_out(ol[0] + ol[1], zl, norm_w)


def _lru_coeffs(x, wa, ba, wx, bx, lam):
    bsz, L, _ = x.shape
    xb = x.reshape(bsz, L, B_BLOCKS, B_BDIM)
    r = jax.nn.sigmoid(jnp.einsum('blnd,nde->blne', xb, wa) + ba)
    i = jax.nn.sigmoid(jnp.einsum('blnd,nde->blne', xb, wx) + bx)
    log_a = -LRU_C * jax.nn.softplus(-lam) * r
    a = jnp.exp(log_a)
    b = jnp.sqrt(-jnp.expm1(2.0 * log_a)) * (i * xb)
    return a.reshape(bsz, L, B_WIDTH), b.reshape(bsz, L, B_WIDTH)


def _linear_scan(a, b, h0):
    def comb(left, right):
        return left[0] * right[0], right[0] * left[1] + right[1]
    a_cum, h = lax.associative_scan(comb, (a, b), axis=1)
    return h + a_cum * h0[:, None, :]


def _lru_mixer(pc, pl, conv_w, conv_b, wa, ba, wx, bx, lam):
    def prep(p):
        xb, gate = _split(p, (B_WIDTH, B_WIDTH))
        return (_dwconv(xb, conv_w) + conv_b).astype(jnp.float32), gate
    xc, gtc = prep(pc)
    xl, gtl = prep(pl)
    hs_c, hs_l = [], []
    for d in range(2):
        ac, bcf = _lru_coeffs(_dirflip(xc, d, 1), wa[d], ba[d], wx[d], bx[d], lam[d])
        al, blf = _lru_coeffs(_dirflip(xl, d, 1), wa[d], ba[d], wx[d], bx[d], lam[d])
        h_c = _linear_scan(ac, bcf, jnp.zeros_like(xc[:, 0]))
        h_l = _linear_scan(al, blf, h_c[:, -1])
        hs_c.append(_dirflip(h_c, d, 1))
        hs_l.append(_dirflip(h_l, d, 1))
    out = lambda h, gate: (h * jax.nn.gelu(gate.astype(jnp.float32))).astype(gate.dtype)
    return out(hs_c[0] + hs_c[1], gtc), out(hs_l[0] + hs_l[1], gtl)


def _even_mixer(hc, hl, w_in, w_out, a_conv, a_alog, a_dtb, a_norm, b_conv_w, b_conv_b, b_wa, b_ba, b_wx, b_bx, b_lam):
    pa_c, pb_c = _split(hc @ w_in, (A_COLS, B_COLS))
    pa_l, pb_l = _split(hl @ w_in, (A_COLS, B_COLS))
    ya_c, ya_l = _gdn_mixer(pa_c, pa_l, a_conv, a_alog, a_dtb, a_norm)
    yb_c, yb_l = _lru_mixer(pb_c, pb_l, b_conv_w, b_conv_b, b_wa, b_ba, b_wx, b_bx, b_lam)
    yc = jnp.concatenate([ya_c, yb_c], axis=-1) @ w_out
    yl = jnp.concatenate([ya_l, yb_l], axis=-1) @ w_out
    return yc, yl


def _mlstm_chunk(q, k, v, logi, logf, state, with_out):
    bsz, nh, L, dk = q.shape
    dv = v.shape[-1]
    nc = L // CHUNK
    q, k, v = (t.reshape(bsz, nh, nc, CHUNK, t.shape[-1]) for t in (q, k, v))
    logi = logi.reshape(bsz, nh, nc, CHUNK)
    bcum = jnp.cumsum(logf.reshape(bsz, nh, nc, CHUNK), axis=-1)
    b_last = bcum[..., -1]
    w_end = b_last[..., None] - bcum + logi
    m_chunk = jnp.max(w_end, axis=-1)
    e_end = jnp.exp(w_end - m_chunk[..., None])
    c_chunk = jnp.einsum('bhncd,bhnce->bhnde', k * e_end[..., None], v)
    n_chunk = jnp.einsum('bhncd,bhnc->bhnd', k, e_end)

    def step(carry, xs):
        c_s, n_s, m_s = carry
        cc, ncc, mc, bl = xs
        m_new = jnp.maximum(bl + m_s, mc)
        a_old = jnp.exp(bl + m_s - m_new)
        a_new = jnp.exp(mc - m_new)
        c_new = a_old[..., None, None] * c_s + a_new[..., None, None] * cc
        n_new = a_old[..., None] * n_s + a_new[..., None] * ncc
        return (c_new, n_new, m_new), ((c_s, n_s, m_s) if with_out else None)

    xs = tuple(jnp.moveaxis(t, 2, 0) for t in (c_chunk, n_chunk, m_chunk, b_last))
    final, prev = lax.scan(step, state, xs)
    if not with_out:
        return final, None
    c0, n0, m0 = (jnp.moveaxis(t, 0, 2) for t in prev)
    incl = jnp.tril(jnp.ones((CHUNK, CHUNK), dtype=bool))
    log_d = jnp.where(incl, bcum[..., :, None] - bcum[..., None, :] + logi[..., None, :], -jnp.inf)
    m_intra = jnp.max(log_d, axis=-1)
    s_qk = jnp.einsum('bhncd,bhnsd->bhncs', q, k) * jnp.exp(log_d - m_intra[..., None])
    m_inter = bcum + m0[..., None]
    m_tot = jnp.maximum(m_inter, m_intra)
    w_inter = jnp.exp(m_inter - m_tot)
    w_intra = jnp.exp(m_intra - m_tot)
    num = (w_inter[..., None] * jnp.einsum('bhncd,bhnde->bhnce', q, c0)
           + w_intra[..., None] * jnp.einsum('bhncs,bhnse->bhnce', s_qk, v))
    den = w_inter * jnp.einsum('bhncd,bhnd->bhnc', q, n0) + w_intra * jnp.sum(s_qk, axis=-1)
    h = num / jnp.maximum(jnp.abs(den), jnp.exp(-m_tot))[..., None]
    return final, h.reshape(bsz, nh, L, dv)


def _mlstm_prep(s, ibias, fbias):
    bsz, L, _ = s.shape
    s = s.astype(jnp.float32)
    q, k, v, ig, fg = _split(s, (C_QK, C_QK, C_WIDTH, 2 * C_HEADS, 2 * C_HEADS))
    q = q.reshape(bsz, L, C_HEADS, C_DQK).transpose(0, 2, 1, 3)
    k = k.reshape(bsz, L, C_HEADS, C_DQK).transpose(0, 2, 1, 3) * C_DQK ** -0.5
    v = v.reshape(bsz, L, C_HEADS, C_DV).transpose(0, 2, 1, 3)
    logi = (ig.reshape(bsz, L, 2, C_HEADS) + ibias).transpose(2, 0, 3, 1)
    logf = jax.nn.log_sigmoid(fg.reshape(bsz, L, 2, C_HEADS) + fbias).transpose(2, 0, 3, 1)
    return q, k, v, logi, logf


def _mlstm_out(h, o, norm_w):
    bsz, nh, L, dv = h.shape
    h = h.transpose(0, 2, 1, 3)
    h = h * lax.rsqrt(jnp.mean(h * h, axis=-1, keepdims=True) + EPS) * norm_w.astype(jnp.float32).reshape(nh, dv)
    return (h.reshape(bsz, L, nh * dv) * jax.nn.sigmoid(o.astype(jnp.float32))).astype(o.dtype)


def _pool_mixer(x, seg, w_grp, scale):
    bsz, L, _ = x.shape
    xs = x.astype(jnp.float32).reshape(bsz, L // seg, seg, D_GROUPS, D_GDIM)
    csum = jnp.concatenate([jnp.zeros_like(xs[:, :, :1]), jnp.cumsum(xs, axis=2)], axis=2)
    t = jnp.arange(seg)
    outs = []
    for gi, w in enumerate(POOL_SIZES):
        lo = jnp.maximum(t - w // 2, 0)
        hi = jnp.minimum(t - w // 2 + w, seg)
        cg = csum[..., gi, :]
        mean = (jnp.take(cg, hi, axis=2) - jnp.take(cg, lo, axis=2)) / (hi - lo).astype(jnp.float32)[:, None]
        outs.append(mean - xs[..., gi, :])
    pooled = jnp.stack(outs, axis=3)
    y = jnp.einsum('bnsgd,gde->bnsge', pooled, w_grp).reshape(bsz, L, D_WIDTH) * scale
    return y.astype(x.dtype)


def _odd_mixer(hc, hl, rows, w_in, w_out, ibias, fbias, norm_w, d_w, d_scale, with_ctx):
    sl, ol, dl = _split(hl @ w_in, (C_STATE_COLS, C_WIDTH, D_WIDTH))
    if with_ctx:
        sc, oc, dc = _split(hc @ w_in, (C_STATE_COLS, C_WIDTH, D_WIDTH))
    else:
        sc = hc @ w_in[:, :C_STATE_COLS]
    ql, kl, vl, il, fl = _mlstm_prep(sl, ibias, fbias)
    qc, kc, vc, ic, fc = _mlstm_prep(sc, ibias, fbias)
    bsz = hl.shape[0]
    zero = (jnp.zeros((bsz, C_HEADS, C_DQK, C_DV), jnp.float32), jnp.zeros((bsz, C_HEADS, C_DQK), jnp.float32),
            jnp.zeros((bsz, C_HEADS), jnp.float32))
    hs_c, hs_l = [], []
    for d in range(2):
        st_c, h_c = _mlstm_chunk(*(_dirflip(t, d, 2) for t in (qc, kc, vc, ic[d], fc[d])), zero, with_ctx)
        _, h_l = _mlstm_chunk(*(_dirflip(t, d, 2) for t in (ql, kl, vl, il[d], fl[d])), st_c, True)
        hs_l.append(_dirflip(h_l, d, 2))
        if with_ctx:
            hs_c.append(_dirflip(h_c, d, 2))
    yl = jnp.concatenate([_mlstm_out(hs_l[0] + hs_l[1], ol, norm_w), _pool_mixer(dl, rows, d_w, d_scale)], axis=-1) @ w_out
    if not with_ctx:
        return None, yl
    yc = jnp.concatenate([_mlstm_out(hs_c[0] + hs_c[1], oc, norm_w), _pool_mixer(dc, dc.shape[1], d_w, d_scale)], axis=-1) @ w_out
    return yc, yl


def _peer(h, wq, keys, u_tab, v_tab):
    n_tok, dm = h.shape
    q = (h @ wq).reshape(n_tok, PEER_HEADS, 2, PEER_QDIM // 2)
    s = jnp.einsum('thpd,hpkd->thpk', q, keys)
    s1, i1 = lax.top_k(s[:, :, 0], PEER_TOPK)
    s2, i2 = lax.top_k(s[:, :, 1], PEER_TOPK)
    cand = (s1[..., :, None] + s2[..., None, :]).reshape(n_tok, PEER_HEADS, PEER_TOPK * PEER_TOPK)
    sc, ci = lax.top_k(cand, PEER_TOPK)
    idx = (jnp.take_along_axis(i1, ci // PEER_TOPK, axis=-1) * N_KEYS
           + jnp.take_along_axis(i2, ci % PEER_TOPK, axis=-1))
    gate = jax.nn.softmax(sc.astype(jnp.float32), axis=-1).astype(h.dtype)
    nb = n_tok // PEER_BLOCK

    def block(args):
        hb, ib, gb = args
        act = jax.nn.gelu(jnp.einsum('thkd,td->thk', jnp.take(u_tab, ib, axis=0), hb))
        return jnp.einsum('thk,thkd->td', gb * act, jnp.take(v_tab, ib, axis=0))

    out = lax.map(block, (h.reshape(nb, PEER_BLOCK, dm), idx.reshape(nb, PEER_BLOCK, PEER_HEADS, PEER_TOPK),
                          gate.reshape(nb, PEER_BLOCK, PEER_HEADS, PEER_TOPK)))
    return out.reshape(n_tok, dm)


def setup_inputs(seed: int = 0) -> dict:
    key = jax.random.key(seed)
    ks = iter(jax.random.split(key, 40))
    nrm = lambda shape, scale: jax.random.normal(next(ks), shape, jnp.float32) * scale
    gain = lambda shape: 1.0 + nrm(shape, 0.05)
    D = D_MODEL
    x = nrm((BATCH, SEQ, D), 1.0)
    c = nrm((BATCH, D), 1.0)
    ctx = nrm((BATCH, CTX_LEN, D), 1.0)
    c_ctx = nrm((D,), 1.0)
    ada_w = nrm((DEPTH, D, 6 * D), 0.3 * D ** -0.5)
    ada_b = nrm((DEPTH, 6 * D), 0.02)
    norm_mix = gain((DEPTH, D))
    norm_ffn = gain((DEPTH, D))
    final_norm = gain((D,))
    peer_wq = nrm((DEPTH, D, PEER_HEADS * PEER_QDIM), D ** -0.5)
    peer_keys = nrm((DEPTH, PEER_HEADS, 2, N_KEYS, PEER_QDIM // 2), (PEER_QDIM // 2) ** -0.5)
    peer_u = nrm((DEPTH, N_EXPERTS, D), D ** -0.5)
    peer_v = nrm((DEPTH, N_EXPERTS, D), 1.0)
    ev_w_in = nrm((N_EVEN, D, EVEN_IN), D ** -0.5)
    ev_w_out = nrm((N_EVEN, MIX_WIDTH, D), MIX_WIDTH ** -0.5)
    a_conv = nrm((N_EVEN, CONV_W, A_QKV), 0.5)
    a_alog = jnp.log(jax.random.uniform(next(ks), (N_EVEN, 2, A_HEADS), jnp.float32, 1.0, 16.0))
    dt = jnp.exp(jax.random.uniform(next(ks), (N_EVEN, 2, A_HEADS), jnp.float32, math.log(1e-3), math.log(1e-1)))
    a_dtb = dt + jnp.log(-jnp.expm1(-dt))
    a_norm = gain((N_EVEN, A_DV))
    b_conv_w = nrm((N_EVEN, CONV_W, B_WIDTH), 0.5)
    b_conv_b = nrm((N_EVEN, B_WIDTH), 0.02)
    b_wa = nrm((N_EVEN, 2, B_BLOCKS, B_BDIM, B_BDIM), B_BDIM ** -0.5)
    b_ba = nrm((N_EVEN, 2, B_BLOCKS, B_BDIM), 0.02)
    b_wx = nrm((N_EVEN, 2, B_BLOCKS, B_BDIM, B_BDIM), B_BDIM ** -0.5)
    b_bx = nrm((N_EVEN, 2, B_BLOCKS, B_BDIM), 0.02)
    a8 = jax.random.uniform(next(ks), (N_EVEN, 2, B_BLOCKS, B_BDIM), jnp.float32, 0.9, 0.999)
    p_lam = a8 ** (1.0 / LRU_C)
    b_lam = jnp.log(p_lam) - jnp.log1p(-p_lam)
    od_w_in = nrm((N_ODD, D, ODD_IN), D ** -0.5)
    od_w_out = nrm((N_ODD, MIX_WIDTH, D), MIX_WIDTH ** -0.5)
    c_ibias = nrm((N_ODD, 2, C_HEADS), 0.1)
    c_fbias = jnp.linspace(3.0, 6.0, C_HEADS, dtype=jnp.float32) + nrm((N_ODD, 2, C_HEADS), 0.1)
    c_norm = gain((N_ODD, C_WIDTH))
    d_w = nrm((N_ODD, D_GROUPS, D_GDIM, D_GDIM), D_GDIM ** -0.5)
    d_scale = 1.0 + nrm((N_ODD, D_WIDTH), 0.1)
    return {'x': x, 'c': c, 'ctx': ctx, 'c_ctx': c_ctx, 'ada_w': ada_w, 'ada_b': ada_b,
            'norm_mix': norm_mix, 'norm_ffn': norm_ffn, 'final_norm': final_norm,
            'peer_wq': peer_wq, 'peer_keys': peer_keys, 'peer_u': peer_u, 'peer_v': peer_v,
            'ev_w_in': ev_w_in, 'ev_w_out': ev_w_out, 'a_conv': a_conv, 'a_alog': a_alog, 'a_dtb': a_dtb,
            'a_norm': a_norm, 'b_conv_w': b_conv_w, 'b_conv_b': b_conv_b, 'b_wa': b_wa, 'b_ba': b_ba,
            'b_wx': b_wx, 'b_bx': b_bx, 'b_lam': b_lam, 'od_w_in': od_w_in, 'od_w_out': od_w_out,
            'c_ibias': c_ibias, 'c_fbias': c_fbias, 'c_norm': c_norm, 'd_w': d_w, 'd_scale': d_scale}


def reference(x, c, ctx, c_ctx, ada_w, ada_b, norm_mix, norm_ffn, final_norm, peer_wq, peer_keys, peer_u, peer_v,
              ev_w_in, ev_w_out, a_conv, a_alog, a_dtb, a_norm, b_conv_w, b_conv_b, b_wa, b_ba, b_wx, b_bx, b_lam,
              od_w_in, od_w_out, c_ibias, c_fbias, c_norm, d_w, d_scale):
    bsz, seq, dm = x.shape
    rows = seq // GRID_W
    n_ctx = ctx.shape[1]
    xl, xc = x, ctx
    for l in range(DEPTH):
        with_ctx = l < DEPTH - 1
        sh1, sc1, g1, sh2, sc2, g2 = (m[:, None, :] for m in jnp.split(jax.nn.silu(c) @ ada_w[l] + ada_b[l], 6, axis=-1))
        n_mod = 6 if with_ctx else 2
        cmods = jnp.split(jax.nn.silu(c_ctx) @ ada_w[l][:, :n_mod * dm] + ada_b[l][:n_mod * dm], n_mod)
        hl = _modulate(_rmsnorm(xl, norm_mix[l]), sh1, sc1)
        hc = _modulate(_rmsnorm(xc, norm_mix[l]), cmods[0], cmods[1])
        j = l // 2
        if l % 2 == 0:
            yc, yl = _even_mixer(hc, hl, ev_w_in[j], ev_w_out[j], a_conv[j], a_alog[j], a_dtb[j], a_norm[j],
                                 b_conv_w[j], b_conv_b[j], b_wa[j], b_ba[j], b_wx[j], b_bx[j], b_lam[j])
        else:
            yc, yl = _odd_mixer(hc, _to_col_major(hl, rows), rows, od_w_in[j], od_w_out[j], c_ibias[j], c_fbias[j],
                                c_norm[j], d_w[j], d_scale[j], with_ctx)
            yl = _from_col_major(yl, rows)
        xl = xl + g1 * yl
        hl = _modulate(_rmsnorm(xl, norm_ffn[l]), sh2, sc2)
        if with_ctx:
            xc = xc + cmods[2] * yc
            hc = _modulate(_rmsnorm(xc, norm_ffn[l]), cmods[3], cmods[4])
            f = _peer(jnp.concatenate([hc.reshape(-1, dm), hl.reshape(-1, dm)], axis=0),
                      peer_wq[l], peer_keys[l], peer_u[l], peer_v[l])
            n_ct = bsz * n_ctx
            xc = xc + cmods[5] * f[:n_ct].reshape(xc.shape)
            xl = xl + g2 * f[n_ct:].reshape(xl.shape)
        else:
            xl = xl + g2 * _peer(hl.reshape(-1, dm), peer_wq[l], peer_keys[l], peer_u[l], peer_v[l]).reshape(xl.shape)
    return _rmsnorm(xl, final_norm)
```

```python
import functools
import math

import jax
import jax.numpy as jnp
import numpy as np
from jax import lax
from jax.experimental import pallas as pl
from jax.experimental.pallas import tpu as pltpu

F32 = jnp.float32
BF16 = jnp.bfloat16
HI = lax.Precision.HIGHEST

EPS = 1e-6
GRID_W = 64
CHUNK = 64
LANES = 128
SUBLANES = 8
HALO = 8
CONV_W = 4
LRU_C = 8.0
PEER_TOPK = 16
N_KEYS = 128
POOL_SIZES = (2, 4, 8, 16)
TOK_BLK = 256
EXP_BLK = 1024
NEG_BIG = -1e30
VMEM_LIMIT = 48 * 1024 * 1024


def _cparams(sem):
    return pltpu.CompilerParams(dimension_semantics=sem, vmem_limit_bytes=VMEM_LIMIT)


def _nt(a, b, precision=None):
    return lax.dot_general(a, b, (((1,), (1,)), ((), ())), precision=precision, preferred_element_type=F32)


def _tn(a, b, precision=None):
    return lax.dot_general(a, b, (((0,), (0,)), ((), ())), precision=precision, preferred_element_type=F32)


def _mm(a, b, precision=None):
    return jnp.dot(a, b, precision=precision, preferred_element_type=F32)


def _silu(x):
    return x * jax.nn.sigmoid(x)


def _softplus(x):
    return jnp.maximum(x, 0.0) + jnp.log1p(jnp.exp(-jnp.abs(x)))


def _gelu_tanh(x):
    return 0.5 * x * (1.0 + jnp.tanh(math.sqrt(2.0 / math.pi) * (x + 0.044715 * (x * x * x))))


def _iota(shape, dim):
    return lax.broadcasted_iota(jnp.int32, shape, dim)


def _ada_kernel(c_ref, w_ref, b_ref, o_ref):
    o_ref[...] = _mm(_silu(c_ref[...]), w_ref[...], HI) + b_ref[...]


def _ada(cc, w, b):
    r, d = cc.shape
    n = w.shape[1]
    tn = 1536
    return pl.pallas_call(
        _ada_kernel,
        grid=(n // tn,),
        in_specs=[pl.BlockSpec((r, d), lambda j: (0, 0)),
                  pl.BlockSpec((d, tn), lambda j: (0, j)),
                  pl.BlockSpec((1, tn), lambda j: (0, j))],
        out_specs=pl.BlockSpec((r, tn), lambda j: (0, j)),
        out_shape=jax.ShapeDtypeStruct((r, n), F32),
        compiler_params=_cparams(("arbitrary",)),
        name="ada_mod",
    )(cc, w, b.reshape(1, n))


def _mod_table(c, c_ctx, w, b):
    bsz, d = c.shape
    rows = ((bsz + 1 + SUBLANES - 1) // SUBLANES) * SUBLANES
    cc = jnp.zeros((rows, d), F32).at[:bsz].set(c).at[bsz].set(c_ctx)
    m = _ada(cc, w, b).reshape(rows, 6, d)
    tab = jnp.stack([jnp.broadcast_to(m[bsz], (bsz, 6, d)), m[:bsz]], axis=1)
    return tab.reshape(bsz * 2 * 6, 1, d)


def _nmm_kernel(x_ref, g_ref, sh_ref, sc_ref, w_ref, *o_refs, widths):
    x = x_ref[...]
    y = x * lax.rsqrt(jnp.mean(x * x, axis=-1, keepdims=True) + EPS) * g_ref[...]
    h = (y * (1.0 + sc_ref[0]) + sh_ref[0]).astype(BF16)
    off = 0
    for o_ref, n in zip(o_refs, widths):
        o_ref[...] = _mm(h, w_ref[:, off:off + n])
        off += n


def _norm_mod_proj(x, gain, tab, rowfn, w, widths):
    t, d = x.shape
    n = w.shape[1]
    tm = TOK_BLK
    return pl.pallas_call(
        functools.partial(_nmm_kernel, widths=widths),
        grid=(t // tm,),
        in_specs=[pl.BlockSpec((tm, d), lambda i: (i, 0)),
                  pl.BlockSpec((1, d), lambda i: (0, 0)),
                  pl.BlockSpec((1, 1, d), lambda i: (rowfn(i), 0, 0)),
                  pl.BlockSpec((1, 1, d), lambda i: (rowfn(i) + 1, 0, 0)),
                  pl.BlockSpec((d, n), lambda i: (0, 0))],
        out_specs=[pl.BlockSpec((tm, wd), lambda i: (i, 0)) for wd in widths],
        out_shape=[jax.ShapeDtypeStruct((t, wd), F32) for wd in widths],
        compiler_params=_cparams(("arbitrary",)),
        name="norm_mod_proj",
    )(x, gain.reshape(1, d), tab, tab, w)


def _conv_chunk(pad_ref, pbase, w):
    blk = pad_ref[pl.ds(pbase - HALO, CHUNK + 2 * HALO), :]
    n = CHUNK + 2 * HALO
    lo, hi = HALO, HALO + CHUNK
    xm1 = pltpu.roll(blk, 1, 0)[lo:hi]
    x0 = blk[lo:hi]
    xp1 = pltpu.roll(blk, n - 1, 0)[lo:hi]
    xp2 = pltpu.roll(blk, n - 2, 0)[lo:hi]
    return w[0:1] * xm1 + w[1:2] * x0 + w[2:3] * xp1 + w[3:4] * xp2


def _fill_padded(pad_ref, src_ref, n_ctx, n_lat):
    width = pad_ref.shape[1]
    z = jnp.zeros((HALO, width), F32)
    pad_ref[0:HALO, :] = z
    pad_ref[HALO + n_ctx:2 * HALO + n_ctx, :] = z
    pad_ref[2 * HALO + n_ctx + n_lat:3 * HALO + n_ctx + n_lat, :] = z
    pad_ref[HALO:HALO + n_ctx, :] = src_ref[0, 0:n_ctx, :]
    pad_ref[2 * HALO + n_ctx:2 * HALO + n_ctx + n_lat, :] = src_ref[0, n_ctx:n_ctx + n_lat, :]


def _padded_base(c, ncc):
    return pl.multiple_of(HALO + c * CHUNK + jnp.where(c >= ncc, HALO, 0), SUBLANES)


def _lane_col(g, lane_idx):
    lane = _iota(g.shape, 1)
    return jnp.sum(jnp.where(lane == lane_idx, g, 0.0), axis=1, keepdims=True)


def _tri(d):
    r = _iota((CHUNK, CHUNK), 0)
    c = _iota((CHUNK, CHUNK), 1)
    incl = (r >= c) if d == 0 else (r <= c)
    strict = (r > c) if d == 0 else (r < c)
    return incl, strict


def _pair_diff(a_rows, b_rows):
    lane = _iota((CHUNK, LANES), 1)
    a = jnp.where(lane == 0, a_rows, jnp.where(lane == 1, 1.0, 0.0))
    b = jnp.where(lane == 0, 1.0, jnp.where(lane == 1, b_rows, 0.0))
    return _nt(a, b, HI)


def _bwd_chunk(s, ncc, nct):
    return jnp.where(s < ncc, ncc - 1 - s, nct + ncc - 1 - s)


def _gdn_kernel(q_ref, k_ref, v_ref, z_ref, ab_ref, cwq_ref, cwk_ref, cwv_ref, gp_ref, nw_ref, o_ref,
                pad_ref, qs_ref, ks_ref, vs_ref, g_ref, acc_ref, *, n_ctx, n_lat, n_heads):
    h = pl.program_id(1)
    ncc = n_ctx // CHUNK
    nct = (n_ctx + n_lat) // CHUNK
    dk = qs_ref.shape[1]

    def prep(src_ref, cw_ref, dst_ref, mode):
        _fill_padded(pad_ref, src_ref, n_ctx, n_lat)
        w = cw_ref[...]

        def body(c, carry):
            y = _silu(_conv_chunk(pad_ref, _padded_base(c, ncc), w))
            if mode != "v":
                y = y * lax.rsqrt(jnp.sum(y * y, axis=-1, keepdims=True) + EPS)
            if mode == "q":
                y = y * (dk ** -0.5)
            dst_ref[pl.ds(pl.multiple_of(c * CHUNK, CHUNK), CHUNK), :] = y
            return carry

        lax.fori_loop(0, nct, body, 0)

    prep(q_ref, cwq_ref, qs_ref, "q")
    prep(k_ref, cwk_ref, ks_ref, "k")
    prep(v_ref, cwv_ref, vs_ref, "v")

    neg_exp_alog = -jnp.exp(gp_ref[0:1, :])
    dtb = gp_ref[1:2, :]

    def gate_body(c, carry):
        r0 = pl.multiple_of(c * CHUNK, CHUNK)
        raw = ab_ref[0, pl.ds(r0, CHUNK), :]
        lane = _iota(raw.shape, 1)
        g_ref[pl.ds(r0, CHUNK), :] = jnp.where(lane < 2 * n_heads, neg_exp_alog * _softplus(raw + dtb),
                                               jax.nn.sigmoid(raw))
        acc_ref[pl.ds(r0, CHUNK), :] = jnp.zeros((CHUNK, dk), F32)
        return carry

    lax.fori_loop(0, nct, gate_body, 0)

    def chunk(c, d, s_state):
        r0 = pl.multiple_of(c * CHUNK, CHUNK)
        q = qs_ref[pl.ds(r0, CHUNK), :]
        k = ks_ref[pl.ds(r0, CHUNK), :]
        v = vs_ref[pl.ds(r0, CHUNK), :]
        gates = g_ref[pl.ds(r0, CHUNK), :]
        gcol = _lane_col(gates, d * n_heads + h)
        bcol = _lane_col(gates, 2 * n_heads + d * n_heads + h)
        incl, strict = _tri(d)
        gc = _mm(incl.astype(F32), jnp.broadcast_to(gcol, (CHUNK, LANES)), HI)
        gtot = gc[CHUNK - 1:CHUNK, :] if d == 0 else gc[0:1, :]
        gamma = jnp.where(incl, jnp.exp(jnp.minimum(_pair_diff(gc, -gc), 0.0)), 0.0)
        m_low = jnp.where(strict, bcol * _nt(k, k, HI) * gamma, 0.0)
        eg = jnp.exp(gc)
        sol = jnp.concatenate([bcol * v, bcol * k * eg], axis=1)
        sol = sol - _mm(m_low, sol, HI)
        p = _mm(m_low, m_low, HI)
        for it in range(5):
            sol = sol + _mm(p, sol, HI)
            if it < 4:
                p = _mm(p, p, HI)
        u = sol[:, :dk]
        w = sol[:, dk:]
        qk = _nt(q, k, HI) * gamma
        v_new = u - _mm(w, s_state, HI)
        o = _mm(q * eg, s_state, HI) + _mm(qk, v_new, HI)
        k_dec = k * jnp.exp(gtot - gc)
        s_state = s_state * jnp.exp(gtot) + _tn(k_dec, v_new, HI)
        acc_ref[pl.ds(r0, CHUNK), :] += o
        return s_state

    def step(s, carry):
        s_f, s_b = carry
        s_f = chunk(s, 0, s_f)
        s_b = chunk(_bwd_chunk(s, ncc, nct), 1, s_b)
        return s_f, s_b

    zero = jnp.zeros((dk, dk), F32)
    lax.fori_loop(0, nct, step, (zero, zero))

    nw = nw_ref[...]

    def out_body(c, carry):
        r0 = pl.multiple_of(c * CHUNK, CHUNK)
        o = acc_ref[pl.ds(r0, CHUNK), :]
        y = o * lax.rsqrt(jnp.mean(o * o, axis=-1, keepdims=True) + EPS) * nw
        o_ref[0, pl.ds(r0, CHUNK), :] = y * _silu(z_ref[0, pl.ds(r0, CHUNK), :])
        return carry

    lax.fori_loop(0, nct, out_body, 0)


def _gdn(qkv, z, ab, conv_w, gparams, norm_w, n_ctx, n_lat, n_heads):
    bsz, L, _ = qkv.shape
    dk = LANES
    seq = lambda off: pl.BlockSpec((1, L, dk), lambda b, h: (b, 0, off + h))
    cw = lambda off: pl.BlockSpec((CONV_W, dk), lambda b, h: (0, off + h))
    lp = L + 3 * HALO
    return pl.pallas_call(
        functools.partial(_gdn_kernel, n_ctx=n_ctx, n_lat=n_lat, n_heads=n_heads),
        grid=(bsz, n_heads),
        in_specs=[seq(0), seq(n_heads), seq(2 * n_heads),
                  pl.BlockSpec((1, L, dk), lambda b, h: (b, 0, h)),
                  pl.BlockSpec((1, L, LANES), lambda b, h: (b, 0, 0)),
                  cw(0), cw(n_heads), cw(2 * n_heads),
                  pl.BlockSpec((SUBLANES, LANES), lambda b, h: (0, 0)),
                  pl.BlockSpec((1, dk), lambda b, h: (0, 0))],
        out_specs=pl.BlockSpec((1, L, dk), lambda b, h: (b, 0, h)),
        out_shape=jax.ShapeDtypeStruct((bsz, L, n_heads * dk), F32),
        scratch_shapes=[pltpu.VMEM((lp, dk), F32)] + [pltpu.VMEM((L, dk), F32) for _ in range(5)],
        compiler_params=_cparams(("arbitrary", "arbitrary")),
        name="gdn_mixer",
    )(qkv, qkv, qkv, z, ab, conv_w, conv_w, conv_w, gparams, norm_w.reshape(1, dk))


def _lru_kernel(x_ref, gt_ref, cw_ref, cb_ref, wa_ref, wx_ref, pv_ref, o_ref,
                pad_ref, xc_ref, a_ref, b_ref, acc_ref, *, n_ctx, n_lat):
    ncc = n_ctx // CHUNK
    nct = (n_ctx + n_lat) // CHUNK
    L = n_ctx + n_lat
    width = xc_ref.shape[1]
    _fill_padded(pad_ref, x_ref, n_ctx, n_lat)
    w = cw_ref[...]
    cb = cb_ref[...]

    def conv_body(c, carry):
        r0 = pl.multiple_of(c * CHUNK, CHUNK)
        xc_ref[pl.ds(r0, CHUNK), :] = _conv_chunk(pad_ref, _padded_base(c, ncc), w) + cb
        acc_ref[pl.ds(r0, CHUNK), :] = jnp.zeros((CHUNK, width), F32)
        return carry

    lax.fori_loop(0, nct, conv_body, 0)

    row = _iota((SUBLANES, width), 0)
    for d in range(2):
        ba = pv_ref[d, 0:1, :]
        bx = pv_ref[d, 1:2, :]
        decay = -LRU_C * _softplus(-pv_ref[d, 2:3, :])

        def coef_body(c, carry, d=d, ba=ba, bx=bx, decay=decay):
            r0 = pl.multiple_of(c * CHUNK, CHUNK)
            x = xc_ref[pl.ds(r0, CHUNK), :]
            r = jax.nn.sigmoid(_mm(x, wa_ref[d, 0], HI) + ba)
            i = jax.nn.sigmoid(_mm(x, wx_ref[d, 0], HI) + bx)
            log_a = decay * r
            a = jnp.exp(log_a)
            a_ref[pl.ds(r0, CHUNK), :] = a
            th = jnp.tanh(log_a)
            b_ref[pl.ds(r0, CHUNK), :] = jnp.sqrt(-2.0 * th / (1.0 - th)) * (i * x)
            return carry

        lax.fori_loop(0, nct, coef_body, 0)

        nt_ctx = n_ctx // SUBLANES
        nt_all = L // SUBLANES

        def scan_body(s, hc, d=d):
            if d == 0:
                t = s
            else:
                t = jnp.where(s < nt_ctx, nt_ctx - 1 - s, nt_all + nt_ctx - 1 - s)
            r0 = pl.multiple_of(t * SUBLANES, SUBLANES)
            a = a_ref[pl.ds(r0, SUBLANES), :]
            b = b_ref[pl.ds(r0, SUBLANES), :]
            for sh in (1, 2, 4):
                if d == 0:
                    a_s = pltpu.roll(a, sh, 0)
                    b_s = pltpu.roll(b, sh, 0)
                    m = row >= sh
                else:
                    a_s = pltpu.roll(a, SUBLANES - sh, 0)
                    b_s = pltpu.roll(b, SUBLANES - sh, 0)
                    m = row < SUBLANES - sh
                b = jnp.where(m, a * b_s + b, b)
                a = jnp.where(m, a * a_s, a)
            hcur = b + a * hc
            acc_ref[pl.ds(r0, SUBLANES), :] += hcur
            last = hcur[SUBLANES - 1:SUBLANES, :] if d == 0 else hcur[0:1, :]
            return jnp.broadcast_to(last, (SUBLANES, width))

        lax.fori_loop(0, nt_all, scan_body, jnp.zeros((SUBLANES, width), F32))

    def out_body(c, carry):
        r0 = pl.multiple_of(c * CHUNK, CHUNK)
        o_ref[0, pl.ds(r0, CHUNK), :] = acc_ref[pl.ds(r0, CHUNK), :] * _gelu_tanh(gt_ref[0, pl.ds(r0, CHUNK), :])
        return carry

    lax.fori_loop(0, nct, out_body, 0)


def _lru(xb, gate, conv_w, conv_b, wa, wx, pvec, n_ctx, n_lat):
    bsz, L, width = xb.shape
    ng = width // LANES
    lp = L + 3 * HALO
    seq = pl.BlockSpec((1, L, LANES), lambda b, j: (b, 0, j))
    return pl.pallas_call(
        functools.partial(_lru_kernel, n_ctx=n_ctx, n_lat=n_lat),
        grid=(bsz, ng),
        in_specs=[seq, seq,
                  pl.BlockSpec((CONV_W, LANES), lambda b, j: (0, j)),
                  pl.BlockSpec((1, LANES), lambda b, j: (0, j)),
                  pl.BlockSpec((2, 1, LANES, LANES), lambda b, j: (0, j, 0, 0)),
                  pl.BlockSpec((2, 1, LANES, LANES), lambda b, j: (0, j, 0, 0)),
                  pl.BlockSpec((2, SUBLANES, LANES), lambda b, j: (0, 0, j))],
        out_specs=seq,
        out_shape=jax.ShapeDtypeStruct((bsz, L, width), F32),
        scratch_shapes=[pltpu.VMEM((lp, LANES), F32)] + [pltpu.VMEM((L, LANES), F32) for _ in range(4)],
        compiler_params=_cparams(("arbitrary", "arbitrary")),
        name="lru_mixer",
    )(xb, gate, conv_w, conv_b.reshape(1, width), wa, wx, pvec)


def _mlstm_kernel(qc_ref, kc_ref, vc_ref, gc_ref, ql_ref, kl_ref, vl_ref, gl_ref, og_ref, gp_ref, nw_ref, o_ref,
                  lg_ref, acc_ref, *, n_ctx, n_lat, n_heads, dqk):
    h = pl.program_id(1)
    ncc = n_ctx // CHUNK
    ncl = n_lat // CHUNK
    dv = LANES
    ibias = gp_ref[0:1, :]
    fbias = gp_ref[1:2, :]

    def gate_prep(src_ref, base, nchunks):
        def body(c, carry):
            r0 = pl.multiple_of(c * CHUNK, CHUNK)
            raw = src_ref[0, pl.ds(r0, CHUNK), :]
            lane = _iota(raw.shape, 1)
            lg_ref[pl.ds(base + r0, CHUNK), :] = jnp.where(lane < 2 * n_heads, raw + ibias, -_softplus(-(raw + fbias)))
            return carry
        lax.fori_loop(0, nchunks, body, 0)

    gate_prep(gc_ref, 0, ncc)
    gate_prep(gl_ref, n_ctx, ncl)

    def zero_body(c, carry):
        acc_ref[pl.ds(pl.multiple_of(c * CHUNK, CHUNK), CHUNK), :] = jnp.zeros((CHUNK, dv), F32)
        return carry

    lax.fori_loop(0, ncl, zero_body, 0)

    def chunk(q_ref, k_ref, v_ref, gbase, c, d, state, with_out):
        cx, m_s = state
        r0 = pl.multiple_of(c * CHUNK, CHUNK)
        q = q_ref[0, pl.ds(r0, CHUNK), :]
        k = k_ref[0, pl.ds(r0, CHUNK), :] * (dqk ** -0.5)
        v = v_ref[0, pl.ds(r0, CHUNK), :]
        gates = lg_ref[pl.ds(gbase + r0, CHUNK), :]
        li = jnp.broadcast_to(_lane_col(gates, d * n_heads + h), (CHUNK, LANES))
        lf = jnp.broadcast_to(_lane_col(gates, 2 * n_heads + d * n_heads + h), (CHUNK, LANES))
        incl, _ = _tri(d)
        bcum = _mm(incl.astype(F32), lf, HI)
        btot = bcum[CHUNK - 1:CHUNK, :] if d == 0 else bcum[0:1, :]
        w_end = btot - bcum + li
        m_chunk = jnp.max(w_end, axis=0, keepdims=True)
        e_end = jnp.exp(w_end - m_chunk)
        lane = _iota((CHUNK, LANES), 1)
        v_ext = jnp.concatenate([v, jnp.where(lane == 0, 1.0, 0.0)], axis=1)
        c_chunk = _tn(k * e_end, v_ext, HI)
        m_new = jnp.maximum(btot + m_s, m_chunk)
        a_old = jnp.exp(btot + m_s - m_new)
        a_new = jnp.exp(m_chunk - m_new)
        cx_new = a_old[:, 0:1] * cx + a_new[:, 0:1] * c_chunk
        if with_out:
            log_d = jnp.where(incl, _pair_diff(bcum, li - bcum), NEG_BIG)
            m_intra = jnp.max(log_d, axis=1, keepdims=True)
            s_qk = _nt(q, k, HI) * jnp.exp(log_d - m_intra)
            m_inter = bcum + m_s
            m_tot = jnp.maximum(m_inter, m_intra)
            w_inter = jnp.exp(m_inter - m_tot)
            w_intra = jnp.exp(m_intra - m_tot)
            p1 = _mm(q, cx, HI)
            p2 = _mm(s_qk, v_ext, HI)
            num = w_inter * p1[:, :dv] + w_intra * p2[:, :dv]
            den = (w_inter * p1[:, dv:] + w_intra * p2[:, dv:])[:, 0:1]
            acc_ref[pl.ds(r0, CHUNK), :] += num / jnp.maximum(jnp.abs(den), jnp.exp(-m_tot))
        return cx_new, m_new

    zero = (jnp.zeros((LANES, 2 * dv), F32), jnp.zeros((1, LANES), F32))

    def ctx_step(s, carry):
        st_f, st_b = carry
        st_f = chunk(qc_ref, kc_ref, vc_ref, 0, s, 0, st_f, False)
        st_b = chunk(qc_ref, kc_ref, vc_ref, 0, ncc - 1 - s, 1, st_b, False)
        return st_f, st_b

    carry = lax.fori_loop(0, ncc, ctx_step, (zero, zero))

    def lat_step(s, carry):
        st_f, st_b = carry
        st_f = chunk(ql_ref, kl_ref, vl_ref, n_ctx, s, 0, st_f, True)
        st_b = chunk(ql_ref, kl_ref, vl_ref, n_ctx, ncl - 1 - s, 1, st_b, True)
        return st_f, st_b

    lax.fori_loop(0, ncl, lat_step, carry)

    nw = nw_ref[...]

    def out_body(c, carry):
        r0 = pl.multiple_of(c * CHUNK, CHUNK)
        hs = acc_ref[pl.ds(r0, CHUNK), :]
        y = hs * lax.rsqrt(jnp.mean(hs * hs, axis=-1, keepdims=True) + EPS) * nw
        o_ref[0, pl.ds(r0, CHUNK), :] = y * jax.nn.sigmoid(og_ref[0, pl.ds(r0, CHUNK), :])
        return carry

    lax.fori_loop(0, ncl, out_body, 0)


def _mlstm(qc, kc, vc, gc, ql, kl, vl, gl, og, gparams, norm_w, n_heads, dqk):
    bsz, n_ctx, _ = qc.shape
    n_lat = ql.shape[1]
    cs = pl.BlockSpec((1, n_ctx, LANES), lambda b, h: (b, 0, h))
    ls = pl.BlockSpec((1, n_lat, LANES), lambda b, h: (b, 0, h))
    return pl.pallas_call(
        functools.partial(_mlstm_kernel, n_ctx=n_ctx, n_lat=n_lat, n_heads=n_heads, dqk=dqk),
        grid=(bsz, n_heads),
        in_specs=[cs, cs, cs, pl.BlockSpec((1, n_ctx, LANES), lambda b, h: (b, 0, 0)),
                  ls, ls, ls, pl.BlockSpec((1, n_lat, LANES), lambda b, h: (b, 0, 0)),
                  ls,
                  pl.BlockSpec((SUBLANES, LANES), lambda b, h: (0, 0)),
                  pl.BlockSpec((1, LANES), lambda b, h: (0, h))],
        out_specs=ls,
        out_shape=jax.ShapeDtypeStruct((bsz, n_lat, n_heads * LANES), F32),
        scratch_shapes=[pltpu.VMEM((n_ctx + n_lat, LANES), F32), pltpu.VMEM((n_lat, LANES), F32)],
        compiler_params=_cparams(("arbitrary", "arbitrary")),
        name="mlstm_mixer",
    )(qc, kc, vc, gc, ql, kl, vl, gl, og, gparams, norm_w.reshape(1, -1))


def _pool_kernel(x_ref, w_ref, sc_ref, o_ref, *, seg):
    tm = x_ref.shape[0]
    r = _iota((tm, tm), 0)
    c = _iota((tm, tm), 1)
    sh = seg.bit_length() - 1
    same = lax.shift_right_logical(r, sh) == lax.shift_right_logical(c, sh)
    t = r & (seg - 1)
    s = c & (seg - 1)
    for gi, wsz in enumerate(POOL_SIZES):
        lo = jnp.maximum(t - wsz // 2, 0)
        hi = jnp.minimum(t - wsz // 2 + wsz, seg)
        inwin = same & (s >= lo) & (s < hi)
        pmat = jnp.where(inwin, 1.0 / (hi - lo).astype(F32), 0.0) - jnp.where(r == c, 1.0, 0.0)
        x = x_ref[:, gi * LANES:(gi + 1) * LANES]
        pooled = _mm(pmat, x, HI)
        o_ref[:, gi * LANES:(gi + 1) * LANES] = _mm(pooled, w_ref[gi], HI) * sc_ref[:, gi * LANES:(gi + 1) * LANES]


def _pool(x, w_grp, scale, seg):
    t, width = x.shape
    tm = TOK_BLK
    assert seg & (seg - 1) == 0 and tm % seg == 0
    return pl.pallas_call(
        functools.partial(_pool_kernel, seg=seg),
        grid=(t // tm,),
        in_specs=[pl.BlockSpec((tm, width), lambda i: (i, 0)),
                  pl.BlockSpec(w_grp.shape, lambda i: (0, 0, 0)),
                  pl.BlockSpec((1, width), lambda i: (0, 0))],
        out_specs=pl.BlockSpec((tm, width), lambda i: (i, 0)),
        out_shape=jax.ShapeDtypeStruct((t, width), F32),
        compiler_params=_cparams(("arbitrary",)),
        name="pool_mixer",
    )(x, w_grp, scale.reshape(1, width))


def _outproj_kernel(x_ref, ya_ref, yb_ref, wa_ref, wb_ref, g1_ref, gn_ref, sh_ref, sc_ref, xo_ref, h_ref):
    y = _mm(ya_ref[...].astype(BF16), wa_ref[...]) + _mm(yb_ref[...].astype(BF16), wb_ref[...])
    xn = x_ref[...] + g1_ref[0] * y
    xo_ref[...] = xn
    hn = xn * lax.rsqrt(jnp.mean(xn * xn, axis=-1, keepdims=True) + EPS) * gn_ref[...]
    h_ref[...] = (hn * (1.0 + sc_ref[0]) + sh_ref[0]).astype(BF16)


def _outproj(x, ya, yb, w_out, gain, tab, rowfn):
    t, d = x.shape
    wa_n = ya.shape[1]
    tm = TOK_BLK
    w = w_out.astype(BF16)
    mod = lambda k: pl.BlockSpec((1, 1, d), lambda i: (rowfn(i) + k, 0, 0))
    return pl.pallas_call(
        _outproj_kernel,
        grid=(t // tm,),
        in_specs=[pl.BlockSpec((tm, d), lambda i: (i, 0)),
                  pl.BlockSpec((tm, wa_n), lambda i: (i, 0)),
                  pl.BlockSpec((tm, yb.shape[1]), lambda i: (i, 0)),
                  pl.BlockSpec((wa_n, d), lambda i: (0, 0)),
                  pl.BlockSpec((yb.shape[1], d), lambda i: (0, 0)),
                  mod(2),
                  pl.BlockSpec((1, d), lambda i: (0, 0)),
                  mod(3), mod(4)],
        out_specs=[pl.BlockSpec((tm, d), lambda i: (i, 0)), pl.BlockSpec((tm, d), lambda i: (i, 0))],
        out_shape=[jax.ShapeDtypeStruct((t, d), F32), jax.ShapeDtypeStruct((t, d), BF16)],
        compiler_params=_cparams(("arbitrary",)),
        name="out_proj",
    )(x, ya, yb, w[:wa_n], w[wa_n:], tab, gain.reshape(1, d), tab, tab)


def _hyperbola_pairs():
    return [(j1, j2) for j1 in range(PEER_TOPK) for j2 in range(PEER_TOPK) if (j1 + 1) * (j2 + 1) <= PEER_TOPK]


def _topk_rows(s, k):
    n, t = s.shape
    rowi = _iota((n, t), 0).astype(F32)
    rank = jnp.full((n, t), float(2 * k), F32)
    cur = s
    vals = []
    for j in range(k):
        m = jnp.max(cur, axis=0, keepdims=True)
        idx = jnp.min(jnp.where(cur == m, rowi, float(n)), axis=0, keepdims=True)
        sel = rowi == idx
        rank = jnp.where(sel, float(j), rank)
        cur = jnp.where(sel, -jnp.inf, cur)
        vals.append(m)
    return vals, rank


def _peer_select_kernel(h_ref, wq_ref, keys_ref, r2_ref, e2_ref, n1_ref, e1_ref):
    q = _mm(h_ref[...], wq_ref[...])
    s1 = _nt(keys_ref[0, 0], q[:, :N_KEYS], HI)
    s2 = _nt(keys_ref[0, 1], q[:, N_KEYS:], HI)
    v1, rank1 = _topk_rows(s1, PEER_TOPK)
    v2, rank2 = _topk_rows(s2, PEER_TOPK)
    pairs = _hyperbola_pairs()
    cand = jnp.concatenate([v1[j1] + v2[j2] for j1, j2 in pairs], axis=0)
    _, crank = _topk_rows(cand, PEER_TOPK)
    chosen = crank < PEER_TOPK
    cmax = v1[0] + v2[0]
    zsum = jnp.sum(jnp.where(chosen, jnp.exp(cand - cmax), 0.0), axis=0, keepdims=True)
    counts = jnp.where(chosen, 1.0, 0.0)
    n1 = jnp.zeros_like(rank1)
    row = 0
    for j1 in range(PEER_TOPK):
        width = PEER_TOPK // (j1 + 1)
        nj = jnp.sum(counts[row:row + width], axis=0, keepdims=True)
        n1 = jnp.where(rank1 == float(j1), nj, n1)
        row += width
    r2_ref[0] = rank2
    e2_ref[0] = jnp.where(rank2 < PEER_TOPK, jnp.exp(s2 - v2[0]), 0.0)
    n1_ref[0] = n1
    e1_ref[0] = jnp.where(rank1 < PEER_TOPK, jnp.exp(s1 - v1[0]), 0.0) / zsum


def _peer_select(h, wq, keys):
    t, d = h.shape
    nh = keys.shape[0]
    qd = wq.shape[1] // nh
    tm = TOK_BLK
    out = jax.ShapeDtypeStruct((nh, N_KEYS, t), F32)
    ospec = pl.BlockSpec((1, N_KEYS, tm), lambda i, hh: (hh, 0, i))
    return pl.pallas_call(
        _peer_select_kernel,
        grid=(t // tm, nh),
        in_specs=[pl.BlockSpec((tm, d), lambda i, hh: (i, 0)),
                  pl.BlockSpec((d, qd), lambda i, hh: (0, hh)),
                  pl.BlockSpec((1, 2, N_KEYS, qd // 2), lambda i, hh: (hh, 0, 0, 0))],
        out_specs=[ospec, ospec, ospec, ospec],
        out_shape=[out, out, out, out],
        compiler_params=_cparams(("arbitrary", "arbitrary")),
        name="peer_select",
    )(h, wq, keys)


def _peer_dense_kernel(h_ref, u_ref, vt_ref, r2_ref, e2_ref, n1_ref, e1_ref, x_ref, g2_ref, fn_ref, o_ref,
                       acc_ref, act_ref, wact_ref, *, n_heads, final):
    eb = pl.program_id(1)

    @pl.when(eb == 0)
    def _():
        acc_ref[...] = jnp.zeros_like(acc_ref)

    act_ref[...] = _nt(u_ref[...], h_ref[...])
    rows_per = EXP_BLK // N_KEYS
    i1_base = pl.multiple_of(eb * rows_per, rows_per)
    n1_tiles = [n1_ref[hh, pl.ds(i1_base, rows_per), :] for hh in range(n_heads)]
    e1_tiles = [e1_ref[hh, pl.ds(i1_base, rows_per), :] for hh in range(n_heads)]
    for r in range(rows_per):
        wgt = None
        for hh in range(n_heads):
            term = jnp.where(r2_ref[hh] < n1_tiles[hh][r:r + 1, :], e2_ref[hh], 0.0) * e1_tiles[hh][r:r + 1, :]
            wgt = term if wgt is None else wgt + term
        a = act_ref[r * N_KEYS:(r + 1) * N_KEYS, :]
        wact_ref[r * N_KEYS:(r + 1) * N_KEYS, :] = (_gelu_tanh(a) * wgt).astype(BF16)
    acc_ref[...] += _mm(vt_ref[...], wact_ref[...])

    @pl.when(eb == pl.num_programs(1) - 1)
    def _():
        y = x_ref[...] + g2_ref[0] * acc_ref[...].T
        if final:
            y = y * lax.rsqrt(jnp.mean(y * y, axis=-1, keepdims=True) + EPS) * fn_ref[...]
        o_ref[...] = y


def _peer_dense(h, u, vt, r2, e2, n1, e1, x, tab, rowfn, final_gain, final):
    t, d = h.shape
    ne = u.shape[0]
    nh = r2.shape[0]
    tm = TOK_BLK
    sel = pl.BlockSpec((nh, N_KEYS, tm), lambda i, e: (0, 0, i))
    return pl.pallas_call(
        functools.partial(_peer_dense_kernel, n_heads=nh, final=final),
        grid=(t // tm, ne // EXP_BLK),
        in_specs=[pl.BlockSpec((tm, d), lambda i, e: (i, 0)),
                  pl.BlockSpec((EXP_BLK, d), lambda i, e: (e, 0)),
                  pl.BlockSpec((d, EXP_BLK), lambda i, e: (0, e)),
                  sel, sel, sel, sel,
                  pl.BlockSpec((tm, d), lambda i, e: (i, 0)),
                  pl.BlockSpec((1, 1, d), lambda i, e: (rowfn(i) + 5, 0, 0)),
                  pl.BlockSpec((1, d), lambda i, e: (0, 0))],
        out_specs=pl.BlockSpec((tm, d), lambda i, e: (i, 0)),
        out_shape=jax.ShapeDtypeStruct((t, d), F32),
        scratch_shapes=[pltpu.VMEM((d, tm), F32), pltpu.VMEM((EXP_BLK, tm), F32), pltpu.VMEM((EXP_BLK, tm), BF16)],
        compiler_params=_cparams(("arbitrary", "arbitrary")),
        name="peer_dense",
    )(h, u, vt, r2, e2, n1, e1, x, tab, final_gain.reshape(1, d))


def _peer(h, x, wq, keys, u_tab, v_tab, tab, rowfn, final_gain, final):
    r2, e2, n1, e1 = _peer_select(h, wq.astype(BF16), keys)
    return _peer_dense(h, u_tab.astype(BF16), v_tab.T.astype(BF16), r2, e2, n1, e1, x, tab, rowfn, final_gain, final)


def _lane_vec(vals, rows=SUBLANES):
    out = jnp.zeros((rows, LANES), F32)
    for r, v in enumerate(vals):
        out = out.at[r, :v.shape[0]].set(v.astype(F32))
    return out


def _block_diag_groups(w):
    two, nb, bd, _ = w.shape
    per = LANES // bd
    wg = w.reshape(two, nb // per, per, bd, bd)
    eye = jnp.eye(per, dtype=w.dtype)
    return jnp.einsum("dgpij,pq->dgpiqj", wg, eye).reshape(two, nb // per, LANES, LANES)


def kernel(x, c, ctx, c_ctx, ada_w, ada_b, norm_mix, norm_ffn, final_norm, peer_wq, peer_keys, peer_u, peer_v,
           ev_w_in, ev_w_out, a_conv, a_alog, a_dtb, a_norm, b_conv_w, b_conv_b, b_wa, b_ba, b_wx, b_bx, b_lam,
           od_w_in, od_w_out, c_ibias, c_fbias, c_norm, d_w, d_scale):
    bsz, seq, dm = x.shape
    n_ctx = ctx.shape[1]
    L = n_ctx + seq
    rows = seq // GRID_W
    blk_per = L // TOK_BLK

    tab0 = _mod_table(c, c_ctx, ada_w[0], ada_b[0])
    row0 = lambda i: ((i // blk_per) * 2 + jnp.minimum(i % blk_per, 1)) * 6
    xcat = jnp.concatenate([ctx, x], axis=1).reshape(bsz * L, dm)

    a_heads = a_alog.shape[-1]
    a_width = a_heads * LANES
    a_qkv = 3 * a_width
    b_width = b_lam.shape[-1] * b_lam.shape[-2]
    w_in = ev_w_in[0]
    a_cols = a_qkv + a_width + 4 * a_heads
    w_ab = jnp.pad(w_in[:, a_qkv + a_width:a_cols], ((0, 0), (0, LANES - 4 * a_heads)))
    w0 = jnp.concatenate([w_in[:, :a_qkv], w_in[:, a_qkv:a_qkv + a_width], w_in[:, a_cols:a_cols + b_width],
                          w_in[:, a_cols + b_width:], w_ab], axis=1).astype(BF16)
    qkv, z, xb, gate, ab = _norm_mod_proj(xcat, norm_mix[0], tab0, row0, w0, (a_qkv, a_width, b_width, b_width, LANES))

    gparams = _lane_vec([a_alog[0].reshape(-1), a_dtb[0].reshape(-1)])
    ya = _gdn(qkv.reshape(bsz, L, a_qkv), z.reshape(bsz, L, a_width), ab.reshape(bsz, L, LANES),
              a_conv[0], gparams, a_norm[0], n_ctx, seq, a_heads)

    pvec = jnp.zeros((2, SUBLANES, b_width), F32)
    pvec = pvec.at[:, 0].set(b_ba[0].reshape(2, b_width)).at[:, 1].set(b_bx[0].reshape(2, b_width))
    pvec = pvec.at[:, 2].set(b_lam[0].reshape(2, b_width))
    yb = _lru(xb.reshape(bsz, L, b_width), gate.reshape(bsz, L, b_width), b_conv_w[0], b_conv_b[0],
              _block_diag_groups(b_wa[0]), _block_diag_groups(b_wx[0]), pvec, n_ctx, seq)

    x1, h1 = _outproj(xcat, ya.reshape(bsz * L, a_width), yb.reshape(bsz * L, b_width), ev_w_out[0], norm_ffn[0],
                      tab0, row0)
    x2 = _peer(h1, x1, peer_wq[0], peer_keys[0], peer_u[0], peer_v[0], tab0, row0, final_norm, False)
    x2 = x2.reshape(bsz, L, dm)

    tab1 = _mod_table(c, c_ctx, ada_w[1], ada_b[1])
    xc = x2[:, :n_ctx].reshape(bsz * n_ctx, dm)
    xl = x2[:, n_ctx:].reshape(bsz, rows, GRID_W, dm).transpose(0, 2, 1, 3).reshape(bsz * seq, dm)
    lat_per = seq // TOK_BLK
    ctx_per = n_ctx // TOK_BLK
    row_lat = lambda i: ((i // lat_per) * 2 + 1) * 6
    row_ctx = lambda i: ((i // ctx_per) * 2) * 6

    c_heads = c_ibias.shape[-1]
    c_width = c_norm.shape[-1]
    d_width = d_scale.shape[-1]
    w_in1 = od_w_in[0]
    n_state = w_in1.shape[1] - c_width - d_width
    c_qk = (n_state - c_width - 4 * c_heads) // 2
    dqk = c_qk // c_heads

    def pad_heads(wcols):
        return jnp.pad(wcols.reshape(dm, c_heads, dqk), ((0, 0), (0, 0), (0, LANES - dqk))).reshape(dm, c_heads * LANES)

    w_gate = jnp.pad(w_in1[:, 2 * c_qk + c_width:n_state], ((0, 0), (0, LANES - 4 * c_heads)))
    w1 = jnp.concatenate([pad_heads(w_in1[:, :c_qk]), pad_heads(w_in1[:, c_qk:2 * c_qk]),
                          w_in1[:, 2 * c_qk:2 * c_qk + c_width], w_in1[:, n_state:n_state + c_width],
                          w_in1[:, n_state + c_width:], w_gate], axis=1).astype(BF16)
    hw = c_heads * LANES
    widths1 = (hw, hw, c_width, c_width, d_width, LANES)
    ql, kl, vl, og, dl, gl = _norm_mod_proj(xl, norm_mix[1], tab1, row_lat, w1, widths1)
    qc, kc, vc, _, _, gc = _norm_mod_proj(xc, norm_mix[1], tab1, row_ctx, w1, widths1)

    gparams1 = _lane_vec([jnp.concatenate([c_ibias[0].reshape(-1), jnp.zeros((2 * c_heads,), F32)]),
                          jnp.concatenate([jnp.zeros((2 * c_heads,), F32), c_fbias[0].reshape(-1)])])
    r3 = lambda a, n: a.reshape(bsz, n, a.shape[-1])
    yc1 = _mlstm(r3(qc, n_ctx), r3(kc, n_ctx), r3(vc, n_ctx), r3(gc, n_ctx),
                 r3(ql, seq), r3(kl, seq), r3(vl, seq), r3(gl, seq), r3(og, seq), gparams1, c_norm[0], c_heads, dqk)
    yd1 = _pool(dl, d_w[0], d_scale[0], rows)

    x3, h3 = _outproj(xl, yc1.reshape(bsz * seq, c_width), yd1, od_w_out[0], norm_ffn[1], tab1, row_lat)
    out = _peer(h3, x3, peer_wq[1], peer_keys[1], peer_u[1], peer_v[1], tab1, row_lat, final_norm, True)
    return out.reshape(bsz, GRID_W, rows, dm).transpose(0, 2, 1, 3).reshape(bsz, seq, dm)
```

```python
import functools
import math

import jax
import jax.numpy as jnp
import numpy as np
from jax import lax
from jax.experimental import pallas as pl
from jax.experimental.pallas import tpu as pltpu

F32 = jnp.float32
BF16 = jnp.bfloat16
HI = lax.Precision.HIGHEST

EPS = 1e-6
GRID_W = 64
CHUNK = 64
LANES = 128
SUBLANES = 8
HALO = 8
CONV_W = 4
LRU_C = 8.0
PEER_TOPK = 16
N_KEYS = 128
POOL_SIZES = (2, 4, 8, 16)
TOK_BLK = 256
PEER_BLK = 2 * TOK_BLK
EXP_BLK = 1024
NEG_BIG = -1e30
VMEM_LIMIT = 48 * 1024 * 1024


def _cparams(sem):
    return pltpu.CompilerParams(dimension_semantics=sem, vmem_limit_bytes=VMEM_LIMIT)


def _nt(a, b, precision=None):
    return lax.dot_general(a, b, (((1,), (1,)), ((), ())), precision=precision, preferred_element_type=F32)


def _tn(a, b, precision=None):
    return lax.dot_general(a, b, (((0,), (0,)), ((), ())), precision=precision, preferred_element_type=F32)


def _mm(a, b, precision=None):
    return jnp.dot(a, b, precision=precision, preferred_element_type=F32)


def _silu(x):
    return x * jax.nn.sigmoid(x)


def _softplus(x):
    return jnp.maximum(x, 0.0) + jnp.log1p(jnp.exp(-jnp.abs(x)))


def _gelu_tanh(x):
    return 0.5 * x * (1.0 + jnp.tanh(math.sqrt(2.0 / math.pi) * (x + 0.044715 * (x * x * x))))


def _iota(shape, dim):
    return lax.broadcasted_iota(jnp.int32, shape, dim)


def _ada_kernel(c_ref, w_ref, b_ref, o_ref):
    o_ref[...] = _mm(_silu(c_ref[...]), w_ref[...], HI) + b_ref[...]


def _ada(cc, w, b):
    r, d = cc.shape
    n = w.shape[1]
    tn = 1536
    return pl.pallas_call(
        _ada_kernel,
        grid=(n // tn,),
        in_specs=[pl.BlockSpec((r, d), lambda j: (0, 0)),
                  pl.BlockSpec((d, tn), lambda j: (0, j)),
                  pl.BlockSpec((1, tn), lambda j: (0, j))],
        out_specs=pl.BlockSpec((r, tn), lambda j: (0, j)),
        out_shape=jax.ShapeDtypeStruct((r, n), F32),
        compiler_params=_cparams(("arbitrary",)),
        name="ada_mod",
    )(cc, w, b.reshape(1, n))


def _mod_table(c, c_ctx, w, b):
    bsz, d = c.shape
    rows = ((bsz + 1 + SUBLANES - 1) // SUBLANES) * SUBLANES
    cc = jnp.zeros((rows, d), F32).at[:bsz].set(c).at[bsz].set(c_ctx)
    m = _ada(cc, w, b).reshape(rows, 6, d)
    tab = jnp.stack([jnp.broadcast_to(m[bsz], (bsz, 6, d)), m[:bsz]], axis=1)
    return tab.reshape(bsz * 2 * 6, 1, d)


def _nmm_kernel(x_ref, g_ref, sh_ref, sc_ref, w_ref, *o_refs, widths):
    x = x_ref[...]
    y = x * lax.rsqrt(jnp.mean(x * x, axis=-1, keepdims=True) + EPS) * g_ref[...]
    h = (y * (1.0 + sc_ref[0]) + sh_ref[0]).astype(BF16)
    off = 0
    for o_ref, n in zip(o_refs, widths):
        o_ref[...] = _mm(h, w_ref[:, off:off + n])
        off += n


def _norm_mod_proj(x, gain, tab, rowfn, w, widths):
    t, d = x.shape
    n = w.shape[1]
    tm = TOK_BLK
    return pl.pallas_call(
        functools.partial(_nmm_kernel, widths=widths),
        grid=(t // tm,),
        in_specs=[pl.BlockSpec((tm, d), lambda i: (i, 0)),
                  pl.BlockSpec((1, d), lambda i: (0, 0)),
                  pl.BlockSpec((1, 1, d), lambda i: (rowfn(i), 0, 0)),
                  pl.BlockSpec((1, 1, d), lambda i: (rowfn(i) + 1, 0, 0)),
                  pl.BlockSpec((d, n), lambda i: (0, 0))],
        out_specs=[pl.BlockSpec((tm, wd), lambda i: (i, 0)) for wd in widths],
        out_shape=[jax.ShapeDtypeStruct((t, wd), F32) for wd in widths],
        compiler_params=_cparams(("arbitrary",)),
        name="norm_mod_proj",
    )(x, gain.reshape(1, d), tab, tab, w)


def _conv_chunk(pad_ref, pbase, w):
    blk = pad_ref[pl.ds(pbase - HALO, CHUNK + 2 * HALO), :]
    n = CHUNK + 2 * HALO
    lo, hi = HALO, HALO + CHUNK
    xm1 = pltpu.roll(blk, 1, 0)[lo:hi]
    x0 = blk[lo:hi]
    xp1 = pltpu.roll(blk, n - 1, 0)[lo:hi]
    xp2 = pltpu.roll(blk, n - 2, 0)[lo:hi]
    return w[0:1] * xm1 + w[1:2] * x0 + w[2:3] * xp1 + w[3:4] * xp2


def _fill_padded(pad_ref, src_ref, n_ctx, n_lat):
    width = pad_ref.shape[1]
    z = jnp.zeros((HALO, width), F32)
    pad_ref[0:HALO, :] = z
    pad_ref[HALO + n_ctx:2 * HALO + n_ctx, :] = z
    pad_ref[2 * HALO + n_ctx + n_lat:3 * HALO + n_ctx + n_lat, :] = z
    pad_ref[HALO:HALO + n_ctx, :] = src_ref[0, 0:n_ctx, :]
    pad_ref[2 * HALO + n_ctx:2 * HALO + n_ctx + n_lat, :] = src_ref[0, n_ctx:n_ctx + n_lat, :]


def _padded_base(c, ncc):
    return pl.multiple_of(HALO + c * CHUNK + jnp.where(c >= ncc, HALO, 0), SUBLANES)


def _lane_col(g, lane_idx):
    lane = _iota(g.shape, 1)
    return jnp.sum(jnp.where(lane == lane_idx, g, 0.0), axis=1, keepdims=True)


def _tri(d):
    r = _iota((CHUNK, CHUNK), 0)
    c = _iota((CHUNK, CHUNK), 1)
    incl = (r >= c) if d == 0 else (r <= c)
    strict = (r > c) if d == 0 else (r < c)
    return incl, strict


def _pair_diff(a_rows, b_rows):
    lane = _iota((CHUNK, LANES), 1)
    a = jnp.where(lane == 0, a_rows, jnp.where(lane == 1, 1.0, 0.0))
    b = jnp.where(lane == 0, 1.0, jnp.where(lane == 1, b_rows, 0.0))
    return _nt(a, b, HI)


def _bwd_chunk(s, ncc, nct):
    return jnp.where(s < ncc, ncc - 1 - s, nct + ncc - 1 - s)


def _gate_kernel(ab_ref, gp_ref, gc_ref, gr_ref, *, n_chunks, n_heads, kind):
    r = _iota((CHUNK, CHUNK), 0)
    c = _iota((CHUNK, CHUNK), 1)
    lower = (r >= c).astype(F32)
    upper = (r <= c).astype(F32)
    eye = (r == c).astype(F32)
    lane = _iota((CHUNK, LANES), 1)
    rowg = _iota((LANES, CHUNK), 0)
    if kind == "gdn":
        fwd_lo, rev_lo, rev_hi = 0, n_heads, 2 * n_heads
    else:
        fwd_lo, rev_lo, rev_hi = 2 * n_heads, 3 * n_heads, 4 * n_heads
    p0 = gp_ref[0:1, :]
    p1 = gp_ref[1:2, :]

    def body(ci, carry):
        r0 = pl.multiple_of(ci * CHUNK, CHUNK)
        raw = ab_ref[0, pl.ds(r0, CHUNK), :]
        if kind == "gdn":
            g = jnp.where(lane < 2 * n_heads, -jnp.exp(p0) * _softplus(raw + p1), jax.nn.sigmoid(raw))
        else:
            g = jnp.where(lane < 2 * n_heads, raw + p0, -_softplus(-(raw + p1)))
        is_fwd = (lane >= fwd_lo) & (lane < rev_lo)
        is_rev = (lane >= rev_lo) & (lane < rev_hi)
        gc_ref[0, pl.ds(r0, CHUNK), :] = jnp.where(is_fwd, _mm(lower, g, HI), jnp.where(is_rev, _mm(upper, g, HI), g))
        row_fwd = (rowg >= fwd_lo) & (rowg < rev_lo)
        row_rev = (rowg >= rev_lo) & (rowg < rev_hi)
        rows = jnp.where(row_fwd, _tn(g, upper, HI), jnp.where(row_rev, _tn(g, lower, HI), _tn(g, eye, HI)))
        gr_ref[0, ci] = rows[0:2 * SUBLANES, :]
        return carry

    lax.fori_loop(0, n_chunks, body, 0)


def _gates(ab, gparams, n_heads, kind):
    bsz, L, _ = ab.shape
    nct = L // CHUNK
    assert 4 * n_heads <= 2 * SUBLANES
    return pl.pallas_call(
        functools.partial(_gate_kernel, n_chunks=nct, n_heads=n_heads, kind=kind),
        grid=(bsz,),
        in_specs=[pl.BlockSpec((1, L, LANES), lambda b: (b, 0, 0)),
                  pl.BlockSpec((SUBLANES, LANES), lambda b: (0, 0))],
        out_specs=[pl.BlockSpec((1, L, LANES), lambda b: (b, 0, 0)),
                   pl.BlockSpec((1, nct, 2 * SUBLANES, CHUNK), lambda b: (b, 0, 0, 0))],
        out_shape=[jax.ShapeDtypeStruct((bsz, L, LANES), F32),
                   jax.ShapeDtypeStruct((bsz, nct, 2 * SUBLANES, CHUNK), F32)],
        compiler_params=_cparams(("arbitrary",)),
        name=kind + "_gates",
    )(ab, gparams)


def _mm3(a, b):
    ah = a.astype(BF16)
    al = (a - ah.astype(F32)).astype(BF16)
    bh = b.astype(BF16)
    bl = (b - bh.astype(F32)).astype(BF16)
    return _mm(ah, bh) + (_mm(ah, bl) + _mm(al, bh))


def _gdn_kernel(q_ref, k_ref, v_ref, z_ref, gc_ref, gr_ref, cwq_ref, cwk_ref, cwv_ref, nw_ref, o_ref,
                pad_ref, qs_ref, ks_ref, vs_ref, acc_ref, u_ref, w_ref, qd_ref, kd_ref, qk_ref, gl_ref,
                *, n_ctx, n_lat, n_heads):
    h = pl.program_id(1)
    ncc = n_ctx // CHUNK
    nct = (n_ctx + n_lat) // CHUNK
    dk = qs_ref.shape[1]

    def prep(src_ref, cw_ref, dst_ref, mode):
        _fill_padded(pad_ref, src_ref, n_ctx, n_lat)
        w = cw_ref[...]

        def body(c, carry):
            y = _silu(_conv_chunk(pad_ref, _padded_base(c, ncc), w))
            if mode != "v":
                y = y * lax.rsqrt(jnp.sum(y * y, axis=-1, keepdims=True) + EPS)
            if mode == "q":
                y = y * (dk ** -0.5)
            dst_ref[pl.ds(pl.multiple_of(c * CHUNK, CHUNK), CHUNK), :] = y
            return carry

        lax.fori_loop(0, nct, body, 0)

    prep(q_ref, cwq_ref, qs_ref, "q")
    prep(k_ref, cwk_ref, ks_ref, "k")
    prep(v_ref, cwv_ref, vs_ref, "v")

    n_chain = 4
    big = n_chain * CHUNK
    rr = _iota((big, big), 0)
    cc = _iota((big, big), 1)
    log_chunk = CHUNK.bit_length() - 1
    same_blk = lax.shift_right_logical(rr, log_chunk) == lax.shift_right_logical(cc, log_chunk)
    bwd_blk = (lax.shift_right_logical(rr, log_chunk) & 1) == 1
    fwd_blk = jnp.logical_not(bwd_blk)
    strict_bd = same_blk & ((bwd_blk & (rr < cc)) | (fwd_blk & (rr > cc)))
    incl_bd = same_blk & ((bwd_blk & (rr <= cc)) | (fwd_blk & (rr >= cc)))

    def prepare_group(c0):
        cols, ks, rhs, gcols, bcols, grows = [], [], [], [], [], []
        chains = []
        for j in range(2):
            c = c0 + j
            r0 = pl.multiple_of(c * CHUNK, CHUNK)
            acc_ref[pl.ds(r0, CHUNK), :] = jnp.zeros((CHUNK, dk), F32)
            q = qs_ref[pl.ds(r0, CHUNK), :]
            k = ks_ref[pl.ds(r0, CHUNK), :]
            v = vs_ref[pl.ds(r0, CHUNK), :]
            gates = gc_ref[0, pl.ds(r0, CHUNK), :]
            grt = gr_ref[0, c]
            for d in range(2):
                gcol = _lane_col(gates, d * n_heads + h)
                bcol = _lane_col(gates, 2 * n_heads + d * n_heads + h)
                grow = jnp.sum(jnp.where(_iota(grt.shape, 0) == d * n_heads + h, grt, 0.0), axis=0, keepdims=True)
                gtot = gcol[CHUNK - 1:CHUNK, :] if d == 0 else gcol[0:1, :]
                eg = jnp.exp(gcol)
                ks.append(k)
                gcols.append(gcol)
                bcols.append(bcol)
                grows.append(grow)
                rhs.append(jnp.concatenate([bcol * v, (bcol * eg) * k], axis=1))
                chains.append((c, d, r0, q, k, gcol, grow, gtot, eg))
        kst = jnp.concatenate(ks, axis=0).astype(BF16)
        gam = jnp.where(incl_bd, jnp.exp(jnp.minimum(jnp.concatenate(gcols, axis=0) - jnp.concatenate(grows, axis=1),
                                                      0.0)), 0.0)
        m_bd = jnp.where(strict_bd, jnp.concatenate(bcols, axis=0) * _nt(kst, kst) * gam, 0.0)
        sol = jnp.concatenate(rhs, axis=0)
        sol = sol - _mm3(m_bd, sol)
        p = _mm3(m_bd, m_bd)
        for it in range(5):
            sol = sol + _mm3(p, sol)
            if it < 4:
                p = _mm3(p, p)
        for b, (c, d, r0, q, k, gcol, grow, gtot, eg) in enumerate(chains):
            incl, _ = _tri(d)
            gamma = jnp.where(incl, jnp.exp(jnp.minimum(gcol - grow, 0.0)), 0.0)
            u_ref[d, pl.ds(r0, CHUNK), :] = sol[b * CHUNK:(b + 1) * CHUNK, :dk]
            w_ref[d, pl.ds(r0, CHUNK), :] = sol[b * CHUNK:(b + 1) * CHUNK, dk:].astype(BF16)
            qd_ref[d, pl.ds(r0, CHUNK), :] = (q * eg).astype(BF16)
            kd_ref[d, pl.ds(r0, CHUNK), :] = (k * jnp.exp(gtot - gcol)).astype(BF16)
            qk_ref[d, c] = (_nt(q.astype(BF16), k.astype(BF16)) * gamma).astype(BF16)
            gl_ref[d, c] = jnp.broadcast_to(jnp.exp(gtot), (SUBLANES, LANES))

    def prepare_body(c2, carry):
        prepare_group(2 * c2)
        return carry

    lax.fori_loop(0, nct // 2, prepare_body, 0)

    def advance(c, d, s_state):
        r0 = pl.multiple_of(c * CHUNK, CHUNK)
        sb = s_state.astype(BF16)
        v_new = u_ref[d, pl.ds(r0, CHUNK), :] - _mm(w_ref[d, pl.ds(r0, CHUNK), :], sb)
        vb = v_new.astype(BF16)
        acc_ref[pl.ds(r0, CHUNK), :] += _mm(qd_ref[d, pl.ds(r0, CHUNK), :], sb) + _mm(qk_ref[d, c], vb)
        return s_state * gl_ref[d, c][0:1, :] + _tn(kd_ref[d, pl.ds(r0, CHUNK), :], vb)

    def step(s, carry):
        s_f, s_b = carry
        return advance(s, 0, s_f), advance(_bwd_chunk(s, ncc, nct), 1, s_b)

    zero = jnp.zeros((dk, dk), F32)
    lax.fori_loop(0, nct, step, (zero, zero))

    nw = nw_ref[...]

    def out_body(c, carry):
        r0 = pl.multiple_of(c * CHUNK, CHUNK)
        o = acc_ref[pl.ds(r0, CHUNK), :]
        y = o * lax.rsqrt(jnp.mean(o * o, axis=-1, keepdims=True) + EPS) * nw
        o_ref[0, pl.ds(r0, CHUNK), :] = y * _silu(z_ref[0, pl.ds(r0, CHUNK), :])
        return carry

    lax.fori_loop(0, nct, out_body, 0)


def _gdn(qkv, z, ab, conv_w, gparams, norm_w, n_ctx, n_lat, n_heads):
    bsz, L, _ = qkv.shape
    dk = LANES
    seq = lambda off: pl.BlockSpec((1, L, dk), lambda b, h: (b, 0, off + h))
    cw = lambda off: pl.BlockSpec((CONV_W, dk), lambda b, h: (0, off + h))
    lp = L + 3 * HALO
    nct = L // CHUNK
    assert nct % 2 == 0
    gc, gr = _gates(ab, gparams, n_heads, "gdn")
    return pl.pallas_call(
        functools.partial(_gdn_kernel, n_ctx=n_ctx, n_lat=n_lat, n_heads=n_heads),
        grid=(bsz, n_heads),
        in_specs=[seq(0), seq(n_heads), seq(2 * n_heads),
                  pl.BlockSpec((1, L, dk), lambda b, h: (b, 0, h)),
                  pl.BlockSpec((1, L, LANES), lambda b, h: (b, 0, 0)),
                  pl.BlockSpec((1, nct, 2 * SUBLANES, CHUNK), lambda b, h: (b, 0, 0, 0)),
                  cw(0), cw(n_heads), cw(2 * n_heads),
                  pl.BlockSpec((1, dk), lambda b, h: (0, 0))],
        out_specs=pl.BlockSpec((1, L, dk), lambda b, h: (b, 0, h)),
        out_shape=jax.ShapeDtypeStruct((bsz, L, n_heads * dk), F32),
        scratch_shapes=[pltpu.VMEM((lp, dk), F32)] + [pltpu.VMEM((L, dk), F32) for _ in range(4)]
        + [pltpu.VMEM((2, L, dk), F32)] + [pltpu.VMEM((2, L, dk), BF16) for _ in range(3)]
        + [pltpu.VMEM((2, nct, CHUNK, CHUNK), BF16), pltpu.VMEM((2, nct, SUBLANES, LANES), F32)],
        compiler_params=_cparams(("arbitrary", "arbitrary")),
        name="gdn_mixer",
    )(qkv, qkv, qkv, z, gc, gr, conv_w, conv_w, conv_w, norm_w.reshape(1, dk))


def _lru_kernel(x_ref, gt_ref, cw_ref, cb_ref, wa_ref, wx_ref, pv_ref, o_ref,
                pad_ref, xc_ref, a_ref, b_ref, acc_ref, *, n_ctx, n_lat):
    ncc = n_ctx // CHUNK
    nct = (n_ctx + n_lat) // CHUNK
    L = n_ctx + n_lat
    width = xc_ref.shape[1]
    _fill_padded(pad_ref, x_ref, n_ctx, n_lat)
    w = cw_ref[...]
    cb = cb_ref[...]

    def conv_body(c, carry):
        r0 = pl.multiple_of(c * CHUNK, CHUNK)
        xc_ref[pl.ds(r0, CHUNK), :] = _conv_chunk(pad_ref, _padded_base(c, ncc), w) + cb
        acc_ref[pl.ds(r0, CHUNK), :] = jnp.zeros((CHUNK, width), F32)
        return carry

    lax.fori_loop(0, nct, conv_body, 0)

    row = _iota((SUBLANES, width), 0)
    for d in range(2):
        ba = pv_ref[d, 0:1, :]
        bx = pv_ref[d, 1:2, :]
        decay = -LRU_C * _softplus(-pv_ref[d, 2:3, :])

        def coef_body(c, carry, d=d, ba=ba, bx=bx, decay=decay):
            r0 = pl.multiple_of(c * CHUNK, CHUNK)
            x = xc_ref[pl.ds(r0, CHUNK), :]
            xb = x.astype(BF16)
            r = jax.nn.sigmoid(_mm(xb, wa_ref[d, 0]) + ba)
            i = jax.nn.sigmoid(_mm(xb, wx_ref[d, 0]) + bx)
            log_a = decay * r
            a = jnp.exp(log_a)
            a_ref[pl.ds(r0, CHUNK), :] = a
            th = jnp.tanh(log_a)
            b_ref[pl.ds(r0, CHUNK), :] = jnp.sqrt(-2.0 * th / (1.0 - th)) * (i * x)
            return carry

        lax.fori_loop(0, nct, coef_body, 0)

        nt_ctx = n_ctx // SUBLANES
        nt_all = L // SUBLANES

        def scan_body(s, hc, d=d):
            if d == 0:
                t = s
            else:
                t = jnp.where(s < nt_ctx, nt_ctx - 1 - s, nt_all + nt_ctx - 1 - s)
            r0 = pl.multiple_of(t * SUBLANES, SUBLANES)
            a = a_ref[pl.ds(r0, SUBLANES), :]
            b = b_ref[pl.ds(r0, SUBLANES), :]
            for sh in (1, 2, 4):
                if d == 0:
                    a_s = pltpu.roll(a, sh, 0)
                    b_s = pltpu.roll(b, sh, 0)
                    m = row >= sh
                else:
                    a_s = pltpu.roll(a, SUBLANES - sh, 0)
                    b_s = pltpu.roll(b, SUBLANES - sh, 0)
                    m = row < SUBLANES - sh
                b = jnp.where(m, a * b_s + b, b)
                a = jnp.where(m, a * a_s, a)
            hcur = b + a * hc
            acc_ref[pl.ds(r0, SUBLANES), :] += hcur
            last = hcur[SUBLANES - 1:SUBLANES, :] if d == 0 else hcur[0:1, :]
            return jnp.broadcast_to(last, (SUBLANES, width))

        lax.fori_loop(0, nt_all, scan_body, jnp.zeros((SUBLANES, width), F32))

    def out_body(c, carry):
        r0 = pl.multiple_of(c * CHUNK, CHUNK)
        o_ref[0, pl.ds(r0, CHUNK), :] = acc_ref[pl.ds(r0, CHUNK), :] * _gelu_tanh(gt_ref[0, pl.ds(r0, CHUNK), :])
        return carry

    lax.fori_loop(0, nct, out_body, 0)


def _lru(xb, gate, conv_w, conv_b, wa, wx, pvec, n_ctx, n_lat):
    bsz, L, width = xb.shape
    ng = width // LANES
    lp = L + 3 * HALO
    seq = pl.BlockSpec((1, L, LANES), lambda b, j: (b, 0, j))
    return pl.pallas_call(
        functools.partial(_lru_kernel, n_ctx=n_ctx, n_lat=n_lat),
        grid=(bsz, ng),
        in_specs=[seq, seq,
                  pl.BlockSpec((CONV_W, LANES), lambda b, j: (0, j)),
                  pl.BlockSpec((1, LANES), lambda b, j: (0, j)),
                  pl.BlockSpec((2, 1, LANES, LANES), lambda b, j: (0, j, 0, 0)),
                  pl.BlockSpec((2, 1, LANES, LANES), lambda b, j: (0, j, 0, 0)),
                  pl.BlockSpec((2, SUBLANES, LANES), lambda b, j: (0, 0, j))],
        out_specs=seq,
        out_shape=jax.ShapeDtypeStruct((bsz, L, width), F32),
        scratch_shapes=[pltpu.VMEM((lp, LANES), F32)] + [pltpu.VMEM((L, LANES), F32) for _ in range(4)],
        compiler_params=_cparams(("arbitrary", "arbitrary")),
        name="lru_mixer",
    )(xb, gate, conv_w, conv_b.reshape(1, width), wa, wx, pvec)


def _mlstm_kernel(qc_ref, kc_ref, vc_ref, gcc_ref, grc_ref, ql_ref, kl_ref, vl_ref, gcl_ref, grl_ref, og_ref, nw_ref,
                  o_ref, acc_ref, *, n_ctx, n_lat, n_heads, dqk):
    h = pl.program_id(1)
    ncc = n_ctx // CHUNK
    ncl = n_lat // CHUNK
    dv = LANES

    def zero_body(c, carry):
        acc_ref[pl.ds(pl.multiple_of(c * CHUNK, CHUNK), CHUNK), :] = jnp.zeros((CHUNK, dv), F32)
        return carry

    lax.fori_loop(0, ncl, zero_body, 0)

    def chunk(refs, c, d, state, with_out):
        q_ref, k_ref, v_ref, gcol_ref, grow_ref = refs
        cx, m_s = state
        r0 = pl.multiple_of(c * CHUNK, CHUNK)
        q = q_ref[0, pl.ds(r0, CHUNK), :].astype(BF16)
        k = k_ref[0, pl.ds(r0, CHUNK), :] * (dqk ** -0.5)
        v = v_ref[0, pl.ds(r0, CHUNK), :]
        gates = gcol_ref[0, pl.ds(r0, CHUNK), :]
        li = _lane_col(gates, d * n_heads + h)
        bcum = _lane_col(gates, 2 * n_heads + d * n_heads + h)
        btot = bcum[CHUNK - 1:CHUNK, :] if d == 0 else bcum[0:1, :]
        w_end = btot - bcum + li
        m_chunk = jnp.max(w_end, axis=0, keepdims=True)
        e_end = jnp.exp(w_end - m_chunk)
        lane = _iota((CHUNK, LANES), 1)
        v_ext = jnp.concatenate([v, jnp.where(lane == 0, 1.0, 0.0)], axis=1).astype(BF16)
        c_chunk = _tn((k * e_end).astype(BF16), v_ext)
        m_new = jnp.maximum(btot + m_s, m_chunk)
        cx_new = jnp.exp(btot + m_s - m_new) * cx + jnp.exp(m_chunk - m_new) * c_chunk
        if with_out:
            grt = grow_ref[0, c]
            rowi = _iota(grt.shape, 0)
            li_row = jnp.sum(jnp.where(rowi == d * n_heads + h, grt, 0.0), axis=0, keepdims=True)
            b_row = jnp.sum(jnp.where(rowi == 2 * n_heads + d * n_heads + h, grt, 0.0), axis=0, keepdims=True)
            incl, _ = _tri(d)
            log_d = jnp.where(incl, bcum + (li_row - b_row), NEG_BIG)
            m_intra = jnp.max(log_d, axis=1, keepdims=True)
            s_qk = (_nt(q, k.astype(BF16)) * jnp.exp(log_d - m_intra)).astype(BF16)
            m_inter = bcum + m_s
            m_tot = jnp.maximum(m_inter, m_intra)
            w_inter = jnp.exp(m_inter - m_tot)
            w_intra = jnp.exp(m_intra - m_tot)
            p1 = _mm(q, cx.astype(BF16))
            p2 = _mm(s_qk, v_ext)
            num = w_inter * p1[:, :dv] + w_intra * p2[:, :dv]
            den = (w_inter * p1[:, dv:] + w_intra * p2[:, dv:])[:, 0:1]
            acc_ref[pl.ds(r0, CHUNK), :] += num / jnp.maximum(jnp.abs(den), jnp.exp(-m_tot))
        return cx_new, m_new

    zero = (jnp.zeros((LANES, 2 * dv), F32), jnp.zeros((1, 1), F32))
    ctx_refs = (qc_ref, kc_ref, vc_ref, gcc_ref, grc_ref)
    lat_refs = (ql_ref, kl_ref, vl_ref, gcl_ref, grl_ref)

    def ctx_step(s, carry):
        st_f, st_b = carry
        st_f = chunk(ctx_refs, s, 0, st_f, False)
        st_b = chunk(ctx_refs, ncc - 1 - s, 1, st_b, False)
        return st_f, st_b

    carry = lax.fori_loop(0, ncc, ctx_step, (zero, zero))

    def lat_step(s, carry):
        st_f, st_b = carry
        st_f = chunk(lat_refs, s, 0, st_f, True)
        st_b = chunk(lat_refs, ncl - 1 - s, 1, st_b, True)
        return st_f, st_b

    lax.fori_loop(0, ncl, lat_step, carry)

    nw = nw_ref[...]

    def out_body(c, carry):
        r0 = pl.multiple_of(c * CHUNK, CHUNK)
        hs = acc_ref[pl.ds(r0, CHUNK), :]
        y = hs * lax.rsqrt(jnp.mean(hs * hs, axis=-1, keepdims=True) + EPS) * nw
        o_ref[0, pl.ds(r0, CHUNK), :] = y * jax.nn.sigmoid(og_ref[0, pl.ds(r0, CHUNK), :])
        return carry

    lax.fori_loop(0, ncl, out_body, 0)


def _mlstm(qc, kc, vc, gc, ql, kl, vl, gl, og, gparams, norm_w, n_heads, dqk):
    bsz, n_ctx, _ = qc.shape
    n_lat = ql.shape[1]
    cs = pl.BlockSpec((1, n_ctx, LANES), lambda b, h: (b, 0, h))
    ls = pl.BlockSpec((1, n_lat, LANES), lambda b, h: (b, 0, h))
    gcc, grc = _gates(gc, gparams, n_heads, "mlstm")
    gcl, grl = _gates(gl, gparams, n_heads, "mlstm")
    row_spec = lambda n: pl.BlockSpec((1, n // CHUNK, 2 * SUBLANES, CHUNK), lambda b, h: (b, 0, 0, 0))
    return pl.pallas_call(
        functools.partial(_mlstm_kernel, n_ctx=n_ctx, n_lat=n_lat, n_heads=n_heads, dqk=dqk),
        grid=(bsz, n_heads),
        in_specs=[cs, cs, cs, pl.BlockSpec((1, n_ctx, LANES), lambda b, h: (b, 0, 0)), row_spec(n_ctx),
                  ls, ls, ls, pl.BlockSpec((1, n_lat, LANES), lambda b, h: (b, 0, 0)), row_spec(n_lat),
                  ls,
                  pl.BlockSpec((1, LANES), lambda b, h: (0, h))],
        out_specs=ls,
        out_shape=jax.ShapeDtypeStruct((bsz, n_lat, n_heads * LANES), F32),
        scratch_shapes=[pltpu.VMEM((n_lat, LANES), F32)],
        compiler_params=_cparams(("arbitrary", "arbitrary")),
        name="mlstm_mixer",
    )(qc, kc, vc, gcc, grc, ql, kl, vl, gcl, grl, og, norm_w.reshape(1, -1))


def _pool_kernel(x_ref, w_ref, sc_ref, o_ref, *, seg):
    tm = x_ref.shape[0]
    r = _iota((tm, tm), 0)
    c = _iota((tm, tm), 1)
    sh = seg.bit_length() - 1
    same = lax.shift_right_logical(r, sh) == lax.shift_right_logical(c, sh)
    t = r & (seg - 1)
    s = c & (seg - 1)
    for gi, wsz in enumerate(POOL_SIZES):
        lo = jnp.maximum(t - wsz // 2, 0)
        hi = jnp.minimum(t - wsz // 2 + wsz, seg)
        inwin = same & (s >= lo) & (s < hi)
        pmat = jnp.where(inwin, 1.0 / (hi - lo).astype(F32), 0.0) - jnp.where(r == c, 1.0, 0.0)
        x = x_ref[:, gi * LANES:(gi + 1) * LANES]
        pooled = _mm(pmat, x, HI)
        o_ref[:, gi * LANES:(gi + 1) * LANES] = _mm(pooled, w_ref[gi], HI) * sc_ref[:, gi * LANES:(gi + 1) * LANES]


def _pool(x, w_grp, scale, seg):
    t, width = x.shape
    tm = TOK_BLK
    assert seg & (seg - 1) == 0 and tm % seg == 0
    return pl.pallas_call(
        functools.partial(_pool_kernel, seg=seg),
        grid=(t // tm,),
        in_specs=[pl.BlockSpec((tm, width), lambda i: (i, 0)),
                  pl.BlockSpec(w_grp.shape, lambda i: (0, 0, 0)),
                  pl.BlockSpec((1, width), lambda i: (0, 0))],
        out_specs=pl.BlockSpec((tm, width), lambda i: (i, 0)),
        out_shape=jax.ShapeDtypeStruct((t, width), F32),
        compiler_params=_cparams(("arbitrary",)),
        name="pool_mixer",
    )(x, w_grp, scale.reshape(1, width))


def _outproj_kernel(x_ref, ya_ref, yb_ref, wa_ref, wb_ref, g1_ref, gn_ref, sh_ref, sc_ref, xo_ref, h_ref):
    y = _mm(ya_ref[...].astype(BF16), wa_ref[...]) + _mm(yb_ref[...].astype(BF16), wb_ref[...])
    xn = x_ref[...] + g1_ref[0] * y
    xo_ref[...] = xn
    hn = xn * lax.rsqrt(jnp.mean(xn * xn, axis=-1, keepdims=True) + EPS) * gn_ref[...]
    h_ref[...] = (hn * (1.0 + sc_ref[0]) + sh_ref[0]).astype(BF16)


def _outproj(x, ya, yb, w_out, gain, tab, rowfn):
    t, d = x.shape
    wa_n = ya.shape[1]
    tm = TOK_BLK
    w = w_out.astype(BF16)
    mod = lambda k: pl.BlockSpec((1, 1, d), lambda i: (rowfn(i) + k, 0, 0))
    return pl.pallas_call(
        _outproj_kernel,
        grid=(t // tm,),
        in_specs=[pl.BlockSpec((tm, d), lambda i: (i, 0)),
                  pl.BlockSpec((tm, wa_n), lambda i: (i, 0)),
                  pl.BlockSpec((tm, yb.shape[1]), lambda i: (i, 0)),
                  pl.BlockSpec((wa_n, d), lambda i: (0, 0)),
                  pl.BlockSpec((yb.shape[1], d), lambda i: (0, 0)),
                  mod(2),
                  pl.BlockSpec((1, d), lambda i: (0, 0)),
                  mod(3), mod(4)],
        out_specs=[pl.BlockSpec((tm, d), lambda i: (i, 0)), pl.BlockSpec((tm, d), lambda i: (i, 0))],
        out_shape=[jax.ShapeDtypeStruct((t, d), F32), jax.ShapeDtypeStruct((t, d), BF16)],
        compiler_params=_cparams(("arbitrary",)),
        name="out_proj",
    )(x, ya, yb, w[:wa_n], w[wa_n:], tab, gain.reshape(1, d), tab, tab)


def _hyperbola_pairs():
    return [(j1, j2) for j1 in range(PEER_TOPK) for j2 in range(PEER_TOPK) if (j1 + 1) * (j2 + 1) <= PEER_TOPK]


def _topk_rows(s, k, break_ties):
    n, t = s.shape
    rowi = _iota((n, t), 0).astype(F32)
    rank = jnp.full((n, t), float(2 * k), F32)
    cur = s
    vals = []
    for j in range(k):
        m = jnp.max(cur, axis=0, keepdims=True)
        sel = cur == m
        if break_ties:
            sel = rowi == jnp.min(jnp.where(sel, rowi, float(n)), axis=0, keepdims=True)
        rank = jnp.where(sel, float(j), rank)
        cur = jnp.where(sel, -jnp.inf, cur)
        vals.append(m)
    return vals, rank


def _route(s1, s2, break_ties):
    v1, rank1 = _topk_rows(s1, PEER_TOPK, break_ties)
    v2, rank2 = _topk_rows(s2, PEER_TOPK, break_ties)
    pairs = _hyperbola_pairs()
    cand = jnp.concatenate([v1[j1] + v2[j2] for j1, j2 in pairs], axis=0)
    _, crank = _topk_rows(cand, PEER_TOPK, break_ties)
    chosen = crank < PEER_TOPK
    cmax = v1[0] + v2[0]
    zsum = jnp.sum(jnp.where(chosen, jnp.exp(cand - cmax), 0.0), axis=0, keepdims=True)
    counts = jnp.where(chosen, 1.0, 0.0)
    n1 = jnp.zeros_like(rank1)
    row = 0
    for j1 in range(PEER_TOPK):
        width = PEER_TOPK // (j1 + 1)
        nj = jnp.sum(counts[row:row + width], axis=0, keepdims=True)
        n1 = jnp.where(rank1 == float(j1), nj, n1)
        row += width
    e2 = jnp.where(rank2 < PEER_TOPK, jnp.exp(s2 - v2[0]), 0.0)
    e1 = jnp.where(rank1 < PEER_TOPK, jnp.exp(s1 - v1[0]), 0.0) / zsum
    n_chosen = [jnp.sum(jnp.where(r < PEER_TOPK, 1.0, 0.0), axis=0, keepdims=True) for r in (rank1, rank2, crank)]
    return rank2, e2, n1, e1, n_chosen


def _peer_select_kernel(h_ref, wq_ref, keys_ref, r2_ref, e2_ref, n1_ref, e1_ref):
    q = _mm(h_ref[...], wq_ref[...])
    s1 = _nt(keys_ref[0, 0], q[:, :N_KEYS], HI)
    s2 = _nt(keys_ref[0, 1], q[:, N_KEYS:], HI)

    def emit(rank2, e2, n1, e1):
        r2_ref[0] = rank2.astype(r2_ref.dtype)
        e2_ref[0] = e2.astype(e2_ref.dtype)
        n1_ref[0] = n1
        e1_ref[0] = e1

    rank2, e2, n1, e1, n_chosen = _route(s1, s2, False)
    emit(rank2, e2, n1, e1)
    excess = sum(jnp.max(jnp.abs(nc - float(PEER_TOPK))) for nc in n_chosen)

    @pl.when(excess > 0.0)
    def _():
        emit(*_route(s1, s2, True)[:4])


def _peer_select(h, wq, keys):
    t, d = h.shape
    nh = keys.shape[0]
    qd = wq.shape[1] // nh
    tm = TOK_BLK
    ospec = pl.BlockSpec((1, N_KEYS, tm), lambda i, hh: (hh, 0, i))
    return pl.pallas_call(
        _peer_select_kernel,
        grid=(t // tm, nh),
        in_specs=[pl.BlockSpec((tm, d), lambda i, hh: (i, 0)),
                  pl.BlockSpec((d, qd), lambda i, hh: (0, hh)),
                  pl.BlockSpec((1, 2, N_KEYS, qd // 2), lambda i, hh: (hh, 0, 0, 0))],
        out_specs=[ospec, ospec, ospec, ospec],
        out_shape=[jax.ShapeDtypeStruct((nh, N_KEYS, t), dt) for dt in (BF16, BF16, F32, F32)],
        compiler_params=_cparams(("arbitrary", "arbitrary")),
        name="peer_select",
    )(h, wq, keys)


def _peer_dense_kernel(h_ref, u_ref, vt_ref, r2_ref, e2_ref, n1_ref, e1_ref, x_ref, g2a_ref, g2b_ref, fn_ref, o_ref,
                       acc_ref, act_ref, wact_ref, *, n_heads, final):
    eb = pl.program_id(1)

    @pl.when(eb == 0)
    def _():
        acc_ref[...] = jnp.zeros_like(acc_ref)

    act_ref[...] = _nt(u_ref[...], h_ref[...])
    rows_per = EXP_BLK // N_KEYS
    i1_base = pl.multiple_of(eb * rows_per, rows_per)
    n1_tiles = [n1_ref[hh, pl.ds(i1_base, rows_per), :] for hh in range(n_heads)]
    e1_tiles = [e1_ref[hh, pl.ds(i1_base, rows_per), :] for hh in range(n_heads)]
    zero = jnp.zeros((), BF16)
    for r in range(rows_per):
        rows = slice(r * N_KEYS, (r + 1) * N_KEYS)
        wgt = None
        for hh in range(n_heads):
            n1row = n1_tiles[hh][r:r + 1, :].astype(BF16)
            e1row = e1_tiles[hh][r:r + 1, :].astype(BF16)
            term = jnp.where(r2_ref[hh] < n1row, e2_ref[hh], zero) * e1row
            wgt = term if wgt is None else wgt + term
        wact_ref[rows, :] = _gelu_tanh(act_ref[rows, :].astype(BF16)) * wgt
    acc_ref[...] += _mm(vt_ref[...], wact_ref[...])

    @pl.when(eb == pl.num_programs(1) - 1)
    def _():
        for part, g2_ref in enumerate((g2a_ref, g2b_ref)):
            rows = slice(part * TOK_BLK, (part + 1) * TOK_BLK)
            y = x_ref[rows, :] + g2_ref[0] * acc_ref[:, rows].T
            if final:
                y = y * lax.rsqrt(jnp.mean(y * y, axis=-1, keepdims=True) + EPS) * fn_ref[...]
            o_ref[rows, :] = y


def _peer_dense(h, u, vt, r2, e2, n1, e1, x, tab, rowfn, final_gain, final):
    t, d = h.shape
    ne = u.shape[0]
    nh = r2.shape[0]
    tm = PEER_BLK
    sel = pl.BlockSpec((nh, N_KEYS, tm), lambda i, e: (0, 0, i))
    g2 = lambda part: pl.BlockSpec((1, 1, d), lambda i, e: (rowfn(2 * i + part) + 5, 0, 0))
    return pl.pallas_call(
        functools.partial(_peer_dense_kernel, n_heads=nh, final=final),
        grid=(t // tm, ne // EXP_BLK),
        in_specs=[pl.BlockSpec((tm, d), lambda i, e: (i, 0)),
                  pl.BlockSpec((EXP_BLK, d), lambda i, e: (e, 0)),
                  pl.BlockSpec((d, EXP_BLK), lambda i, e: (0, e)),
                  sel, sel, sel, sel,
                  pl.BlockSpec((tm, d), lambda i, e: (i, 0)),
                  g2(0), g2(1),
                  pl.BlockSpec((1, d), lambda i, e: (0, 0))],
        out_specs=pl.BlockSpec((tm, d), lambda i, e: (i, 0)),
        out_shape=jax.ShapeDtypeStruct((t, d), F32),
        scratch_shapes=[pltpu.VMEM((d, tm), F32), pltpu.VMEM((EXP_BLK, tm), F32), pltpu.VMEM((EXP_BLK, tm), BF16)],
        compiler_params=_cparams(("arbitrary", "arbitrary")),
        name="peer_dense",
    )(h, u, vt, r2, e2, n1, e1, x, tab, tab, final_gain.reshape(1, d))


def _peer(h, x, wq, keys, u_tab, v_tab, tab, rowfn, final_gain, final):
    r2, e2, n1, e1 = _peer_select(h, wq.astype(BF16), keys)
    return _peer_dense(h, u_tab.astype(BF16), v_tab.T.astype(BF16), r2, e2, n1, e1, x, tab, rowfn, final_gain, final)


def _lane_vec(vals, rows=SUBLANES):
    out = jnp.zeros((rows, LANES), F32)
    for r, v in enumerate(vals):
        out = out.at[r, :v.shape[0]].set(v.astype(F32))
    return out


def _block_diag_groups(w):
    two, nb, bd, _ = w.shape
    per = LANES // bd
    wg = w.reshape(two, nb // per, per, bd, bd)
    eye = jnp.eye(per, dtype=w.dtype)
    return jnp.einsum("dgpij,pq->dgpiqj", wg, eye).reshape(two, nb // per, LANES, LANES)


def kernel(x, c, ctx, c_ctx, ada_w, ada_b, norm_mix, norm_ffn, final_norm, peer_wq, peer_keys, peer_u, peer_v,
           ev_w_in, ev_w_out, a_conv, a_alog, a_dtb, a_norm, b_conv_w, b_conv_b, b_wa, b_ba, b_wx, b_bx, b_lam,
           od_w_in, od_w_out, c_ibias, c_fbias, c_norm, d_w, d_scale):
    bsz, seq, dm = x.shape
    n_ctx = ctx.shape[1]
    L = n_ctx + seq
    rows = seq // GRID_W
    blk_per = L // TOK_BLK

    tab0 = _mod_table(c, c_ctx, ada_w[0], ada_b[0])
    row0 = lambda i: ((i // blk_per) * 2 + jnp.minimum(i % blk_per, 1)) * 6
    xcat = jnp.concatenate([ctx, x], axis=1).reshape(bsz * L, dm)

    a_heads = a_alog.shape[-1]
    a_width = a_heads * LANES
    a_qkv = 3 * a_width
    b_width = b_lam.shape[-1] * b_lam.shape[-2]
    w_in = ev_w_in[0]
    a_cols = a_qkv + a_width + 4 * a_heads
    w_ab = jnp.pad(w_in[:, a_qkv + a_width:a_cols], ((0, 0), (0, LANES - 4 * a_heads)))
    w0 = jnp.concatenate([w_in[:, :a_qkv], w_in[:, a_qkv:a_qkv + a_width], w_in[:, a_cols:a_cols + b_width],
                          w_in[:, a_cols + b_width:], w_ab], axis=1).astype(BF16)
    qkv, z, xb, gate, ab = _norm_mod_proj(xcat, norm_mix[0], tab0, row0, w0, (a_qkv, a_width, b_width, b_width, LANES))

    gparams = _lane_vec([a_alog[0].reshape(-1), a_dtb[0].reshape(-1)])
    ya = _gdn(qkv.reshape(bsz, L, a_qkv), z.reshape(bsz, L, a_width), ab.reshape(bsz, L, LANES),
              a_conv[0], gparams, a_norm[0], n_ctx, seq, a_heads)

    pvec = jnp.zeros((2, SUBLANES, b_width), F32)
    pvec = pvec.at[:, 0].set(b_ba[0].reshape(2, b_width)).at[:, 1].set(b_bx[0].reshape(2, b_width))
    pvec = pvec.at[:, 2].set(b_lam[0].reshape(2, b_width))
    yb = _lru(xb.reshape(bsz, L, b_width), gate.reshape(bsz, L, b_width), b_conv_w[0], b_conv_b[0],
              _block_diag_groups(b_wa[0]).astype(BF16), _block_diag_groups(b_wx[0]).astype(BF16), pvec, n_ctx, seq)

    x1, h1 = _outproj(xcat, ya.reshape(bsz * L, a_width), yb.reshape(bsz * L, b_width), ev_w_out[0], norm_ffn[0],
                      tab0, row0)
    x2 = _peer(h1, x1, peer_wq[0], peer_keys[0], peer_u[0], peer_v[0], tab0, row0, final_norm, False)
    x2 = x2.reshape(bsz, L, dm)

    tab1 = _mod_table(c, c_ctx, ada_w[1], ada_b[1])
    xc = x2[:, :n_ctx].reshape(bsz * n_ctx, dm)
    xl = x2[:, n_ctx:].reshape(bsz, rows, GRID_W, dm).transpose(0, 2, 1, 3).reshape(bsz * seq, dm)
    lat_per = seq // TOK_BLK
    ctx_per = n_ctx // TOK_BLK
    row_lat = lambda i: ((i // lat_per) * 2 + 1) * 6
    row_ctx = lambda i: ((i // ctx_per) * 2) * 6

    c_heads = c_ibias.shape[-1]
    c_width = c_norm.shape[-1]
    d_width = d_scale.shape[-1]
    w_in1 = od_w_in[0]
    n_state = w_in1.shape[1] - c_width - d_width
    c_qk = (n_state - c_width - 4 * c_heads) // 2
    dqk = c_qk // c_heads

    def pad_heads(wcols):
        return jnp.pad(wcols.reshape(dm, c_heads, dqk), ((0, 0), (0, 0), (0, LANES - dqk))).reshape(dm, c_heads * LANES)

    w_gate = jnp.pad(w_in1[:, 2 * c_qk + c_width:n_state], ((0, 0), (0, LANES - 4 * c_heads)))
    w1 = jnp.concatenate([pad_heads(w_in1[:, :c_qk]), pad_heads(w_in1[:, c_qk:2 * c_qk]),
                          w_in1[:, 2 * c_qk:2 * c_qk + c_width], w_in1[:, n_state:n_state + c_width],
                          w_in1[:, n_state + c_width:], w_gate], axis=1).astype(BF16)
    hw = c_heads * LANES
    widths1 = (hw, hw, c_width, c_width, d_width, LANES)
    ql, kl, vl, og, dl, gl = _norm_mod_proj(xl, norm_mix[1], tab1, row_lat, w1, widths1)
    qc, kc, vc, _, _, gc = _norm_mod_proj(xc, norm_mix[1], tab1, row_ctx, w1, widths1)

    gparams1 = _lane_vec([jnp.concatenate([c_ibias[0].reshape(-1), jnp.zeros((2 * c_heads,), F32)]),
                          jnp.concatenate([jnp.zeros((2 * c_heads,), F32), c_fbias[0].reshape(-1)])])
    r3 = lambda a, n: a.reshape(bsz, n, a.shape[-1])
    yc1 = _mlstm(r3(qc, n_ctx), r3(kc, n_ctx), r3(vc, n_ctx), r3(gc, n_ctx),
                 r3(ql, seq), r3(kl, seq), r3(vl, seq), r3(gl, seq), r3(og, seq), gparams1, c_norm[0], c_heads, dqk)
    yd1 = _pool(dl, d_w[0], d_scale[0], rows)

    x3, h3 = _outproj(xl, yc1.reshape(bsz * seq, c_width), yd1, od_w_out[0], norm_ffn[1], tab1, row_lat)
    out = _peer(h3, x3, peer_wq[1], peer_keys[1], peer_u[1], peer_v[1], tab1, row_lat, final_norm, True)
    return out.reshape(bsz, GRID_W, rows, dm).transpose(0, 2, 1, 3).reshape(bsz, seq, dm)
```

```python
import functools
import math

import jax
import jax.numpy as jnp
import numpy as np
from jax import lax
from jax.experimental import pallas as pl
from jax.experimental.pallas import tpu as pltpu

F32 = jnp.float32
BF16 = jnp.bfloat16
HI = lax.Precision.HIGHEST

EPS = 1e-6
GRID_W = 64
CHUNK = 64
LANES = 128
SUBLANES = 8
HALO = 8
CONV_W = 4
LRU_C = 8.0
PEER_TOPK = 16
N_KEYS = 128
POOL_SIZES = (2, 4, 8, 16)
TOK_BLK = 256
PEER_BLK = 2 * TOK_BLK
EXP_BLK = 1024
NEG_BIG = -1e30
VMEM_LIMIT = 48 * 1024 * 1024


def _cparams(sem):
    return pltpu.CompilerParams(dimension_semantics=sem, vmem_limit_bytes=VMEM_LIMIT)


def _nt(a, b, precision=None):
    return lax.dot_general(a, b, (((1,), (1,)), ((), ())), precision=precision, preferred_element_type=F32)


def _tn(a, b, precision=None):
    return lax.dot_general(a, b, (((0,), (0,)), ((), ())), precision=precision, preferred_element_type=F32)


def _mm(a, b, precision=None):
    return jnp.dot(a, b, precision=precision, preferred_element_type=F32)


def _silu(x):
    return x * jax.nn.sigmoid(x)


def _softplus(x):
    return jnp.maximum(x, 0.0) + jnp.log1p(jnp.exp(-jnp.abs(x)))


def _gelu_tanh(x):
    return 0.5 * x * (1.0 + jnp.tanh(math.sqrt(2.0 / math.pi) * (x + 0.044715 * (x * x * x))))


def _iota(shape, dim):
    return lax.broadcasted_iota(jnp.int32, shape, dim)


def _ada_kernel(c_ref, w_ref, b_ref, o_ref):
    o_ref[...] = _mm(_silu(c_ref[...]), w_ref[...], HI) + b_ref[...]


def _ada(cc, w, b):
    r, d = cc.shape
    n = w.shape[1]
    tn = 1536
    return pl.pallas_call(
        _ada_kernel,
        grid=(n // tn,),
        in_specs=[pl.BlockSpec((r, d), lambda j: (0, 0)),
                  pl.BlockSpec((d, tn), lambda j: (0, j)),
                  pl.BlockSpec((1, tn), lambda j: (0, j))],
        out_specs=pl.BlockSpec((r, tn), lambda j: (0, j)),
        out_shape=jax.ShapeDtypeStruct((r, n), F32),
        compiler_params=_cparams(("arbitrary",)),
        name="ada_mod",
    )(cc, w, b.reshape(1, n))


def _mod_table(c, c_ctx, w, b):
    bsz, d = c.shape
    rows = ((bsz + 1 + SUBLANES - 1) // SUBLANES) * SUBLANES
    cc = jnp.zeros((rows, d), F32).at[:bsz].set(c).at[bsz].set(c_ctx)
    m = _ada(cc, w, b).reshape(rows, 6, d)
    tab = jnp.stack([jnp.broadcast_to(m[bsz], (bsz, 6, d)), m[:bsz]], axis=1)
    return tab.reshape(bsz * 2 * 6, 1, d)


def _nmm_kernel(x_ref, g_ref, sh_ref, sc_ref, w_ref, *o_refs, widths):
    x = x_ref[...]
    y = x * lax.rsqrt(jnp.mean(x * x, axis=-1, keepdims=True) + EPS) * g_ref[...]
    h = (y * (1.0 + sc_ref[0]) + sh_ref[0]).astype(BF16)
    off = 0
    for o_ref, n in zip(o_refs, widths):
        o_ref[...] = _mm(h, w_ref[:, off:off + n])
        off += n


def _norm_mod_proj(x, gain, tab, rowfn, w, widths):
    t, d = x.shape
    n = w.shape[1]
    tm = TOK_BLK
    return pl.pallas_call(
        functools.partial(_nmm_kernel, widths=widths),
        grid=(t // tm,),
        in_specs=[pl.BlockSpec((tm, d), lambda i: (i, 0)),
                  pl.BlockSpec((1, d), lambda i: (0, 0)),
                  pl.BlockSpec((1, 1, d), lambda i: (rowfn(i), 0, 0)),
                  pl.BlockSpec((1, 1, d), lambda i: (rowfn(i) + 1, 0, 0)),
                  pl.BlockSpec((d, n), lambda i: (0, 0))],
        out_specs=[pl.BlockSpec((tm, wd), lambda i: (i, 0)) for wd in widths],
        out_shape=[jax.ShapeDtypeStruct((t, wd), F32) for wd in widths],
        compiler_params=_cparams(("arbitrary",)),
        name="norm_mod_proj",
    )(x, gain.reshape(1, d), tab, tab, w)


def _conv_chunk(pad_ref, pbase, w):
    blk = pad_ref[pl.ds(pbase - HALO, CHUNK + 2 * HALO), :]
    n = CHUNK + 2 * HALO
    lo, hi = HALO, HALO + CHUNK
    xm1 = pltpu.roll(blk, 1, 0)[lo:hi]
    x0 = blk[lo:hi]
    xp1 = pltpu.roll(blk, n - 1, 0)[lo:hi]
    xp2 = pltpu.roll(blk, n - 2, 0)[lo:hi]
    return w[0:1] * xm1 + w[1:2] * x0 + w[2:3] * xp1 + w[3:4] * xp2


def _fill_padded(pad_ref, src_ref, n_ctx, n_lat):
    width = pad_ref.shape[1]
    z = jnp.zeros((HALO, width), F32)
    pad_ref[0:HALO, :] = z
    pad_ref[HALO + n_ctx:2 * HALO + n_ctx, :] = z
    pad_ref[2 * HALO + n_ctx + n_lat:3 * HALO + n_ctx + n_lat, :] = z
    pad_ref[HALO:HALO + n_ctx, :] = src_ref[0, 0:n_ctx, :]
    pad_ref[2 * HALO + n_ctx:2 * HALO + n_ctx + n_lat, :] = src_ref[0, n_ctx:n_ctx + n_lat, :]


def _padded_base(c, ncc):
    return pl.multiple_of(HALO + c * CHUNK + jnp.where(c >= ncc, HALO, 0), SUBLANES)


def _lane_col(g, lane_idx):
    lane = _iota(g.shape, 1)
    return jnp.sum(jnp.where(lane == lane_idx, g, 0.0), axis=1, keepdims=True)


def _tri(d):
    r = _iota((CHUNK, CHUNK), 0)
    c = _iota((CHUNK, CHUNK), 1)
    incl = (r >= c) if d == 0 else (r <= c)
    strict = (r > c) if d == 0 else (r < c)
    return incl, strict


def _pair_diff(a_rows, b_rows):
    lane = _iota((CHUNK, LANES), 1)
    a = jnp.where(lane == 0, a_rows, jnp.where(lane == 1, 1.0, 0.0))
    b = jnp.where(lane == 0, 1.0, jnp.where(lane == 1, b_rows, 0.0))
    return _nt(a, b, HI)


def _bwd_chunk(s, ncc, nct):
    return jnp.where(s < ncc, ncc - 1 - s, nct + ncc - 1 - s)


def _split3(x):
    x1 = x.astype(BF16)
    r1 = x - x1.astype(F32)
    x2 = r1.astype(BF16)
    x3 = (r1 - x2.astype(F32)).astype(BF16)
    return x1, x2, x3


def _mask_mm(mask, x):
    x1, x2, x3 = _split3(x)
    return _mm(mask, x1) + (_mm(mask, x2) + _mm(mask, x3))


def _mask_tn(x, mask):
    x1, x2, x3 = _split3(x)
    return _tn(x1, mask) + (_tn(x2, mask) + _tn(x3, mask))


def _gate_kernel(ab_ref, gp_ref, gc_ref, gr_ref, *, n_chunks, n_heads, kind):
    r = _iota((CHUNK, CHUNK), 0)
    c = _iota((CHUNK, CHUNK), 1)
    lower = (r >= c).astype(BF16)
    upper = (r <= c).astype(BF16)
    eye = (r == c).astype(BF16)
    lane = _iota((CHUNK, LANES), 1)
    rowg = _iota((LANES, CHUNK), 0)
    if kind == "gdn":
        fwd_lo, rev_lo, rev_hi = 0, n_heads, 2 * n_heads
    else:
        fwd_lo, rev_lo, rev_hi = 2 * n_heads, 3 * n_heads, 4 * n_heads
    p0 = gp_ref[0:1, :]
    p1 = gp_ref[1:2, :]

    def body(ci, carry):
        r0 = pl.multiple_of(ci * CHUNK, CHUNK)
        raw = ab_ref[0, pl.ds(r0, CHUNK), :]
        if kind == "gdn":
            g = jnp.where(lane < 2 * n_heads, -jnp.exp(p0) * _softplus(raw + p1), jax.nn.sigmoid(raw))
        else:
            g = jnp.where(lane < 2 * n_heads, raw + p0, -_softplus(-(raw + p1)))
        is_fwd = (lane >= fwd_lo) & (lane < rev_lo)
        is_rev = (lane >= rev_lo) & (lane < rev_hi)
        gc_ref[0, pl.ds(r0, CHUNK), :] = jnp.where(is_fwd, _mask_mm(lower, g), jnp.where(is_rev, _mask_mm(upper, g), g))
        row_fwd = (rowg >= fwd_lo) & (rowg < rev_lo)
        row_rev = (rowg >= rev_lo) & (rowg < rev_hi)
        rows = jnp.where(row_fwd, _mask_tn(g, upper), jnp.where(row_rev, _mask_tn(g, lower), _mask_tn(g, eye)))
        gr_ref[0, ci] = rows[0:2 * SUBLANES, :]
        return carry

    lax.fori_loop(0, n_chunks, body, 0)


def _gates(ab, gparams, n_heads, kind):
    bsz, L, _ = ab.shape
    nct = L // CHUNK
    assert 4 * n_heads <= 2 * SUBLANES
    return pl.pallas_call(
        functools.partial(_gate_kernel, n_chunks=nct, n_heads=n_heads, kind=kind),
        grid=(bsz,),
        in_specs=[pl.BlockSpec((1, L, LANES), lambda b: (b, 0, 0)),
                  pl.BlockSpec((SUBLANES, LANES), lambda b: (0, 0))],
        out_specs=[pl.BlockSpec((1, L, LANES), lambda b: (b, 0, 0)),
                   pl.BlockSpec((1, nct, 2 * SUBLANES, CHUNK), lambda b: (b, 0, 0, 0))],
        out_shape=[jax.ShapeDtypeStruct((bsz, L, LANES), F32),
                   jax.ShapeDtypeStruct((bsz, nct, 2 * SUBLANES, CHUNK), F32)],
        compiler_params=_cparams(("arbitrary",)),
        name=kind + "_gates",
    )(ab, gparams)


def _mm3(a, b):
    ah = a.astype(BF16)
    al = (a - ah.astype(F32)).astype(BF16)
    bh = b.astype(BF16)
    bl = (b - bh.astype(F32)).astype(BF16)
    return _mm(ah, bh) + (_mm(ah, bl) + _mm(al, bh))


def _gdn_kernel(q_ref, k_ref, v_ref, z_ref, gc_ref, gr_ref, cwq_ref, cwk_ref, cwv_ref, nw_ref, o_ref,
                pad_ref, qs_ref, ks_ref, vs_ref, acc_ref, u_ref, w_ref, qd_ref, kd_ref, qk_ref, gl_ref,
                *, n_ctx, n_lat, n_heads):
    h = pl.program_id(1)
    ncc = n_ctx // CHUNK
    nct = (n_ctx + n_lat) // CHUNK
    dk = qs_ref.shape[1]

    def prep(src_ref, cw_ref, dst_ref, mode):
        _fill_padded(pad_ref, src_ref, n_ctx, n_lat)
        w = cw_ref[...]

        def body(c, carry):
            y = _silu(_conv_chunk(pad_ref, _padded_base(c, ncc), w))
            if mode != "v":
                y = y * lax.rsqrt(jnp.sum(y * y, axis=-1, keepdims=True) + EPS)
            if mode == "q":
                y = y * (dk ** -0.5)
            dst_ref[pl.ds(pl.multiple_of(c * CHUNK, CHUNK), CHUNK), :] = y
            return carry

        lax.fori_loop(0, nct, body, 0)

    prep(q_ref, cwq_ref, qs_ref, "q")
    prep(k_ref, cwk_ref, ks_ref, "k")
    prep(v_ref, cwv_ref, vs_ref, "v")

    n_chain = 4
    big = n_chain * CHUNK
    rr = _iota((big, big), 0)
    cc = _iota((big, big), 1)
    log_chunk = CHUNK.bit_length() - 1
    same_blk = lax.shift_right_logical(rr, log_chunk) == lax.shift_right_logical(cc, log_chunk)
    bwd_blk = (lax.shift_right_logical(rr, log_chunk) & 1) == 1
    fwd_blk = jnp.logical_not(bwd_blk)
    strict_bd = same_blk & ((bwd_blk & (rr < cc)) | (fwd_blk & (rr > cc)))
    incl_bd = same_blk & ((bwd_blk & (rr <= cc)) | (fwd_blk & (rr >= cc)))

    def prepare_group(c0):
        cols, ks, rhs, gcols, bcols, grows = [], [], [], [], [], []
        chains = []
        for j in range(2):
            c = c0 + j
            r0 = pl.multiple_of(c * CHUNK, CHUNK)
            acc_ref[pl.ds(r0, CHUNK), :] = jnp.zeros((CHUNK, dk), F32)
            q = qs_ref[pl.ds(r0, CHUNK), :]
            k = ks_ref[pl.ds(r0, CHUNK), :]
            v = vs_ref[pl.ds(r0, CHUNK), :]
            gates = gc_ref[0, pl.ds(r0, CHUNK), :]
            grt = gr_ref[0, c]
            for d in range(2):
                gcol = _lane_col(gates, d * n_heads + h)
                bcol = _lane_col(gates, 2 * n_heads + d * n_heads + h)
                grow = jnp.sum(jnp.where(_iota(grt.shape, 0) == d * n_heads + h, grt, 0.0), axis=0, keepdims=True)
                gtot = gcol[CHUNK - 1:CHUNK, :] if d == 0 else gcol[0:1, :]
                eg = jnp.exp(gcol)
                ks.append(k)
                gcols.append(gcol)
                bcols.append(bcol)
                grows.append(grow)
                rhs.append(jnp.concatenate([bcol * v, (bcol * eg) * k], axis=1))
                chains.append((c, d, r0, q, k, gcol, grow, gtot, eg))
        kst = jnp.concatenate(ks, axis=0).astype(BF16)
        gam = jnp.where(incl_bd, jnp.exp(jnp.minimum(jnp.concatenate(gcols, axis=0) - jnp.concatenate(grows, axis=1),
                                                      0.0)), 0.0)
        m_bd = jnp.where(strict_bd, jnp.concatenate(bcols, axis=0) * _nt(kst, kst) * gam, 0.0)
        sol = jnp.concatenate(rhs, axis=0)
        sol = sol - _mm3(m_bd, sol)
        p = _mm3(m_bd, m_bd)
        sol = sol + _mm3(p, sol)
        for it in range(4):
            p = p.astype(BF16)
            p = _mm(p, p)
            sol = sol + _mm(p.astype(BF16), sol.astype(BF16))
        for b, (c, d, r0, q, k, gcol, grow, gtot, eg) in enumerate(chains):
            incl, _ = _tri(d)
            gamma = jnp.where(incl, jnp.exp(jnp.minimum(gcol - grow, 0.0)), 0.0)
            u_ref[d, pl.ds(r0, CHUNK), :] = sol[b * CHUNK:(b + 1) * CHUNK, :dk]
            w_ref[d, pl.ds(r0, CHUNK), :] = sol[b * CHUNK:(b + 1) * CHUNK, dk:].astype(BF16)
            qd_ref[d, pl.ds(r0, CHUNK), :] = (q * eg).astype(BF16)
            kd_ref[d, pl.ds(r0, CHUNK), :] = (k * jnp.exp(gtot - gcol)).astype(BF16)
            qk_ref[d, c] = (_nt(q.astype(BF16), k.astype(BF16)) * gamma).astype(BF16)
            gl_ref[d, c] = jnp.broadcast_to(jnp.exp(gtot), (SUBLANES, LANES))

    def prepare_body(c2, carry):
        prepare_group(2 * c2)
        return carry

    lax.fori_loop(0, nct // 2, prepare_body, 0)

    def advance(c, d, s_state):
        r0 = pl.multiple_of(c * CHUNK, CHUNK)
        sb = s_state.astype(BF16)
        v_new = u_ref[d, pl.ds(r0, CHUNK), :] - _mm(w_ref[d, pl.ds(r0, CHUNK), :], sb)
        vb = v_new.astype(BF16)
        acc_ref[pl.ds(r0, CHUNK), :] += _mm(qd_ref[d, pl.ds(r0, CHUNK), :], sb) + _mm(qk_ref[d, c], vb)
        return s_state * gl_ref[d, c][0:1, :] + _tn(kd_ref[d, pl.ds(r0, CHUNK), :], vb)

    def step(s, carry):
        s_f, s_b = carry
        return advance(s, 0, s_f), advance(_bwd_chunk(s, ncc, nct), 1, s_b)

    zero = jnp.zeros((dk, dk), F32)
    lax.fori_loop(0, nct, step, (zero, zero))

    nw = nw_ref[...]

    def out_body(c, carry):
        r0 = pl.multiple_of(c * CHUNK, CHUNK)
        o = acc_ref[pl.ds(r0, CHUNK), :]
        y = o * lax.rsqrt(jnp.mean(o * o, axis=-1, keepdims=True) + EPS) * nw
        o_ref[0, pl.ds(r0, CHUNK), :] = y * _silu(z_ref[0, pl.ds(r0, CHUNK), :])
        return carry

    lax.fori_loop(0, nct, out_body, 0)


def _gdn(qkv, z, ab, conv_w, gparams, norm_w, n_ctx, n_lat, n_heads):
    bsz, L, _ = qkv.shape
    dk = LANES
    seq = lambda off: pl.BlockSpec((1, L, dk), lambda b, h: (b, 0, off + h))
    cw = lambda off: pl.BlockSpec((CONV_W, dk), lambda b, h: (0, off + h))
    lp = L + 3 * HALO
    nct = L // CHUNK
    assert nct % 2 == 0
    gc, gr = _gates(ab, gparams, n_heads, "gdn")
    return pl.pallas_call(
        functools.partial(_gdn_kernel, n_ctx=n_ctx, n_lat=n_lat, n_heads=n_heads),
        grid=(bsz, n_heads),
        in_specs=[seq(0), seq(n_heads), seq(2 * n_heads),
                  pl.BlockSpec((1, L, dk), lambda b, h: (b, 0, h)),
                  pl.BlockSpec((1, L, LANES), lambda b, h: (b, 0, 0)),
                  pl.BlockSpec((1, nct, 2 * SUBLANES, CHUNK), lambda b, h: (b, 0, 0, 0)),
                  cw(0), cw(n_heads), cw(2 * n_heads),
                  pl.BlockSpec((1, dk), lambda b, h: (0, 0))],
        out_specs=pl.BlockSpec((1, L, dk), lambda b, h: (b, 0, h)),
        out_shape=jax.ShapeDtypeStruct((bsz, L, n_heads * dk), F32),
        scratch_shapes=[pltpu.VMEM((lp, dk), F32)] + [pltpu.VMEM((L, dk), F32) for _ in range(4)]
        + [pltpu.VMEM((2, L, dk), F32)] + [pltpu.VMEM((2, L, dk), BF16) for _ in range(3)]
        + [pltpu.VMEM((2, nct, CHUNK, CHUNK), BF16), pltpu.VMEM((2, nct, SUBLANES, LANES), F32)],
        compiler_params=_cparams(("arbitrary", "arbitrary")),
        name="gdn_mixer",
    )(qkv, qkv, qkv, z, gc, gr, conv_w, conv_w, conv_w, norm_w.reshape(1, dk))


def _lru_kernel(x_ref, gt_ref, cw_ref, cb_ref, wa_ref, wx_ref, pv_ref, o_ref,
                pad_ref, xc_ref, a_ref, b_ref, acc_ref, *, n_ctx, n_lat):
    ncc = n_ctx // CHUNK
    nct = (n_ctx + n_lat) // CHUNK
    L = n_ctx + n_lat
    width = xc_ref.shape[1]
    _fill_padded(pad_ref, x_ref, n_ctx, n_lat)
    w = cw_ref[...]
    cb = cb_ref[...]

    def conv_body(c, carry):
        r0 = pl.multiple_of(c * CHUNK, CHUNK)
        xc_ref[pl.ds(r0, CHUNK), :] = _conv_chunk(pad_ref, _padded_base(c, ncc), w) + cb
        acc_ref[pl.ds(r0, CHUNK), :] = jnp.zeros((CHUNK, width), F32)
        return carry

    lax.fori_loop(0, nct, conv_body, 0)

    row = _iota((SUBLANES, width), 0)
    bias_a = [pv_ref[d, 0:1, :] for d in range(2)]
    bias_x = [pv_ref[d, 1:2, :] for d in range(2)]
    decay = [-LRU_C * _softplus(-pv_ref[d, 2:3, :]) for d in range(2)]

    def coef_body(c, carry):
        r0 = pl.multiple_of(c * CHUNK, CHUNK)
        x = xc_ref[pl.ds(r0, CHUNK), :]
        xb = x.astype(BF16)
        for d in range(2):
            r = jax.nn.sigmoid(_mm(xb, wa_ref[d, 0]) + bias_a[d])
            i = jax.nn.sigmoid(_mm(xb, wx_ref[d, 0]) + bias_x[d])
            log_a = decay[d] * r
            a_ref[d, pl.ds(r0, CHUNK), :] = jnp.exp(log_a)
            th = jnp.tanh(log_a)
            b_ref[d, pl.ds(r0, CHUNK), :] = jnp.sqrt(-2.0 * th / (1.0 - th)) * (i * x)
        return carry

    lax.fori_loop(0, nct, coef_body, 0)

    nt_ctx = n_ctx // SUBLANES
    nt_all = L // SUBLANES

    def scan_tile(t, d, hc):
        r0 = pl.multiple_of(t * SUBLANES, SUBLANES)
        a = a_ref[d, pl.ds(r0, SUBLANES), :]
        b = b_ref[d, pl.ds(r0, SUBLANES), :]
        for sh in (1, 2, 4):
            if d == 0:
                a_s = pltpu.roll(a, sh, 0)
                b_s = pltpu.roll(b, sh, 0)
                m = row >= sh
            else:
                a_s = pltpu.roll(a, SUBLANES - sh, 0)
                b_s = pltpu.roll(b, SUBLANES - sh, 0)
                m = row < SUBLANES - sh
            b = jnp.where(m, a * b_s + b, b)
            a = jnp.where(m, a * a_s, a)
        hcur = b + a * hc
        acc_ref[pl.ds(r0, SUBLANES), :] += hcur
        last = hcur[SUBLANES - 1:SUBLANES, :] if d == 0 else hcur[0:1, :]
        return jnp.broadcast_to(last, (SUBLANES, width))

    def scan_body(s, carry):
        hf, hb = carry
        tb = jnp.where(s < nt_ctx, nt_ctx - 1 - s, nt_all + nt_ctx - 1 - s)
        return scan_tile(s, 0, hf), scan_tile(tb, 1, hb)

    zero = jnp.zeros((SUBLANES, width), F32)
    lax.fori_loop(0, nt_all, scan_body, (zero, zero))


    def out_body(c, carry):
        r0 = pl.multiple_of(c * CHUNK, CHUNK)
        o_ref[0, pl.ds(r0, CHUNK), :] = acc_ref[pl.ds(r0, CHUNK), :] * _gelu_tanh(gt_ref[0, pl.ds(r0, CHUNK), :])
        return carry

    lax.fori_loop(0, nct, out_body, 0)


def _lru(xb, gate, conv_w, conv_b, wa, wx, pvec, n_ctx, n_lat):
    bsz, L, width = xb.shape
    ng = width // LANES
    lp = L + 3 * HALO
    seq = pl.BlockSpec((1, L, LANES), lambda b, j: (b, 0, j))
    return pl.pallas_call(
        functools.partial(_lru_kernel, n_ctx=n_ctx, n_lat=n_lat),
        grid=(bsz, ng),
        in_specs=[seq, seq,
                  pl.BlockSpec((CONV_W, LANES), lambda b, j: (0, j)),
                  pl.BlockSpec((1, LANES), lambda b, j: (0, j)),
                  pl.BlockSpec((2, 1, LANES, LANES), lambda b, j: (0, j, 0, 0)),
                  pl.BlockSpec((2, 1, LANES, LANES), lambda b, j: (0, j, 0, 0)),
                  pl.BlockSpec((2, SUBLANES, LANES), lambda b, j: (0, 0, j))],
        out_specs=seq,
        out_shape=jax.ShapeDtypeStruct((bsz, L, width), F32),
        scratch_shapes=[pltpu.VMEM((lp, LANES), F32), pltpu.VMEM((L, LANES), F32), pltpu.VMEM((2, L, LANES), F32),
                        pltpu.VMEM((2, L, LANES), F32), pltpu.VMEM((L, LANES), F32)],
        compiler_params=_cparams(("arbitrary", "arbitrary")),
        name="lru_mixer",
    )(xb, gate, conv_w, conv_b.reshape(1, width), wa, wx, pvec)


def _mlstm_kernel(qc_ref, kc_ref, vc_ref, gcc_ref, grc_ref, ql_ref, kl_ref, vl_ref, gcl_ref, grl_ref, og_ref, nw_ref,
                  o_ref, acc_ref, *, n_ctx, n_lat, n_heads, dqk):
    h = pl.program_id(1)
    ncc = n_ctx // CHUNK
    ncl = n_lat // CHUNK
    dv = LANES

    def zero_body(c, carry):
        acc_ref[pl.ds(pl.multiple_of(c * CHUNK, CHUNK), CHUNK), :] = jnp.zeros((CHUNK, dv), F32)
        return carry

    lax.fori_loop(0, ncl, zero_body, 0)

    def chunk(refs, c, d, state, with_out):
        q_ref, k_ref, v_ref, gcol_ref, grow_ref = refs
        cx, m_s = state
        r0 = pl.multiple_of(c * CHUNK, CHUNK)
        q = q_ref[0, pl.ds(r0, CHUNK), :].astype(BF16)
        k = k_ref[0, pl.ds(r0, CHUNK), :] * (dqk ** -0.5)
        v = v_ref[0, pl.ds(r0, CHUNK), :]
        gates = gcol_ref[0, pl.ds(r0, CHUNK), :]
        li = _lane_col(gates, d * n_heads + h)
        bcum = _lane_col(gates, 2 * n_heads + d * n_heads + h)
        btot = bcum[CHUNK - 1:CHUNK, :] if d == 0 else bcum[0:1, :]
        w_end = btot - bcum + li
        m_chunk = jnp.max(w_end, axis=0, keepdims=True)
        e_end = jnp.exp(w_end - m_chunk)
        lane = _iota((CHUNK, LANES), 1)
        v_ext = jnp.concatenate([v, jnp.where(lane == 0, 1.0, 0.0)], axis=1).astype(BF16)
        c_chunk = _tn((k * e_end).astype(BF16), v_ext)
        m_new = jnp.maximum(btot + m_s, m_chunk)
        cx_new = jnp.exp(btot + m_s - m_new) * cx + jnp.exp(m_chunk - m_new) * c_chunk
        if with_out:
            grt = grow_ref[0, c]
            rowi = _iota(grt.shape, 0)
            li_row = jnp.sum(jnp.where(rowi == d * n_heads + h, grt, 0.0), axis=0, keepdims=True)
            b_row = jnp.sum(jnp.where(rowi == 2 * n_heads + d * n_heads + h, grt, 0.0), axis=0, keepdims=True)
            incl, _ = _tri(d)
            log_d = jnp.where(incl, bcum + (li_row - b_row), NEG_BIG)
            m_intra = jnp.max(log_d, axis=1, keepdims=True)
            s_qk = (_nt(q, k.astype(BF16)) * jnp.exp(log_d - m_intra)).astype(BF16)
            m_inter = bcum + m_s
            m_tot = jnp.maximum(m_inter, m_intra)
            w_inter = jnp.exp(m_inter - m_tot)
            w_intra = jnp.exp(m_intra - m_tot)
            p1 = _mm(q, cx.astype(BF16))
            p2 = _mm(s_qk, v_ext)
            num = w_inter * p1[:, :dv] + w_intra * p2[:, :dv]
            den = (w_inter * p1[:, dv:] + w_intra * p2[:, dv:])[:, 0:1]
            acc_ref[pl.ds(r0, CHUNK), :] += num / jnp.maximum(jnp.abs(den), jnp.exp(-m_tot))
        return cx_new, m_new

    zero = (jnp.zeros((LANES, 2 * dv), F32), jnp.zeros((1, 1), F32))
    ctx_refs = (qc_ref, kc_ref, vc_ref, gcc_ref, grc_ref)
    lat_refs = (ql_ref, kl_ref, vl_ref, gcl_ref, grl_ref)

    def ctx_step(s, carry):
        st_f, st_b = carry
        st_f = chunk(ctx_refs, s, 0, st_f, False)
        st_b = chunk(ctx_refs, ncc - 1 - s, 1, st_b, False)
        return st_f, st_b

    carry = lax.fori_loop(0, ncc, ctx_step, (zero, zero))

    def lat_step(s, carry):
        st_f, st_b = carry
        st_f = chunk(lat_refs, s, 0, st_f, True)
        st_b = chunk(lat_refs, ncl - 1 - s, 1, st_b, True)
        return st_f, st_b

    lax.fori_loop(0, ncl, lat_step, carry)

    nw = nw_ref[...]

    def out_body(c, carry):
        r0 = pl.multiple_of(c * CHUNK, CHUNK)
        hs = acc_ref[pl.ds(r0, CHUNK), :]
        y = hs * lax.rsqrt(jnp.mean(hs * hs, axis=-1, keepdims=True) + EPS) * nw
        o_ref[0, pl.ds(r0, CHUNK), :] = y * jax.nn.sigmoid(og_ref[0, pl.ds(r0, CHUNK), :])
        return carry

    lax.fori_loop(0, ncl, out_body, 0)


def _mlstm(qc, kc, vc, gc, ql, kl, vl, gl, og, gparams, norm_w, n_heads, dqk):
    bsz, n_ctx, _ = qc.shape
    n_lat = ql.shape[1]
    cs = pl.BlockSpec((1, n_ctx, LANES), lambda b, h: (b, 0, h))
    ls = pl.BlockSpec((1, n_lat, LANES), lambda b, h: (b, 0, h))
    gcc, grc = _gates(gc, gparams, n_heads, "mlstm")
    gcl, grl = _gates(gl, gparams, n_heads, "mlstm")
    row_spec = lambda n: pl.BlockSpec((1, n // CHUNK, 2 * SUBLANES, CHUNK), lambda b, h: (b, 0, 0, 0))
    return pl.pallas_call(
        functools.partial(_mlstm_kernel, n_ctx=n_ctx, n_lat=n_lat, n_heads=n_heads, dqk=dqk),
        grid=(bsz, n_heads),
        in_specs=[cs, cs, cs, pl.BlockSpec((1, n_ctx, LANES), lambda b, h: (b, 0, 0)), row_spec(n_ctx),
                  ls, ls, ls, pl.BlockSpec((1, n_lat, LANES), lambda b, h: (b, 0, 0)), row_spec(n_lat),
                  ls,
                  pl.BlockSpec((1, LANES), lambda b, h: (0, h))],
        out_specs=ls,
        out_shape=jax.ShapeDtypeStruct((bsz, n_lat, n_heads * LANES), F32),
        scratch_shapes=[pltpu.VMEM((n_lat, LANES), F32)],
        compiler_params=_cparams(("arbitrary", "arbitrary")),
        name="mlstm_mixer",
    )(qc, kc, vc, gcc, grc, ql, kl, vl, gcl, grl, og, norm_w.reshape(1, -1))


def _pool_kernel(x_ref, w_ref, sc_ref, o_ref, *, seg):
    tm = x_ref.shape[0]
    r = _iota((tm, tm), 0)
    c = _iota((tm, tm), 1)
    sh = seg.bit_length() - 1
    same = lax.shift_right_logical(r, sh) == lax.shift_right_logical(c, sh)
    t = r & (seg - 1)
    s = c & (seg - 1)
    for gi, wsz in enumerate(POOL_SIZES):
        lo = jnp.maximum(t - wsz // 2, 0)
        hi = jnp.minimum(t - wsz // 2 + wsz, seg)
        inwin = same & (s >= lo) & (s < hi)
        pmat = jnp.where(inwin, 1.0 / (hi - lo).astype(F32), 0.0) - jnp.where(r == c, 1.0, 0.0)
        x = x_ref[:, gi * LANES:(gi + 1) * LANES]
        pooled = _mm3(pmat, x)
        y = _mm(pooled.astype(BF16), w_ref[gi].astype(BF16))
        o_ref[:, gi * LANES:(gi + 1) * LANES] = y * sc_ref[:, gi * LANES:(gi + 1) * LANES]


def _pool(x, w_grp, scale, seg):
    t, width = x.shape
    tm = TOK_BLK
    assert seg & (seg - 1) == 0 and tm % seg == 0
    return pl.pallas_call(
        functools.partial(_pool_kernel, seg=seg),
        grid=(t // tm,),
        in_specs=[pl.BlockSpec((tm, width), lambda i: (i, 0)),
                  pl.BlockSpec(w_grp.shape, lambda i: (0, 0, 0)),
                  pl.BlockSpec((1, width), lambda i: (0, 0))],
        out_specs=pl.BlockSpec((tm, width), lambda i: (i, 0)),
        out_shape=jax.ShapeDtypeStruct((t, width), F32),
        compiler_params=_cparams(("arbitrary",)),
        name="pool_mixer",
    )(x, w_grp, scale.reshape(1, width))


def _outproj_kernel(x_ref, ya_ref, yb_ref, wa_ref, wb_ref, g1_ref, gn_ref, sh_ref, sc_ref, xo_ref, h_ref):
    y = _mm(ya_ref[...].astype(BF16), wa_ref[...]) + _mm(yb_ref[...].astype(BF16), wb_ref[...])
    xn = x_ref[...] + g1_ref[0] * y
    xo_ref[...] = xn
    hn = xn * lax.rsqrt(jnp.mean(xn * xn, axis=-1, keepdims=True) + EPS) * gn_ref[...]
    h_ref[...] = (hn * (1.0 + sc_ref[0]) + sh_ref[0]).astype(BF16)


def _outproj(x, ya, yb, w_out, gain, tab, rowfn):
    t, d = x.shape
    wa_n = ya.shape[1]
    tm = TOK_BLK
    w = w_out.astype(BF16)
    mod = lambda k: pl.BlockSpec((1, 1, d), lambda i: (rowfn(i) + k, 0, 0))
    return pl.pallas_call(
        _outproj_kernel,
        grid=(t // tm,),
        in_specs=[pl.BlockSpec((tm, d), lambda i: (i, 0)),
                  pl.BlockSpec((tm, wa_n), lambda i: (i, 0)),
                  pl.BlockSpec((tm, yb.shape[1]), lambda i: (i, 0)),
                  pl.BlockSpec((wa_n, d), lambda i: (0, 0)),
                  pl.BlockSpec((yb.shape[1], d), lambda i: (0, 0)),
                  mod(2),
                  pl.BlockSpec((1, d), lambda i: (0, 0)),
                  mod(3), mod(4)],
        out_specs=[pl.BlockSpec((tm, d), lambda i: (i, 0)), pl.BlockSpec((tm, d), lambda i: (i, 0))],
        out_shape=[jax.ShapeDtypeStruct((t, d), F32), jax.ShapeDtypeStruct((t, d), BF16)],
        compiler_params=_cparams(("arbitrary",)),
        name="out_proj",
    )(x, ya, yb, w[:wa_n], w[wa_n:], tab, gain.reshape(1, d), tab, tab)


def _hyperbola_pairs():
    return [(j1, j2) for j1 in range(PEER_TOPK) for j2 in range(PEER_TOPK) if (j1 + 1) * (j2 + 1) <= PEER_TOPK]


def _topk_rows(s, k, break_ties):
    n, t = s.shape
    rowi = _iota((n, t), 0).astype(F32)
    rank = jnp.full((n, t), float(2 * k), F32)
    cur = s
    vals = []
    for j in range(k):
        m = jnp.max(cur, axis=0, keepdims=True)
        sel = cur == m
        if break_ties:
            sel = rowi == jnp.min(jnp.where(sel, rowi, float(n)), axis=0, keepdims=True)
        rank = jnp.where(sel, float(j), rank)
        cur = jnp.where(sel, -jnp.inf, cur)
        vals.append(m)
    return vals, rank


def _candidate_stage(v1, v2, break_ties):
    pairs = _hyperbola_pairs()
    cand = jnp.concatenate([v1[j1] + v2[j2] for j1, j2 in pairs], axis=0)
    _, crank = _topk_rows(cand, PEER_TOPK, break_ties)
    chosen = crank < PEER_TOPK
    zsum = jnp.sum(jnp.where(chosen, jnp.exp(cand - (v1[0] + v2[0])), 0.0), axis=0, keepdims=True)
    counts = jnp.where(chosen, 1.0, 0.0)
    n_by_rank = []
    row = 0
    for j1 in range(PEER_TOPK):
        width = PEER_TOPK // (j1 + 1)
        n_by_rank.append(jnp.sum(counts[row:row + width], axis=0, keepdims=True))
        row += width
    return n_by_rank, zsum, jnp.sum(counts, axis=0, keepdims=True)


def _route(s1, s2):
    v1, rank1 = _topk_rows(s1, PEER_TOPK, True)
    v2, rank2 = _topk_rows(s2, PEER_TOPK, True)
    n_by_rank, zsum, _ = _candidate_stage(v1, v2, True)
    n1 = jnp.zeros_like(rank1)
    for j1 in range(PEER_TOPK):
        n1 = jnp.where(rank1 == float(j1), n_by_rank[j1], n1)
    e2 = jnp.where(rank2 < PEER_TOPK, jnp.exp(s2 - v2[0]), 0.0)
    e1 = jnp.where(rank1 < PEER_TOPK, jnp.exp(s1 - v1[0]), 0.0) / zsum
    return rank2, e2, n1, e1


def _sorted_top(s):
    n = s.shape[0] // SUBLANES
    a = [s[SUBLANES * j:SUBLANES * (j + 1), :] for j in range(n)]

    def cex(i, l):
        a[i], a[l] = jnp.maximum(a[i], a[l]), jnp.minimum(a[i], a[l])

    k = 2
    while k <= n:
        j = k // 2
        while j >= 1:
            for i in range(n):
                l = i ^ j
                if l > i:
                    if (i & k) == 0:
                        cex(i, l)
                    else:
                        cex(l, i)
            j //= 2
        k *= 2
    for shift in (4, 2, 1):
        a = [jnp.maximum(a[j], pltpu.roll(a[n - 1 - j], shift, 0)) for j in range(n)]
        j = n // 2
        while j >= 1:
            for i in range(n):
                l = i ^ j
                if l > i:
                    cex(i, l)
            j //= 2
    return a


def _by_rank(bits, table):
    level = list(table)
    for g in reversed(bits):
        level = [jnp.where(g, level[2 * i + 1], level[2 * i]) for i in range(len(level) // 2)]
    return level[0]


def _rank_bits(x, v):
    g8 = v[7] > x
    g4 = jnp.where(g8, v[11], v[3]) > x
    g2 = _by_rank([g8, g4], [v[1], v[5], v[9], v[13]]) > x
    g1 = _by_rank([g8, g4, g2], [v[2 * i] for i in range(8)]) > x
    return [g8, g4, g2, g1]


def _route_untied(s1, s2):
    v1 = _sorted_top(s1)
    v2 = _sorted_top(s2)
    n_by_rank, zsum, n_cand = _candidate_stage([v[0:1, :] for v in v1], [v[0:1, :] for v in v2], False)
    shape = v1[0].shape
    n_tab = [jnp.broadcast_to(nj, shape) for nj in n_by_rank]
    inv_z = jnp.broadcast_to(1.0 / zsum, shape)
    rank2, e2, n1, e1 = [], [], [], []
    cnt1 = jnp.zeros(shape, F32)
    cnt2 = jnp.zeros(shape, F32)
    for r in range(s1.shape[0] // SUBLANES):
        x1 = s1[SUBLANES * r:SUBLANES * (r + 1), :]
        x2 = s2[SUBLANES * r:SUBLANES * (r + 1), :]
        in1 = x1 >= v1[PEER_TOPK - 1]
        in2 = x2 >= v2[PEER_TOPK - 1]
        n1.append(jnp.where(in1, _by_rank(_rank_bits(x1, v1), n_tab), 0.0))
        e1.append(jnp.where(in1, jnp.exp(x1 - v1[0]) * inv_z, 0.0))
        g8, g4, g2, g1 = _rank_bits(x2, v2)
        rk = (jnp.where(g8, 8.0, 0.0) + jnp.where(g4, 4.0, 0.0)) + (jnp.where(g2, 2.0, 0.0) + jnp.where(g1, 1.0, 0.0))
        rank2.append(jnp.where(in2, rk, float(2 * PEER_TOPK)))
        e2.append(jnp.where(in2, jnp.exp(x2 - v2[0]), 0.0))
        cnt1 = cnt1 + jnp.where(in1, 1.0, 0.0)
        cnt2 = cnt2 + jnp.where(in2, 1.0, 0.0)
    tied = jnp.abs(n_cand - float(PEER_TOPK))
    for cnt in (cnt1, cnt2):
        tied = tied + jnp.abs(jnp.sum(cnt, axis=0, keepdims=True) - float(PEER_TOPK))
    for v in (v1, v2):
        for j in range(PEER_TOPK - 1):
            tied = tied + jnp.where(v[j][0:1, :] == v[j + 1][0:1, :], 1.0, 0.0)
    cat = lambda parts: jnp.concatenate(parts, axis=0)
    return cat(rank2), cat(e2), cat(n1), cat(e1), tied


def _peer_select_kernel(h_ref, wq_ref, keys_ref, r2_ref, e2_ref, n1_ref, e1_ref):
    q = _mm(h_ref[...], wq_ref[...])
    s1 = _nt(keys_ref[0, 0], q[:, :N_KEYS], HI)
    s2 = _nt(keys_ref[0, 1], q[:, N_KEYS:], HI)

    def emit(rank2, e2, n1, e1):
        r2_ref[0] = rank2.astype(r2_ref.dtype)
        e2_ref[0] = e2.astype(e2_ref.dtype)
        n1_ref[0] = n1
        e1_ref[0] = e1

    rank2, e2, n1, e1, tied = _route_untied(s1, s2)
    emit(rank2, e2, n1, e1)

    @pl.when(jnp.max(tied) > 0.0)
    def _():
        emit(*_route(s1, s2))


def _peer_select(h, wq, keys):
    t, d = h.shape
    nh = keys.shape[0]
    qd = wq.shape[1] // nh
    tm = TOK_BLK
    ospec = pl.BlockSpec((1, N_KEYS, tm), lambda i, hh: (hh, 0, i))
    return pl.pallas_call(
        _peer_select_kernel,
        grid=(t // tm, nh),
        in_specs=[pl.BlockSpec((tm, d), lambda i, hh: (i, 0)),
                  pl.BlockSpec((d, qd), lambda i, hh: (0, hh)),
                  pl.BlockSpec((1, 2, N_KEYS, qd // 2), lambda i, hh: (hh, 0, 0, 0))],
        out_specs=[ospec, ospec, ospec, ospec],
        out_shape=[jax.ShapeDtypeStruct((nh, N_KEYS, t), dt) for dt in (BF16, BF16, F32, F32)],
        compiler_params=_cparams(("arbitrary", "arbitrary")),
        name="peer_select",
    )(h, wq, keys)


def _peer_dense_kernel(h_ref, u_ref, vt_ref, r2_ref, e2_ref, n1_ref, e1_ref, x_ref, g2a_ref, g2b_ref, fn_ref, o_ref,
                       acc_ref, wact_ref, *, n_heads, final):
    eb = pl.program_id(1)
    n_eb = pl.num_programs(1) - 1
    cur = eb & 1
    prev = 1 - cur

    @pl.when(eb == 0)
    def _():
        acc_ref[...] = jnp.zeros_like(acc_ref)
        wact_ref[1] = jnp.zeros(wact_ref.shape[1:], BF16)

    hb = h_ref[...]
    n_part = 4
    part_rows = EXP_BLK // n_part
    acts = [_nt(u_ref[p * part_rows:(p + 1) * part_rows, :], hb) for p in range(n_part)]
    upd = _mm(vt_ref[...], wact_ref[prev])
    rows_per = EXP_BLK // N_KEYS
    i1_base = pl.multiple_of(jnp.minimum(eb, n_eb - 1) * rows_per, rows_per)
    n1_tiles = [n1_ref[hh, pl.ds(i1_base, rows_per), :] for hh in range(n_heads)]
    e1_tiles = [e1_ref[hh, pl.ds(i1_base, rows_per), :] for hh in range(n_heads)]
    zero = jnp.zeros((), BF16)
    for r in range(rows_per):
        wgt = None
        for hh in range(n_heads):
            shape = r2_ref.shape[1:]
            n1row = jnp.broadcast_to(n1_tiles[hh][r:r + 1, :], shape).astype(BF16)
            e1row = jnp.broadcast_to(e1_tiles[hh][r:r + 1, :], shape).astype(BF16)
            term = jnp.where(r2_ref[hh] < n1row, e2_ref[hh], zero) * e1row
            wgt = term if wgt is None else wgt + term
        off = (r * N_KEYS) % part_rows
        a = acts[(r * N_KEYS) // part_rows][off:off + N_KEYS, :]
        wact_ref[cur, r * N_KEYS:(r + 1) * N_KEYS, :] = _gelu_tanh(a.astype(BF16)) * wgt
    acc_ref[...] += upd

    @pl.when(eb == n_eb)
    def _():
        for part, g2_ref in enumerate((g2a_ref, g2b_ref)):
            rows = slice(part * TOK_BLK, (part + 1) * TOK_BLK)
            y = x_ref[rows, :] + g2_ref[0] * acc_ref[:, rows].T
            if final:
                y = y * lax.rsqrt(jnp.mean(y * y, axis=-1, keepdims=True) + EPS) * fn_ref[...]
            o_ref[rows, :] = y


def _peer_dense(h, u, vt, r2, e2, n1, e1, x, tab, rowfn, final_gain, final):
    t, d = h.shape
    n_eb = u.shape[0] // EXP_BLK
    nh = r2.shape[0]
    tm = PEER_BLK
    sel = pl.BlockSpec((nh, N_KEYS, tm), lambda i, e: (0, 0, i))
    g2 = lambda part: pl.BlockSpec((1, 1, d), lambda i, e: (rowfn(2 * i + part) + 5, 0, 0))
    return pl.pallas_call(
        functools.partial(_peer_dense_kernel, n_heads=nh, final=final),
        grid=(t // tm, n_eb + 1),
        in_specs=[pl.BlockSpec((tm, d), lambda i, e: (i, 0)),
                  pl.BlockSpec((EXP_BLK, d), lambda i, e: (jnp.minimum(e, n_eb - 1), 0)),
                  pl.BlockSpec((d, EXP_BLK), lambda i, e: (0, jnp.maximum(e - 1, 0))),
                  sel, sel, sel, sel,
                  pl.BlockSpec((tm, d), lambda i, e: (i, 0)),
                  g2(0), g2(1),
                  pl.BlockSpec((1, d), lambda i, e: (0, 0))],
        out_specs=pl.BlockSpec((tm, d), lambda i, e: (i, 0)),
        out_shape=jax.ShapeDtypeStruct((t, d), F32),
        scratch_shapes=[pltpu.VMEM((d, tm), F32), pltpu.VMEM((2, EXP_BLK, tm), BF16)],
        compiler_params=_cparams(("arbitrary", "arbitrary")),
        name="peer_dense",
    )(h, u, vt, r2, e2, n1, e1, x, tab, tab, final_gain.reshape(1, d))


def _peer(h, x, wq, keys, u_tab, v_tab, tab, rowfn, final_gain, final):
    r2, e2, n1, e1 = _peer_select(h, wq.astype(BF16), keys)
    return _peer_dense(h, u_tab.astype(BF16), v_tab.T.astype(BF16), r2, e2, n1, e1, x, tab, rowfn, final_gain, final)


def _lane_vec(vals, rows=SUBLANES):
    out = jnp.zeros((rows, LANES), F32)
    for r, v in enumerate(vals):
        out = out.at[r, :v.shape[0]].set(v.astype(F32))
    return out


def _block_diag_groups(w):
    two, nb, bd, _ = w.shape
    per = LANES // bd
    wg = w.reshape(two, nb // per, per, bd, bd)
    eye = jnp.eye(per, dtype=w.dtype)
    return jnp.einsum("dgpij,pq->dgpiqj", wg, eye).reshape(two, nb // per, LANES, LANES)


def kernel(x, c, ctx, c_ctx, ada_w, ada_b, norm_mix, norm_ffn, final_norm, peer_wq, peer_keys, peer_u, peer_v,
           ev_w_in, ev_w_out, a_conv, a_alog, a_dtb, a_norm, b_conv_w, b_conv_b, b_wa, b_ba, b_wx, b_bx, b_lam,
           od_w_in, od_w_out, c_ibias, c_fbias, c_norm, d_w, d_scale):
    bsz, seq, dm = x.shape
    n_ctx = ctx.shape[1]
    L = n_ctx + seq
    rows = seq // GRID_W
    blk_per = L // TOK_BLK

    tab0 = _mod_table(c, c_ctx, ada_w[0], ada_b[0])
    row0 = lambda i: ((i // blk_per) * 2 + jnp.minimum(i % blk_per, 1)) * 6
    xcat = jnp.concatenate([ctx, x], axis=1).reshape(bsz * L, dm)

    a_heads = a_alog.shape[-1]
    a_width = a_heads * LANES
    a_qkv = 3 * a_width
    b_width = b_lam.shape[-1] * b_lam.shape[-2]
    w_in = ev_w_in[0]
    a_cols = a_qkv + a_width + 4 * a_heads
    w_ab = jnp.pad(w_in[:, a_qkv + a_width:a_cols], ((0, 0), (0, LANES - 4 * a_heads)))
    w0 = jnp.concatenate([w_in[:, :a_qkv], w_in[:, a_qkv:a_qkv + a_width], w_in[:, a_cols:a_cols + b_width],
                          w_in[:, a_cols + b_width:], w_ab], axis=1).astype(BF16)
    qkv, z, xb, gate, ab = _norm_mod_proj(xcat, norm_mix[0], tab0, row0, w0, (a_qkv, a_width, b_width, b_width, LANES))

    gparams = _lane_vec([a_alog[0].reshape(-1), a_dtb[0].reshape(-1)])
    ya = _gdn(qkv.reshape(bsz, L, a_qkv), z.reshape(bsz, L, a_width), ab.reshape(bsz, L, LANES),
              a_conv[0], gparams, a_norm[0], n_ctx, seq, a_heads)

    pvec = jnp.zeros((2, SUBLANES, b_width), F32)
    pvec = pvec.at[:, 0].set(b_ba[0].reshape(2, b_width)).at[:, 1].set(b_bx[0].reshape(2, b_width))
    pvec = pvec.at[:, 2].set(b_lam[0].reshape(2, b_width))
    yb = _lru(xb.reshape(bsz, L, b_width), gate.reshape(bsz, L, b_width), b_conv_w[0], b_conv_b[0],
              _block_diag_groups(b_wa[0]).astype(BF16), _block_diag_groups(b_wx[0]).astype(BF16), pvec, n_ctx, seq)

    x1, h1 = _outproj(xcat, ya.reshape(bsz * L, a_width), yb.reshape(bsz * L, b_width), ev_w_out[0], norm_ffn[0],
                      tab0, row0)
    x2 = _peer(h1, x1, peer_wq[0], peer_keys[0], peer_u[0], peer_v[0], tab0, row0, final_norm, False)
    x2 = x2.reshape(bsz, L, dm)

    tab1 = _mod_table(c, c_ctx, ada_w[1], ada_b[1])
    xc = x2[:, :n_ctx].reshape(bsz * n_ctx, dm)
    xl = x2[:, n_ctx:].reshape(bsz, rows, GRID_W, dm).transpose(0, 2, 1, 3).reshape(bsz * seq, dm)
    lat_per = seq // TOK_BLK
    ctx_per = n_ctx // TOK_BLK
    row_lat = lambda i: ((i // lat_per) * 2 + 1) * 6
    row_ctx = lambda i: ((i // ctx_per) * 2) * 6

    c_heads = c_ibias.shape[-1]
    c_width = c_norm.shape[-1]
    d_width = d_scale.shape[-1]
    w_in1 = od_w_in[0]
    n_state = w_in1.shape[1] - c_width - d_width
    c_qk = (n_state - c_width - 4 * c_heads) // 2
    dqk = c_qk // c_heads

    def pad_heads(wcols):
        return jnp.pad(wcols.reshape(dm, c_heads, dqk), ((0, 0), (0, 0), (0, LANES - dqk))).reshape(dm, c_heads * LANES)

    w_gate = jnp.pad(w_in1[:, 2 * c_qk + c_width:n_state], ((0, 0), (0, LANES - 4 * c_heads)))
    w1 = jnp.concatenate([pad_heads(w_in1[:, :c_qk]), pad_heads(w_in1[:, c_qk:2 * c_qk]),
                          w_in1[:, 2 * c_qk:2 * c_qk + c_width], w_in1[:, n_state:n_state + c_width],
                          w_in1[:, n_state + c_width:], w_gate], axis=1).astype(BF16)
    hw = c_heads * LANES
    widths1 = (hw, hw, c_width, c_width, d_width, LANES)
    ql, kl, vl, og, dl, gl = _norm_mod_proj(xl, norm_mix[1], tab1, row_lat, w1, widths1)
    qc, kc, vc, _, _, gc = _norm_mod_proj(xc, norm_mix[1], tab1, row_ctx, w1, widths1)

    gparams1 = _lane_vec([jnp.concatenate([c_ibias[0].reshape(-1), jnp.zeros((2 * c_heads,), F32)]),
                          jnp.concatenate([jnp.zeros((2 * c_heads,), F32), c_fbias[0].reshape(-1)])])
    r3 = lambda a, n: a.reshape(bsz, n, a.shape[-1])
    yc1 = _mlstm(r3(qc, n_ctx), r3(kc, n_ctx), r3(vc, n_ctx), r3(gc, n_ctx),
                 r3(ql, seq), r3(kl, seq), r3(vl, seq), r3(gl, seq), r3(og, seq), gparams1, c_norm[0], c_heads, dqk)
    yd1 = _pool(dl, d_w[0], d_scale[0], rows)

    x3, h3 = _outproj(xl, yc1.reshape(bsz * seq, c_width), yd1, od_w_out[0], norm_ffn[1], tab1, row_lat)
    out = _peer(h3, x3, peer_wq[1], peer_keys[1], peer_u[1], peer_v[1], tab1, row_lat, final_norm, True)
    return out.reshape(bsz, GRID_W, rows, dm).transpose(0, 2, 1, 3).reshape(bsz, seq, dm)
```

```python
import functools
import math

import jax
import jax.numpy as jnp
import numpy as np
from jax import lax
from jax.experimental import pallas as pl
from jax.experimental.pallas import tpu as pltpu

F32 = jnp.float32
BF16 = jnp.bfloat16
HI = lax.Precision.HIGHEST

EPS = 1e-6
GRID_W = 64
CHUNK = 64
LANES = 128
SUBLANES = 8
HALO = 8
CONV_W = 4
LRU_C = 8.0
PEER_TOPK = 16
N_KEYS = 128
POOL_SIZES = (2, 4, 8, 16)
TOK_BLK = 256
PEER_BLK = 2 * TOK_BLK
ROW_BLK = 256
EXP_BLK = 1024
NEG_BIG = -1e30
VMEM_LIMIT = 48 * 1024 * 1024


def _cparams(sem):
    return pltpu.CompilerParams(dimension_semantics=sem, vmem_limit_bytes=VMEM_LIMIT)


def _nt(a, b, precision=None):
    return lax.dot_general(a, b, (((1,), (1,)), ((), ())), precision=precision, preferred_element_type=F32)


def _tn(a, b, precision=None):
    return lax.dot_general(a, b, (((0,), (0,)), ((), ())), precision=precision, preferred_element_type=F32)


def _mm(a, b, precision=None):
    return jnp.dot(a, b, precision=precision, preferred_element_type=F32)


def _silu(x):
    return x * jax.nn.sigmoid(x)


def _softplus(x):
    return jnp.maximum(x, 0.0) + jnp.log1p(jnp.exp(-jnp.abs(x)))


def _gelu_tanh(x):
    return 0.5 * x * (1.0 + jnp.tanh(math.sqrt(2.0 / math.pi) * (x + 0.044715 * (x * x * x))))


def _iota(shape, dim):
    return lax.broadcasted_iota(jnp.int32, shape, dim)


def _ada_kernel(c_ref, w_ref, b_ref, o_ref):
    o_ref[...] = _mm(_silu(c_ref[...]), w_ref[...], HI) + b_ref[...]


def _ada(cc, w, b):
    r, d = cc.shape
    n = w.shape[1]
    tn = 1536
    return pl.pallas_call(
        _ada_kernel,
        grid=(n // tn,),
        in_specs=[pl.BlockSpec((r, d), lambda j: (0, 0)),
                  pl.BlockSpec((d, tn), lambda j: (0, j)),
                  pl.BlockSpec((1, tn), lambda j: (0, j))],
        out_specs=pl.BlockSpec((r, tn), lambda j: (0, j)),
        out_shape=jax.ShapeDtypeStruct((r, n), F32),
        compiler_params=_cparams(("arbitrary",)),
        name="ada_mod",
    )(cc, w, b.reshape(1, n))


def _mod_table(c, c_ctx, w, b):
    bsz, d = c.shape
    rows = ((bsz + 1 + SUBLANES - 1) // SUBLANES) * SUBLANES
    cc = jnp.zeros((rows, d), F32).at[:bsz].set(c).at[bsz].set(c_ctx)
    m = _ada(cc, w, b).reshape(rows, 6, d)
    tab = jnp.stack([jnp.broadcast_to(m[bsz], (bsz, 6, d)), m[:bsz]], axis=1)
    return tab.reshape(bsz * 2 * 6, 1, d)


def _nmm_kernel(x_ref, g_ref, sh_ref, sc_ref, w_ref, *o_refs, widths):
    x = x_ref[...]
    y = x * lax.rsqrt(jnp.mean(x * x, axis=-1, keepdims=True) + EPS) * g_ref[...]
    h = (y * (1.0 + sc_ref[0]) + sh_ref[0]).astype(BF16)
    off = 0
    for o_ref, n in zip(o_refs, widths):
        o_ref[...] = _mm(h, w_ref[:, off:off + n])
        off += n


def _norm_mod_proj(x, gain, tab, rowfn, w, widths):
    t, d = x.shape
    n = w.shape[1]
    tm = TOK_BLK
    return pl.pallas_call(
        functools.partial(_nmm_kernel, widths=widths),
        grid=(t // tm,),
        in_specs=[pl.BlockSpec((tm, d), lambda i: (i, 0)),
                  pl.BlockSpec((1, d), lambda i: (0, 0)),
                  pl.BlockSpec((1, 1, d), lambda i: (rowfn(i), 0, 0)),
                  pl.BlockSpec((1, 1, d), lambda i: (rowfn(i) + 1, 0, 0)),
                  pl.BlockSpec((d, n), lambda i: (0, 0))],
        out_specs=[pl.BlockSpec((tm, wd), lambda i: (i, 0)) for wd in widths],
        out_shape=[jax.ShapeDtypeStruct((t, wd), F32) for wd in widths],
        compiler_params=_cparams(("arbitrary",)),
        name="norm_mod_proj",
    )(x, gain.reshape(1, d), tab, tab, w)


def _conv_chunk(pad_ref, pbase, w, rows=CHUNK):
    blk = pad_ref[pl.ds(pbase - HALO, rows + 2 * HALO), :]
    n = rows + 2 * HALO
    lo, hi = HALO, HALO + rows
    xm1 = pltpu.roll(blk, 1, 0)[lo:hi]
    x0 = blk[lo:hi]
    xp1 = pltpu.roll(blk, n - 1, 0)[lo:hi]
    xp2 = pltpu.roll(blk, n - 2, 0)[lo:hi]
    return w[0:1] * xm1 + w[1:2] * x0 + w[2:3] * xp1 + w[3:4] * xp2


def _fill_padded(pad_ref, src_ref, n_ctx, n_lat):
    width = pad_ref.shape[1]
    z = jnp.zeros((HALO, width), F32)
    pad_ref[0:HALO, :] = z
    pad_ref[HALO + n_ctx:2 * HALO + n_ctx, :] = z
    pad_ref[2 * HALO + n_ctx + n_lat:3 * HALO + n_ctx + n_lat, :] = z
    pad_ref[HALO:HALO + n_ctx, :] = src_ref[0, 0:n_ctx, :]
    pad_ref[2 * HALO + n_ctx:2 * HALO + n_ctx + n_lat, :] = src_ref[0, n_ctx:n_ctx + n_lat, :]


def _padded_base(c, ncc, rows=CHUNK):
    return pl.multiple_of(HALO + c * rows + jnp.where(c >= ncc, HALO, 0), SUBLANES)


def _lane_col(g, lane_idx):
    lane = _iota(g.shape, 1)
    return jnp.sum(jnp.where(lane == lane_idx, g, 0.0), axis=1, keepdims=True)


def _tri(d):
    r = _iota((CHUNK, CHUNK), 0)
    c = _iota((CHUNK, CHUNK), 1)
    incl = (r >= c) if d == 0 else (r <= c)
    strict = (r > c) if d == 0 else (r < c)
    return incl, strict


def _pair_diff(a_rows, b_rows):
    lane = _iota((CHUNK, LANES), 1)
    a = jnp.where(lane == 0, a_rows, jnp.where(lane == 1, 1.0, 0.0))
    b = jnp.where(lane == 0, 1.0, jnp.where(lane == 1, b_rows, 0.0))
    return _nt(a, b, HI)


def _bwd_chunk(s, ncc, nct):
    return jnp.where(s < ncc, ncc - 1 - s, nct + ncc - 1 - s)


def _split3(x):
    x1 = x.astype(BF16)
    r1 = x - x1.astype(F32)
    x2 = r1.astype(BF16)
    x3 = (r1 - x2.astype(F32)).astype(BF16)
    return x1, x2, x3


def _mask_mm(mask, x):
    x1, x2, x3 = _split3(x)
    return _mm(mask, x1) + (_mm(mask, x2) + _mm(mask, x3))


def _mask_tn(x, mask):
    x1, x2, x3 = _split3(x)
    return _tn(x1, mask) + (_tn(x2, mask) + _tn(x3, mask))


def _gate_kernel(ab_ref, gp_ref, gc_ref, gr_ref, *, n_chunks, n_heads, kind):
    r = _iota((CHUNK, CHUNK), 0)
    c = _iota((CHUNK, CHUNK), 1)
    lower = (r >= c).astype(BF16)
    upper = (r <= c).astype(BF16)
    eye = (r == c).astype(BF16)
    lane = _iota((CHUNK, LANES), 1)
    rowg = _iota((LANES, CHUNK), 0)
    if kind == "gdn":
        fwd_lo, rev_lo, rev_hi = 0, n_heads, 2 * n_heads
    else:
        fwd_lo, rev_lo, rev_hi = 2 * n_heads, 3 * n_heads, 4 * n_heads
    p0 = gp_ref[0:1, :]
    p1 = gp_ref[1:2, :]

    def body(ci, carry):
        r0 = pl.multiple_of(ci * CHUNK, CHUNK)
        raw = ab_ref[0, pl.ds(r0, CHUNK), :]
        if kind == "gdn":
            g = jnp.where(lane < 2 * n_heads, -jnp.exp(p0) * _softplus(raw + p1), jax.nn.sigmoid(raw))
        else:
            g = jnp.where(lane < 2 * n_heads, raw + p0, -_softplus(-(raw + p1)))
        is_fwd = (lane >= fwd_lo) & (lane < rev_lo)
        is_rev = (lane >= rev_lo) & (lane < rev_hi)
        gc_ref[0, pl.ds(r0, CHUNK), :] = jnp.where(is_fwd, _mask_mm(lower, g), jnp.where(is_rev, _mask_mm(upper, g), g))
        row_fwd = (rowg >= fwd_lo) & (rowg < rev_lo)
        row_rev = (rowg >= rev_lo) & (rowg < rev_hi)
        rows = jnp.where(row_fwd, _mask_tn(g, upper), jnp.where(row_rev, _mask_tn(g, lower), _mask_tn(g, eye)))
        gr_ref[0, ci] = rows[0:2 * SUBLANES, :]
        return carry

    lax.fori_loop(0, n_chunks, body, 0)


def _gates(ab, gparams, n_heads, kind):
    bsz, L, _ = ab.shape
    nct = L // CHUNK
    assert 4 * n_heads <= 2 * SUBLANES
    return pl.pallas_call(
        functools.partial(_gate_kernel, n_chunks=nct, n_heads=n_heads, kind=kind),
        grid=(bsz,),
        in_specs=[pl.BlockSpec((1, L, LANES), lambda b: (b, 0, 0)),
                  pl.BlockSpec((SUBLANES, LANES), lambda b: (0, 0))],
        out_specs=[pl.BlockSpec((1, L, LANES), lambda b: (b, 0, 0)),
                   pl.BlockSpec((1, nct, 2 * SUBLANES, CHUNK), lambda b: (b, 0, 0, 0))],
        out_shape=[jax.ShapeDtypeStruct((bsz, L, LANES), F32),
                   jax.ShapeDtypeStruct((bsz, nct, 2 * SUBLANES, CHUNK), F32)],
        compiler_params=_cparams(("arbitrary",)),
        name=kind + "_gates",
    )(ab, gparams)


def _mm3(a, b):
    ah = a.astype(BF16)
    al = (a - ah.astype(F32)).astype(BF16)
    bh = b.astype(BF16)
    bl = (b - bh.astype(F32)).astype(BF16)
    return _mm(ah, bh) + (_mm(ah, bl) + _mm(al, bh))


def _gdn_kernel(q_ref, k_ref, v_ref, z_ref, gc_ref, gr_ref, cwq_ref, cwk_ref, cwv_ref, nw_ref, o_ref,
                pad_ref, qs_ref, ks_ref, vs_ref, acc_ref, u_ref, w_ref, qd_ref, kd_ref, qk_ref, gl_ref,
                *, n_ctx, n_lat, n_heads):
    h = pl.program_id(1)
    ncc = n_ctx // CHUNK
    nct = (n_ctx + n_lat) // CHUNK
    dk = qs_ref.shape[1]

    def prep(src_ref, cw_ref, dst_ref, mode):
        _fill_padded(pad_ref, src_ref, n_ctx, n_lat)
        w = cw_ref[...]

        def body(c, carry):
            y = _silu(_conv_chunk(pad_ref, _padded_base(c, n_ctx // ROW_BLK, ROW_BLK), w, ROW_BLK))
            if mode != "v":
                y = y * lax.rsqrt(jnp.sum(y * y, axis=-1, keepdims=True) + EPS)
            if mode == "q":
                y = y * (dk ** -0.5)
            dst_ref[pl.ds(pl.multiple_of(c * ROW_BLK, ROW_BLK), ROW_BLK), :] = y
            return carry

        lax.fori_loop(0, (n_ctx + n_lat) // ROW_BLK, body, 0)

    prep(q_ref, cwq_ref, qs_ref, "q")
    prep(k_ref, cwk_ref, ks_ref, "k")
    prep(v_ref, cwv_ref, vs_ref, "v")

    n_chain = 4
    big = n_chain * CHUNK
    rr = _iota((big, big), 0)
    cc = _iota((big, big), 1)
    log_chunk = CHUNK.bit_length() - 1
    same_blk = lax.shift_right_logical(rr, log_chunk) == lax.shift_right_logical(cc, log_chunk)
    bwd_blk = (lax.shift_right_logical(rr, log_chunk) & 1) == 1
    fwd_blk = jnp.logical_not(bwd_blk)
    strict_bd = same_blk & ((bwd_blk & (rr < cc)) | (fwd_blk & (rr > cc)))
    incl_bd = same_blk & ((bwd_blk & (rr <= cc)) | (fwd_blk & (rr >= cc)))

    def prepare_group(c0):
        cols, ks, rhs, gcols, bcols, grows = [], [], [], [], [], []
        chains = []
        for j in range(2):
            c = c0 + j
            r0 = pl.multiple_of(c * CHUNK, CHUNK)
            acc_ref[pl.ds(r0, CHUNK), :] = jnp.zeros((CHUNK, dk), F32)
            q = qs_ref[pl.ds(r0, CHUNK), :]
            k = ks_ref[pl.ds(r0, CHUNK), :]
            v = vs_ref[pl.ds(r0, CHUNK), :]
            gates = gc_ref[0, pl.ds(r0, CHUNK), :]
            grt = gr_ref[0, c]
            for d in range(2):
                gcol = _lane_col(gates, d * n_heads + h)
                bcol = _lane_col(gates, 2 * n_heads + d * n_heads + h)
                grow = jnp.sum(jnp.where(_iota(grt.shape, 0) == d * n_heads + h, grt, 0.0), axis=0, keepdims=True)
                gtot = gcol[CHUNK - 1:CHUNK, :] if d == 0 else gcol[0:1, :]
                eg = jnp.exp(gcol)
                ks.append(k)
                gcols.append(gcol)
                bcols.append(bcol)
                grows.append(grow)
                rhs.append(jnp.concatenate([bcol * v, (bcol * eg) * k], axis=1))
                chains.append((c, d, r0, q, k, gcol, grow, gtot, eg))
        kst = jnp.concatenate(ks, axis=0).astype(BF16)
        gam = jnp.where(incl_bd, jnp.exp(jnp.minimum(jnp.concatenate(gcols, axis=0) - jnp.concatenate(grows, axis=1),
                                                      0.0)), 0.0)
        m_bd = jnp.where(strict_bd, jnp.concatenate(bcols, axis=0) * _nt(kst, kst) * gam, 0.0)
        sol = jnp.concatenate(rhs, axis=0)
        sol = sol - _mm3(m_bd, sol)
        p = _mm3(m_bd, m_bd)
        sol = sol + _mm3(p, sol)
        for it in range(4):
            p = p.astype(BF16)
            p = _mm(p, p)
            sol = sol + _mm(p.astype(BF16), sol.astype(BF16))
        for b, (c, d, r0, q, k, gcol, grow, gtot, eg) in enumerate(chains):
            incl, _ = _tri(d)
            gamma = jnp.where(incl, jnp.exp(jnp.minimum(gcol - grow, 0.0)), 0.0)
            u_ref[d, pl.ds(r0, CHUNK), :] = sol[b * CHUNK:(b + 1) * CHUNK, :dk]
            w_ref[d, pl.ds(r0, CHUNK), :] = sol[b * CHUNK:(b + 1) * CHUNK, dk:].astype(BF16)
            qd_ref[d, pl.ds(r0, CHUNK), :] = (q * eg).astype(BF16)
            kd_ref[d, pl.ds(r0, CHUNK), :] = (k * jnp.exp(gtot - gcol)).astype(BF16)
            qk_ref[d, c] = (_nt(q.astype(BF16), k.astype(BF16)) * gamma).astype(BF16)
            gl_ref[d, c] = jnp.broadcast_to(jnp.exp(gtot), (SUBLANES, LANES))

    def prepare_body(c2, carry):
        prepare_group(2 * c2)
        return carry

    lax.fori_loop(0, nct // 2, prepare_body, 0)

    def step(s, carry):
        cs = (s, _bwd_chunk(s, ncc, nct))
        r0 = [pl.multiple_of(c * CHUNK, CHUNK) for c in cs]
        sb = [st.astype(BF16) for st in carry]
        ws = [_mm(w_ref[d, pl.ds(r0[d], CHUNK), :], sb[d]) for d in range(2)]
        qs = [_mm(qd_ref[d, pl.ds(r0[d], CHUNK), :], sb[d]) for d in range(2)]
        vb = [(u_ref[d, pl.ds(r0[d], CHUNK), :] - ws[d]).astype(BF16) for d in range(2)]
        kv = [_tn(kd_ref[d, pl.ds(r0[d], CHUNK), :], vb[d]) for d in range(2)]
        ov = [_mm(qk_ref[d, cs[d]], vb[d]) for d in range(2)]
        for d in range(2):
            acc_ref[pl.ds(r0[d], CHUNK), :] += qs[d] + ov[d]
        return tuple(carry[d] * gl_ref[d, cs[d]][0:1, :] + kv[d] for d in range(2))

    zero = jnp.zeros((dk, dk), F32)
    lax.fori_loop(0, nct, step, (zero, zero))

    nw = nw_ref[...]

    def out_body(c, carry):
        r0 = pl.multiple_of(c * ROW_BLK, ROW_BLK)
        o = acc_ref[pl.ds(r0, ROW_BLK), :]
        y = o * lax.rsqrt(jnp.mean(o * o, axis=-1, keepdims=True) + EPS) * nw
        o_ref[0, pl.ds(r0, ROW_BLK), :] = y * _silu(z_ref[0, pl.ds(r0, ROW_BLK), :])
        return carry

    lax.fori_loop(0, (n_ctx + n_lat) // ROW_BLK, out_body, 0)


def _gdn(qkv, z, ab, conv_w, gparams, norm_w, n_ctx, n_lat, n_heads):
    bsz, L, _ = qkv.shape
    dk = LANES
    seq = lambda off: pl.BlockSpec((1, L, dk), lambda b, h: (b, 0, off + h))
    cw = lambda off: pl.BlockSpec((CONV_W, dk), lambda b, h: (0, off + h))
    lp = L + 3 * HALO
    nct = L // CHUNK
    assert nct % 2 == 0 and n_ctx % ROW_BLK == 0 and n_lat % ROW_BLK == 0
    gc, gr = _gates(ab, gparams, n_heads, "gdn")
    return pl.pallas_call(
        functools.partial(_gdn_kernel, n_ctx=n_ctx, n_lat=n_lat, n_heads=n_heads),
        grid=(bsz, n_heads),
        in_specs=[seq(0), seq(n_heads), seq(2 * n_heads),
                  pl.BlockSpec((1, L, dk), lambda b, h: (b, 0, h)),
                  pl.BlockSpec((1, L, LANES), lambda b, h: (b, 0, 0)),
                  pl.BlockSpec((1, nct, 2 * SUBLANES, CHUNK), lambda b, h: (b, 0, 0, 0)),
                  cw(0), cw(n_heads), cw(2 * n_heads),
                  pl.BlockSpec((1, dk), lambda b, h: (0, 0))],
        out_specs=pl.BlockSpec((1, L, dk), lambda b, h: (b, 0, h)),
        out_shape=jax.ShapeDtypeStruct((bsz, L, n_heads * dk), F32),
        scratch_shapes=[pltpu.VMEM((lp, dk), F32)] + [pltpu.VMEM((L, dk), F32) for _ in range(4)]
        + [pltpu.VMEM((2, L, dk), F32)] + [pltpu.VMEM((2, L, dk), BF16) for _ in range(3)]
        + [pltpu.VMEM((2, nct, CHUNK, CHUNK), BF16), pltpu.VMEM((2, nct, SUBLANES, LANES), F32)],
        compiler_params=_cparams(("arbitrary", "arbitrary")),
        name="gdn_mixer",
    )(qkv, qkv, qkv, z, gc, gr, conv_w, conv_w, conv_w, norm_w.reshape(1, dk))


def _lru_kernel(x_ref, gt_ref, cw_ref, cb_ref, wa_ref, wx_ref, pv_ref, o_ref,
                pad_ref, xc_ref, a_ref, b_ref, acc_ref, *, n_ctx, n_lat):
    ncc = n_ctx // CHUNK
    nct = (n_ctx + n_lat) // CHUNK
    L = n_ctx + n_lat
    width = xc_ref.shape[1]
    _fill_padded(pad_ref, x_ref, n_ctx, n_lat)
    w = cw_ref[...]
    cb = cb_ref[...]

    def conv_body(c, carry):
        r0 = pl.multiple_of(c * CHUNK, CHUNK)
        xc_ref[pl.ds(r0, CHUNK), :] = _conv_chunk(pad_ref, _padded_base(c, ncc), w) + cb
        acc_ref[pl.ds(r0, CHUNK), :] = jnp.zeros((CHUNK, width), F32)
        return carry

    lax.fori_loop(0, nct, conv_body, 0)

    row = _iota((SUBLANES, width), 0)
    bias_a = [pv_ref[d, 0:1, :] for d in range(2)]
    bias_x = [pv_ref[d, 1:2, :] for d in range(2)]
    decay = [-LRU_C * _softplus(-pv_ref[d, 2:3, :]) for d in range(2)]

    def coef_body(c, carry):
        r0 = pl.multiple_of(c * CHUNK, CHUNK)
        x = xc_ref[pl.ds(r0, CHUNK), :]
        xb = x.astype(BF16)
        for d in range(2):
            r = jax.nn.sigmoid(_mm(xb, wa_ref[d, 0]) + bias_a[d])
            i = jax.nn.sigmoid(_mm(xb, wx_ref[d, 0]) + bias_x[d])
            log_a = decay[d] * r
            a_ref[d, pl.ds(r0, CHUNK), :] = jnp.exp(log_a)
            th = jnp.tanh(log_a)
            b_ref[d, pl.ds(r0, CHUNK), :] = jnp.sqrt(-2.0 * th / (1.0 - th)) * (i * x)
        return carry

    lax.fori_loop(0, nct, coef_body, 0)

    nt_ctx = n_ctx // SUBLANES
    nt_all = L // SUBLANES

    def scan_tile(t, d, hc):
        r0 = pl.multiple_of(t * SUBLANES, SUBLANES)
        a = a_ref[d, pl.ds(r0, SUBLANES), :]
        b = b_ref[d, pl.ds(r0, SUBLANES), :]
        for sh in (1, 2, 4):
            if d == 0:
                a_s = pltpu.roll(a, sh, 0)
                b_s = pltpu.roll(b, sh, 0)
                m = row >= sh
            else:
                a_s = pltpu.roll(a, SUBLANES - sh, 0)
                b_s = pltpu.roll(b, SUBLANES - sh, 0)
                m = row < SUBLANES - sh
            b = jnp.where(m, a * b_s + b, b)
            a = jnp.where(m, a * a_s, a)
        hcur = b + a * hc
        acc_ref[pl.ds(r0, SUBLANES), :] += hcur
        last = hcur[SUBLANES - 1:SUBLANES, :] if d == 0 else hcur[0:1, :]
        return jnp.broadcast_to(last, (SUBLANES, width))

    def scan_body(s, carry):
        hf, hb = carry
        tb = jnp.where(s < nt_ctx, nt_ctx - 1 - s, nt_all + nt_ctx - 1 - s)
        return scan_tile(s, 0, hf), scan_tile(tb, 1, hb)

    zero = jnp.zeros((SUBLANES, width), F32)
    lax.fori_loop(0, nt_all, scan_body, (zero, zero))


    def out_body(c, carry):
        r0 = pl.multiple_of(c * CHUNK, CHUNK)
        o_ref[0, pl.ds(r0, CHUNK), :] = acc_ref[pl.ds(r0, CHUNK), :] * _gelu_tanh(gt_ref[0, pl.ds(r0, CHUNK), :])
        return carry

    lax.fori_loop(0, nct, out_body, 0)


def _lru(xb, gate, conv_w, conv_b, wa, wx, pvec, n_ctx, n_lat):
    bsz, L, width = xb.shape
    ng = width // LANES
    lp = L + 3 * HALO
    seq = pl.BlockSpec((1, L, LANES), lambda b, j: (b, 0, j))
    return pl.pallas_call(
        functools.partial(_lru_kernel, n_ctx=n_ctx, n_lat=n_lat),
        grid=(bsz, ng),
        in_specs=[seq, seq,
                  pl.BlockSpec((CONV_W, LANES), lambda b, j: (0, j)),
                  pl.BlockSpec((1, LANES), lambda b, j: (0, j)),
                  pl.BlockSpec((2, 1, LANES, LANES), lambda b, j: (0, j, 0, 0)),
                  pl.BlockSpec((2, 1, LANES, LANES), lambda b, j: (0, j, 0, 0)),
                  pl.BlockSpec((2, SUBLANES, LANES), lambda b, j: (0, 0, j))],
        out_specs=seq,
        out_shape=jax.ShapeDtypeStruct((bsz, L, width), F32),
        scratch_shapes=[pltpu.VMEM((lp, LANES), F32), pltpu.VMEM((L, LANES), F32), pltpu.VMEM((2, L, LANES), F32),
                        pltpu.VMEM((2, L, LANES), F32), pltpu.VMEM((L, LANES), F32)],
        compiler_params=_cparams(("arbitrary", "arbitrary")),
        name="lru_mixer",
    )(xb, gate, conv_w, conv_b.reshape(1, width), wa, wx, pvec)


def _mlstm_kernel(qc_ref, kc_ref, vc_ref, gcc_ref, grc_ref, ql_ref, kl_ref, vl_ref, gcl_ref, grl_ref, og_ref, nw_ref,
                  o_ref, acc_ref, cc_ref, ms_ref, p2_ref, mi_ref, bc_ref, *, n_ctx, n_lat, n_heads, dqk):
    h = pl.program_id(1)
    ncc = n_ctx // CHUNK
    ncl = n_lat // CHUNK
    dv = LANES
    lane = _iota((CHUNK, LANES), 1)
    ones_blk = jnp.ones((CHUNK, LANES), F32)

    def prepare(refs, slot0, chains, with_out):
        q_ref, k_ref, v_ref, gcol_ref, grow_ref = refs
        idx = range(len(chains))
        r0 = [pl.multiple_of(c * CHUNK, CHUNK) for c, _ in chains]
        k = [k_ref[0, pl.ds(r0[i], CHUNK), :] * (dqk ** -0.5) for i in idx]
        v_ext = [jnp.concatenate([v_ref[0, pl.ds(r0[i], CHUNK), :], ones_blk], axis=1).astype(BF16) for i in idx]
        gates = [gcol_ref[0, pl.ds(r0[i], CHUNK), :] for i in idx]
        li = [_lane_col(gates[i], chains[i][1] * n_heads + h) for i in idx]
        bcum = [_lane_col(gates[i], 2 * n_heads + chains[i][1] * n_heads + h) for i in idx]
        btot = [bcum[i][CHUNK - 1:CHUNK, :] if chains[i][1] == 0 else bcum[i][0:1, :] for i in idx]
        w_end = [btot[i] - bcum[i] + li[i] for i in idx]
        m_chunk = [jnp.max(w_end[i], axis=0, keepdims=True) for i in idx]
        c_chunk = [_tn((k[i] * jnp.exp(w_end[i] - m_chunk[i])).astype(BF16), v_ext[i]) for i in idx]
        if with_out:
            qk = [_nt(q_ref[0, pl.ds(r0[i], CHUNK), :].astype(BF16), k[i].astype(BF16)) for i in idx]
            s_qk, m_intra = [], []
            for i in idx:
                c, d = chains[i]
                grt = grow_ref[0, c]
                rowi = _iota(grt.shape, 0)
                li_row = jnp.sum(jnp.where(rowi == d * n_heads + h, grt, 0.0), axis=0, keepdims=True)
                b_row = jnp.sum(jnp.where(rowi == 2 * n_heads + d * n_heads + h, grt, 0.0), axis=0, keepdims=True)
                incl, _ = _tri(d)
                log_d = jnp.where(incl, bcum[i] + (li_row - b_row), NEG_BIG)
                m_intra.append(jnp.max(log_d, axis=1, keepdims=True))
                s_qk.append((qk[i] * jnp.exp(log_d - m_intra[i])).astype(BF16))
            p2 = [_mm(s_qk[i], v_ext[i]) for i in idx]
        row8 = _iota((SUBLANES, LANES), 0)
        for i in idx:
            c, d = chains[i]
            cc_ref[d, slot0 + c] = c_chunk[i]
            ms_ref[d, slot0 + c] = jnp.where(row8 == 0, btot[i], m_chunk[i])
            if with_out:
                p2_ref[d, pl.ds(r0[i], CHUNK), :] = p2[i]
                mi_ref[d, pl.ds(r0[i], CHUNK), :] = jnp.broadcast_to(m_intra[i], (CHUNK, LANES))
                bc_ref[d, pl.ds(r0[i], CHUNK), :] = jnp.broadcast_to(bcum[i], (CHUNK, LANES))

    ctx_refs = (qc_ref, kc_ref, vc_ref, gcc_ref, grc_ref)
    lat_refs = (ql_ref, kl_ref, vl_ref, gcl_ref, grl_ref)

    def ctx_prep(c2, carry):
        prepare(ctx_refs, 0, [(2 * c2 + j, d) for j in range(2) for d in range(2)], False)
        return carry

    lax.fori_loop(0, ncc // 2, ctx_prep, 0)

    def lat_prep(c2, carry):
        for j in range(2):
            acc_ref[pl.ds(pl.multiple_of((2 * c2 + j) * CHUNK, CHUNK), CHUNK), :] = jnp.zeros((CHUNK, dv), F32)
        prepare(lat_refs, ncc, [(2 * c2 + j, d) for j in range(2) for d in range(2)], True)
        return carry

    lax.fori_loop(0, ncl // 2, lat_prep, 0)

    def advance(states, slots):
        out = []
        for d in range(2):
            cx, m_s = states[d]
            ms = ms_ref[d, slots[d]]
            btot = ms[0:1, :]
            m_chunk = ms[1:2, :]
            m_new = jnp.maximum(btot + m_s, m_chunk)
            f_old = jnp.exp(btot + m_s - m_new)
            f_new = jnp.exp(m_chunk - m_new)
            wide = lambda f: jnp.concatenate([f, f], axis=1)
            out.append((wide(f_old) * cx + wide(f_new) * cc_ref[d, slots[d]], m_new))
        return tuple(out)

    def ctx_step(s, carry):
        return advance(carry, (s, ncc - 1 - s))

    zero = (jnp.zeros((LANES, 2 * dv), F32), jnp.zeros((1, LANES), F32))
    carry = lax.fori_loop(0, ncc, ctx_step, (zero, zero))

    def lat_step(s, carry):
        cs = (s, ncl - 1 - s)
        r0 = [pl.multiple_of(c * CHUNK, CHUNK) for c in cs]
        p1 = [_mm(ql_ref[0, pl.ds(r0[d], CHUNK), :].astype(BF16), carry[d][0].astype(BF16)) for d in range(2)]
        for d in range(2):
            m_intra = mi_ref[d, pl.ds(r0[d], CHUNK), :]
            m_inter = bc_ref[d, pl.ds(r0[d], CHUNK), :] + carry[d][1]
            m_tot = jnp.maximum(m_inter, m_intra)
            w_inter = jnp.exp(m_inter - m_tot)
            w_intra = jnp.exp(m_intra - m_tot)
            p2 = p2_ref[d, pl.ds(r0[d], CHUNK), :]
            num = w_inter * p1[d][:, :dv] + w_intra * p2[:, :dv]
            den = w_inter * p1[d][:, dv:] + w_intra * p2[:, dv:]
            acc_ref[pl.ds(r0[d], CHUNK), :] += num / jnp.maximum(jnp.abs(den), jnp.exp(-m_tot))
        return advance(carry, (ncc + cs[0], ncc + cs[1]))

    lax.fori_loop(0, ncl, lat_step, carry)

    nw = nw_ref[...]

    def out_body(c, carry):
        r0 = pl.multiple_of(c * ROW_BLK, ROW_BLK)
        hs = acc_ref[pl.ds(r0, ROW_BLK), :]
        y = hs * lax.rsqrt(jnp.mean(hs * hs, axis=-1, keepdims=True) + EPS) * nw
        o_ref[0, pl.ds(r0, ROW_BLK), :] = y * jax.nn.sigmoid(og_ref[0, pl.ds(r0, ROW_BLK), :])
        return carry

    lax.fori_loop(0, n_lat // ROW_BLK, out_body, 0)


def _mlstm(qc, kc, vc, gc, ql, kl, vl, gl, og, gparams, norm_w, n_heads, dqk):
    bsz, n_ctx, _ = qc.shape
    n_lat = ql.shape[1]
    cs = pl.BlockSpec((1, n_ctx, LANES), lambda b, h: (b, 0, h))
    ls = pl.BlockSpec((1, n_lat, LANES), lambda b, h: (b, 0, h))
    assert n_ctx % (2 * CHUNK) == 0 and n_lat % (2 * CHUNK) == 0 and n_lat % ROW_BLK == 0
    gcc, grc = _gates(gc, gparams, n_heads, "mlstm")
    gcl, grl = _gates(gl, gparams, n_heads, "mlstm")
    row_spec = lambda n: pl.BlockSpec((1, n // CHUNK, 2 * SUBLANES, CHUNK), lambda b, h: (b, 0, 0, 0))
    return pl.pallas_call(
        functools.partial(_mlstm_kernel, n_ctx=n_ctx, n_lat=n_lat, n_heads=n_heads, dqk=dqk),
        grid=(bsz, n_heads),
        in_specs=[cs, cs, cs, pl.BlockSpec((1, n_ctx, LANES), lambda b, h: (b, 0, 0)), row_spec(n_ctx),
                  ls, ls, ls, pl.BlockSpec((1, n_lat, LANES), lambda b, h: (b, 0, 0)), row_spec(n_lat),
                  ls,
                  pl.BlockSpec((1, LANES), lambda b, h: (0, h))],
        out_specs=ls,
        out_shape=jax.ShapeDtypeStruct((bsz, n_lat, n_heads * LANES), F32),
        scratch_shapes=[pltpu.VMEM((n_lat, LANES), F32),
                        pltpu.VMEM((2, (n_ctx + n_lat) // CHUNK, LANES, 2 * LANES), F32),
                        pltpu.VMEM((2, (n_ctx + n_lat) // CHUNK, SUBLANES, LANES), F32),
                        pltpu.VMEM((2, n_lat, 2 * LANES), F32),
                        pltpu.VMEM((2, n_lat, LANES), F32),
                        pltpu.VMEM((2, n_lat, LANES), F32)],
        compiler_params=_cparams(("arbitrary", "arbitrary")),
        name="mlstm_mixer",
    )(qc, kc, vc, gcc, grc, ql, kl, vl, gcl, grl, og, norm_w.reshape(1, -1))


def _pool_kernel(x_ref, w_ref, sc_ref, o_ref, *, seg):
    tm = x_ref.shape[0]
    r = _iota((tm, tm), 0)
    c = _iota((tm, tm), 1)
    sh = seg.bit_length() - 1
    same = lax.shift_right_logical(r, sh) == lax.shift_right_logical(c, sh)
    t = r & (seg - 1)
    s = c & (seg - 1)
    for gi, wsz in enumerate(POOL_SIZES):
        lo = jnp.maximum(t - wsz // 2, 0)
        hi = jnp.minimum(t - wsz // 2 + wsz, seg)
        inwin = same & (s >= lo) & (s < hi)
        pmat = jnp.where(inwin, 1.0 / (hi - lo).astype(F32), 0.0) - jnp.where(r == c, 1.0, 0.0)
        x = x_ref[:, gi * LANES:(gi + 1) * LANES]
        pooled = _mm3(pmat, x)
        y = _mm(pooled.astype(BF16), w_ref[gi].astype(BF16))
        o_ref[:, gi * LANES:(gi + 1) * LANES] = y * sc_ref[:, gi * LANES:(gi + 1) * LANES]


def _pool(x, w_grp, scale, seg):
    t, width = x.shape
    tm = TOK_BLK
    assert seg & (seg - 1) == 0 and tm % seg == 0
    return pl.pallas_call(
        functools.partial(_pool_kernel, seg=seg),
        grid=(t // tm,),
        in_specs=[pl.BlockSpec((tm, width), lambda i: (i, 0)),
                  pl.BlockSpec(w_grp.shape, lambda i: (0, 0, 0)),
                  pl.BlockSpec((1, width), lambda i: (0, 0))],
        out_specs=pl.BlockSpec((tm, width), lambda i: (i, 0)),
        out_shape=jax.ShapeDtypeStruct((t, width), F32),
        compiler_params=_cparams(("arbitrary",)),
        name="pool_mixer",
    )(x, w_grp, scale.reshape(1, width))


def _outproj_kernel(x_ref, ya_ref, yb_ref, wa_ref, wb_ref, g1_ref, gn_ref, sh_ref, sc_ref, xo_ref, h_ref):
    y = _mm(ya_ref[...].astype(BF16), wa_ref[...]) + _mm(yb_ref[...].astype(BF16), wb_ref[...])
    xn = x_ref[...] + g1_ref[0] * y
    xo_ref[...] = xn
    hn = xn * lax.rsqrt(jnp.mean(xn * xn, axis=-1, keepdims=True) + EPS) * gn_ref[...]
    h_ref[...] = (hn * (1.0 + sc_ref[0]) + sh_ref[0]).astype(BF16)


def _outproj(x, ya, yb, w_out, gain, tab, rowfn):
    t, d = x.shape
    wa_n = ya.shape[1]
    tm = TOK_BLK
    w = w_out.astype(BF16)
    mod = lambda k: pl.BlockSpec((1, 1, d), lambda i: (rowfn(i) + k, 0, 0))
    return pl.pallas_call(
        _outproj_kernel,
        grid=(t // tm,),
        in_specs=[pl.BlockSpec((tm, d), lambda i: (i, 0)),
                  pl.BlockSpec((tm, wa_n), lambda i: (i, 0)),
                  pl.BlockSpec((tm, yb.shape[1]), lambda i: (i, 0)),
                  pl.BlockSpec((wa_n, d), lambda i: (0, 0)),
                  pl.BlockSpec((yb.shape[1], d), lambda i: (0, 0)),
                  mod(2),
                  pl.BlockSpec((1, d), lambda i: (0, 0)),
                  mod(3), mod(4)],
        out_specs=[pl.BlockSpec((tm, d), lambda i: (i, 0)), pl.BlockSpec((tm, d), lambda i: (i, 0))],
        out_shape=[jax.ShapeDtypeStruct((t, d), F32), jax.ShapeDtypeStruct((t, d), BF16)],
        compiler_params=_cparams(("arbitrary",)),
        name="out_proj",
    )(x, ya, yb, w[:wa_n], w[wa_n:], tab, gain.reshape(1, d), tab, tab)


def _hyperbola_pairs():
    return [(j1, j2) for j1 in range(PEER_TOPK) for j2 in range(PEER_TOPK) if (j1 + 1) * (j2 + 1) <= PEER_TOPK]


def _topk_rows(s, k, break_ties):
    n, t = s.shape
    rowi = _iota((n, t), 0).astype(F32)
    rank = jnp.full((n, t), float(2 * k), F32)
    cur = s
    vals = []
    for j in range(k):
        m = jnp.max(cur, axis=0, keepdims=True)
        sel = cur == m
        if break_ties:
            sel = rowi == jnp.min(jnp.where(sel, rowi, float(n)), axis=0, keepdims=True)
        rank = jnp.where(sel, float(j), rank)
        cur = jnp.where(sel, -jnp.inf, cur)
        vals.append(m)
    return vals, rank


def _candidate_stage(v1, v2, break_ties):
    pairs = _hyperbola_pairs()
    cand = jnp.concatenate([v1[j1] + v2[j2] for j1, j2 in pairs], axis=0)
    _, crank = _topk_rows(cand, PEER_TOPK, break_ties)
    chosen = crank < PEER_TOPK
    zsum = jnp.sum(jnp.where(chosen, jnp.exp(cand - (v1[0] + v2[0])), 0.0), axis=0, keepdims=True)
    counts = jnp.where(chosen, 1.0, 0.0)
    n_by_rank = []
    row = 0
    for j1 in range(PEER_TOPK):
        width = PEER_TOPK // (j1 + 1)
        n_by_rank.append(jnp.sum(counts[row:row + width], axis=0, keepdims=True))
        row += width
    return n_by_rank, zsum, jnp.sum(counts, axis=0, keepdims=True)


def _route(s1, s2):
    v1, rank1 = _topk_rows(s1, PEER_TOPK, True)
    v2, rank2 = _topk_rows(s2, PEER_TOPK, True)
    n_by_rank, zsum, _ = _candidate_stage(v1, v2, True)
    n1 = jnp.zeros_like(rank1)
    for j1 in range(PEER_TOPK):
        n1 = jnp.where(rank1 == float(j1), n_by_rank[j1], n1)
    e2 = jnp.where(rank2 < PEER_TOPK, jnp.exp(s2 - v2[0]), 0.0)
    e1 = jnp.where(rank1 < PEER_TOPK, jnp.exp(s1 - v1[0]), 0.0) / zsum
    return rank2, e2, n1, e1


def _sorted_top(s):
    n = s.shape[0] // SUBLANES
    a = [s[SUBLANES * j:SUBLANES * (j + 1), :] for j in range(n)]

    def cex(i, l):
        a[i], a[l] = jnp.maximum(a[i], a[l]), jnp.minimum(a[i], a[l])

    k = 2
    while k <= n:
        j = k // 2
        while j >= 1:
            for i in range(n):
                l = i ^ j
                if l > i:
                    if (i & k) == 0:
                        cex(i, l)
                    else:
                        cex(l, i)
            j //= 2
        k *= 2
    for shift in (4, 2, 1):
        a = [jnp.maximum(a[j], pltpu.roll(a[n - 1 - j], shift, 0)) for j in range(n)]
        j = n // 2
        while j >= 1:
            for i in range(n):
                l = i ^ j
                if l > i:
                    cex(i, l)
            j //= 2
    return a


def _by_rank(bits, table):
    level = list(table)
    for g in reversed(bits):
        level = [jnp.where(g, level[2 * i + 1], level[2 * i]) for i in range(len(level) // 2)]
    return level[0]


def _rank_bits(x, v):
    g8 = v[7] > x
    g4 = jnp.where(g8, v[11], v[3]) > x
    g2 = _by_rank([g8, g4], [v[1], v[5], v[9], v[13]]) > x
    g1 = _by_rank([g8, g4, g2], [v[2 * i] for i in range(8)]) > x
    return [g8, g4, g2, g1]


def _route_untied(s1, s2):
    v1 = _sorted_top(s1)
    v2 = _sorted_top(s2)
    n_by_rank, zsum, n_cand = _candidate_stage([v[0:1, :] for v in v1], [v[0:1, :] for v in v2], False)
    shape = v1[0].shape
    n_tab = [jnp.broadcast_to(nj, shape) for nj in n_by_rank]
    inv_z = jnp.broadcast_to(1.0 / zsum, shape)
    rank2, e2, n1, e1 = [], [], [], []
    cnt1 = jnp.zeros(shape, F32)
    cnt2 = jnp.zeros(shape, F32)
    for r in range(s1.shape[0] // SUBLANES):
        x1 = s1[SUBLANES * r:SUBLANES * (r + 1), :]
        x2 = s2[SUBLANES * r:SUBLANES * (r + 1), :]
        in1 = x1 >= v1[PEER_TOPK - 1]
        in2 = x2 >= v2[PEER_TOPK - 1]
        n1.append(jnp.where(in1, _by_rank(_rank_bits(x1, v1), n_tab), 0.0))
        e1.append(jnp.where(in1, jnp.exp(x1 - v1[0]) * inv_z, 0.0))
        g8, g4, g2, g1 = _rank_bits(x2, v2)
        rk = (jnp.where(g8, 8.0, 0.0) + jnp.where(g4, 4.0, 0.0)) + (jnp.where(g2, 2.0, 0.0) + jnp.where(g1, 1.0, 0.0))
        rank2.append(jnp.where(in2, rk, float(2 * PEER_TOPK)))
        e2.append(jnp.where(in2, jnp.exp(x2 - v2[0]), 0.0))
        cnt1 = cnt1 + jnp.where(in1, 1.0, 0.0)
        cnt2 = cnt2 + jnp.where(in2, 1.0, 0.0)
    tied = jnp.abs(n_cand - float(PEER_TOPK))
    for cnt in (cnt1, cnt2):
        tied = tied + jnp.abs(jnp.sum(cnt, axis=0, keepdims=True) - float(PEER_TOPK))
    for v in (v1, v2):
        for j in range(PEER_TOPK - 1):
            tied = tied + jnp.where(v[j][0:1, :] == v[j + 1][0:1, :], 1.0, 0.0)
    cat = lambda parts: jnp.concatenate(parts, axis=0)
    return cat(rank2), cat(e2), cat(n1), cat(e1), tied


def _peer_select_kernel(h_ref, wq_ref, keys_ref, r2_ref, e2_ref, n1_ref, e1_ref):
    q = _mm(h_ref[...], wq_ref[...])
    s1 = _nt(keys_ref[0, 0], q[:, :N_KEYS], HI)
    s2 = _nt(keys_ref[0, 1], q[:, N_KEYS:], HI)

    def emit(rank2, e2, n1, e1):
        r2_ref[0] = rank2.astype(r2_ref.dtype)
        e2_ref[0] = e2.astype(e2_ref.dtype)
        n1_ref[0] = n1
        e1_ref[0] = e1

    rank2, e2, n1, e1, tied = _route_untied(s1, s2)
    emit(rank2, e2, n1, e1)

    @pl.when(jnp.max(tied) > 0.0)
    def _():
        emit(*_route(s1, s2))


def _peer_select(h, wq, keys):
    t, d = h.shape
    nh = keys.shape[0]
    qd = wq.shape[1] // nh
    tm = PEER_BLK
    ospec = pl.BlockSpec((1, N_KEYS, tm), lambda i, hh: (hh, 0, i))
    return pl.pallas_call(
        _peer_select_kernel,
        grid=(t // tm, nh),
        in_specs=[pl.BlockSpec((tm, d), lambda i, hh: (i, 0)),
                  pl.BlockSpec((d, qd), lambda i, hh: (0, hh)),
                  pl.BlockSpec((1, 2, N_KEYS, qd // 2), lambda i, hh: (hh, 0, 0, 0))],
        out_specs=[ospec, ospec, ospec, ospec],
        out_shape=[jax.ShapeDtypeStruct((nh, N_KEYS, t), dt) for dt in (BF16, BF16, F32, F32)],
        compiler_params=_cparams(("arbitrary", "arbitrary")),
        name="peer_select",
    )(h, wq, keys)


def _peer_dense_kernel(h_ref, u_ref, vt_ref, r2_ref, e2_ref, n1_ref, e1_ref, x_ref, g2a_ref, g2b_ref, fn_ref, o_ref,
                       acc_ref, wact_ref, *, n_heads, final):
    eb = pl.program_id(1)
    n_eb = pl.num_programs(1) - 1
    cur = eb & 1
    prev = 1 - cur

    @pl.when(eb == 0)
    def _():
        acc_ref[...] = jnp.zeros_like(acc_ref)
        wact_ref[1] = jnp.zeros(wact_ref.shape[1:], BF16)

    hb = h_ref[...]
    n_part = 4
    part_rows = EXP_BLK // n_part
    acts = [_nt(u_ref[p * part_rows:(p + 1) * part_rows, :], hb) for p in range(n_part)]
    upd = _mm(vt_ref[...], wact_ref[prev])
    rows_per = EXP_BLK // N_KEYS
    i1_base = pl.multiple_of(jnp.minimum(eb, n_eb - 1) * rows_per, rows_per)
    n1_tiles = [n1_ref[hh, pl.ds(i1_base, rows_per), :] for hh in range(n_heads)]
    e1_tiles = [e1_ref[hh, pl.ds(i1_base, rows_per), :] for hh in range(n_heads)]
    zero = jnp.zeros((), BF16)
    for r in range(rows_per):
        wgt = None
        for hh in range(n_heads):
            shape = r2_ref.shape[1:]
            n1row = jnp.broadcast_to(n1_tiles[hh][r:r + 1, :], shape).astype(BF16)
            e1row = jnp.broadcast_to(e1_tiles[hh][r:r + 1, :], shape).astype(BF16)
            term = jnp.where(r2_ref[hh] < n1row, e2_ref[hh], zero) * e1row
            wgt = term if wgt is None else wgt + term
        off = (r * N_KEYS) % part_rows
        a = acts[(r * N_KEYS) // part_rows][off:off + N_KEYS, :]
        wact_ref[cur, r * N_KEYS:(r + 1) * N_KEYS, :] = _gelu_tanh(a.astype(BF16)) * wgt
    acc_ref[...] += upd

    @pl.when(eb == n_eb)
    def _():
        for part, g2_ref in enumerate((g2a_ref, g2b_ref)):
            rows = slice(part * TOK_BLK, (part + 1) * TOK_BLK)
            y = x_ref[rows, :] + g2_ref[0] * acc_ref[:, rows].T
            if final:
                y = y * lax.rsqrt(jnp.mean(y * y, axis=-1, keepdims=True) + EPS) * fn_ref[...]
            o_ref[rows, :] = y


def _peer_dense(h, u, vt, r2, e2, n1, e1, x, tab, rowfn, final_gain, final):
    t, d = h.shape
    n_eb = u.shape[0] // EXP_BLK
    nh = r2.shape[0]
    tm = PEER_BLK
    sel = pl.BlockSpec((nh, N_KEYS, tm), lambda i, e: (0, 0, i))
    g2 = lambda part: pl.BlockSpec((1, 1, d), lambda i, e: (rowfn(2 * i + part) + 5, 0, 0))
    return pl.pallas_call(
        functools.partial(_peer_dense_kernel, n_heads=nh, final=final),
        grid=(t // tm, n_eb + 1),
        in_specs=[pl.BlockSpec((tm, d), lambda i, e: (i, 0)),
                  pl.BlockSpec((EXP_BLK, d), lambda i, e: (jnp.minimum(e, n_eb - 1), 0)),
                  pl.BlockSpec((d, EXP_BLK), lambda i, e: (0, jnp.maximum(e - 1, 0))),
                  sel, sel, sel, sel,
                  pl.BlockSpec((tm, d), lambda i, e: (i, 0)),
                  g2(0), g2(1),
                  pl.BlockSpec((1, d), lambda i, e: (0, 0))],
        out_specs=pl.BlockSpec((tm, d), lambda i, e: (i, 0)),
        out_shape=jax.ShapeDtypeStruct((t, d), F32),
        scratch_shapes=[pltpu.VMEM((d, tm), F32), pltpu.VMEM((2, EXP_BLK, tm), BF16)],
        compiler_params=_cparams(("arbitrary", "arbitrary")),
        name="peer_dense",
    )(h, u, vt, r2, e2, n1, e1, x, tab, tab, final_gain.reshape(1, d))


def _peer(h, x, wq, keys, u_tab, v_tab, tab, rowfn, final_gain, final):
    r2, e2, n1, e1 = _peer_select(h, wq.astype(BF16), keys)
    return _peer_dense(h, u_tab.astype(BF16), v_tab.T.astype(BF16), r2, e2, n1, e1, x, tab, rowfn, final_gain, final)


def _lane_vec(vals, rows=SUBLANES):
    out = jnp.zeros((rows, LANES), F32)
    for r, v in enumerate(vals):
        out = out.at[r, :v.shape[0]].set(v.astype(F32))
    return out


def _block_diag_groups(w):
    two, nb, bd, _ = w.shape
    per = LANES // bd
    wg = w.reshape(two, nb // per, per, bd, bd)
    eye = jnp.eye(per, dtype=w.dtype)
    return jnp.einsum("dgpij,pq->dgpiqj", wg, eye).reshape(two, nb // per, LANES, LANES)


def kernel(x, c, ctx, c_ctx, ada_w, ada_b, norm_mix, norm_ffn, final_norm, peer_wq, peer_keys, peer_u, peer_v,
           ev_w_in, ev_w_out, a_conv, a_alog, a_dtb, a_norm, b_conv_w, b_conv_b, b_wa, b_ba, b_wx, b_bx, b_lam,
           od_w_in, od_w_out, c_ibias, c_fbias, c_norm, d_w, d_scale):
    bsz, seq, dm = x.shape
    n_ctx = ctx.shape[1]
    L = n_ctx + seq
    rows = seq // GRID_W
    blk_per = L // TOK_BLK

    tab0 = _mod_table(c, c_ctx, ada_w[0], ada_b[0])
    row0 = lambda i: ((i // blk_per) * 2 + jnp.minimum(i % blk_per, 1)) * 6
    xcat = jnp.concatenate([ctx, x], axis=1).reshape(bsz * L, dm)

    a_heads = a_alog.shape[-1]
    a_width = a_heads * LANES
    a_qkv = 3 * a_width
    b_width = b_lam.shape[-1] * b_lam.shape[-2]
    w_in = ev_w_in[0]
    a_cols = a_qkv + a_width + 4 * a_heads
    w_ab = jnp.pad(w_in[:, a_qkv + a_width:a_cols], ((0, 0), (0, LANES - 4 * a_heads)))
    w0 = jnp.concatenate([w_in[:, :a_qkv], w_in[:, a_qkv:a_qkv + a_width], w_in[:, a_cols:a_cols + b_width],
                          w_in[:, a_cols + b_width:], w_ab], axis=1).astype(BF16)
    qkv, z, xb, gate, ab = _norm_mod_proj(xcat, norm_mix[0], tab0, row0, w0, (a_qkv, a_width, b_width, b_width, LANES))

    gparams = _lane_vec([a_alog[0].reshape(-1), a_dtb[0].reshape(-1)])
    ya = _gdn(qkv.reshape(bsz, L, a_qkv), z.reshape(bsz, L, a_width), ab.reshape(bsz, L, LANES),
              a_conv[0], gparams, a_norm[0], n_ctx, seq, a_heads)

    pvec = jnp.zeros((2, SUBLANES, b_width), F32)
    pvec = pvec.at[:, 0].set(b_ba[0].reshape(2, b_width)).at[:, 1].set(b_bx[0].reshape(2, b_width))
    pvec = pvec.at[:, 2].set(b_lam[0].reshape(2, b_width))
    yb = _lru(xb.reshape(bsz, L, b_width), gate.reshape(bsz, L, b_width), b_conv_w[0], b_conv_b[0],
              _block_diag_groups(b_wa[0]).astype(BF16), _block_diag_groups(b_wx[0]).astype(BF16), pvec, n_ctx, seq)

    x1, h1 = _outproj(xcat, ya.reshape(bsz * L, a_width), yb.reshape(bsz * L, b_width), ev_w_out[0], norm_ffn[0],
                      tab0, row0)
    x2 = _peer(h1, x1, peer_wq[0], peer_keys[0], peer_u[0], peer_v[0], tab0, row0, final_norm, False)
    x2 = x2.reshape(bsz, L, dm)

    tab1 = _mod_table(c, c_ctx, ada_w[1], ada_b[1])
    xc = x2[:, :n_ctx].reshape(bsz * n_ctx, dm)
    xl = x2[:, n_ctx:].reshape(bsz, rows, GRID_W, dm).transpose(0, 2, 1, 3).reshape(bsz * seq, dm)
    lat_per = seq // TOK_BLK
    ctx_per = n_ctx // TOK_BLK
    row_lat = lambda i: ((i // lat_per) * 2 + 1) * 6
    row_ctx = lambda i: ((i // ctx_per) * 2) * 6

    c_heads = c_ibias.shape[-1]
    c_width = c_norm.shape[-1]
    d_width = d_scale.shape[-1]
    w_in1 = od_w_in[0]
    n_state = w_in1.shape[1] - c_width - d_width
    c_qk = (n_state - c_width - 4 * c_heads) // 2
    dqk = c_qk // c_heads

    def pad_heads(wcols):
        return jnp.pad(wcols.reshape(dm, c_heads, dqk), ((0, 0), (0, 0), (0, LANES - dqk))).reshape(dm, c_heads * LANES)

    w_gate = jnp.pad(w_in1[:, 2 * c_qk + c_width:n_state], ((0, 0), (0, LANES - 4 * c_heads)))
    w1 = jnp.concatenate([pad_heads(w_in1[:, :c_qk]), pad_heads(w_in1[:, c_qk:2 * c_qk]),
                          w_in1[:, 2 * c_qk:2 * c_qk + c_width], w_in1[:, n_state:n_state + c_width],
                          w_in1[:, n_state + c_width:], w_gate], axis=1).astype(BF16)
    hw = c_heads * LANES
    widths1 = (hw, hw, c_width, c_width, d_width, LANES)
    ql, kl, vl, og, dl, gl = _norm_mod_proj(xl, norm_mix[1], tab1, row_lat, w1, widths1)
    qc, kc, vc, _, _, gc = _norm_mod_proj(xc, norm_mix[1], tab1, row_ctx, w1, widths1)

    gparams1 = _lane_vec([jnp.concatenate([c_ibias[0].reshape(-1), jnp.zeros((2 * c_heads,), F32)]),
                          jnp.concatenate([jnp.zeros((2 * c_heads,), F32), c_fbias[0].reshape(-1)])])
    r3 = lambda a, n: a.reshape(bsz, n, a.shape[-1])
    yc1 = _mlstm(r3(qc, n_ctx), r3(kc, n_ctx), r3(vc, n_ctx), r3(gc, n_ctx),
                 r3(ql, seq), r3(kl, seq), r3(vl, seq), r3(gl, seq), r3(og, seq), gparams1, c_norm[0], c_heads, dqk)
    yd1 = _pool(dl, d_w[0], d_scale[0], rows)

    x3, h3 = _outproj(xl, yc1.reshape(bsz * seq, c_width), yd1, od_w_out[0], norm_ffn[1], tab1, row_lat)
    out = _peer(h3, x3, peer_wq[1], peer_keys[1], peer_u[1], peer_v[1], tab1, row_lat, final_norm, True)
    return out.reshape(bsz, GRID_W, rows, dm).transpose(0, 2, 1, 3).reshape(bsz, seq, dm)
```

```python
import functools
import math

import jax
import jax.numpy as jnp
import numpy as np
from jax import lax
from jax.experimental import pallas as pl
from jax.experimental.pallas import tpu as pltpu

F32 = jnp.float32
BF16 = jnp.bfloat16
HI = lax.Precision.HIGHEST

EPS = 1e-6
GRID_W = 64
CHUNK = 64
LANES = 128
SUBLANES = 8
HALO = 8
CONV_W = 4
LRU_C = 8.0
PEER_TOPK = 16
N_KEYS = 128
POOL_SIZES = (2, 4, 8, 16)
TOK_BLK = 256
PEER_BLK = 2 * TOK_BLK
ROW_BLK = 256
EXP_BLK = 1024
NEG_BIG = -1e30
VMEM_LIMIT = 48 * 1024 * 1024


def _cparams(sem):
    return pltpu.CompilerParams(dimension_semantics=sem, vmem_limit_bytes=VMEM_LIMIT)


def _nt(a, b, precision=None):
    return lax.dot_general(a, b, (((1,), (1,)), ((), ())), precision=precision, preferred_element_type=F32)


def _tn(a, b, precision=None):
    return lax.dot_general(a, b, (((0,), (0,)), ((), ())), precision=precision, preferred_element_type=F32)


def _mm(a, b, precision=None):
    return jnp.dot(a, b, precision=precision, preferred_element_type=F32)


def _silu(x):
    return x * jax.nn.sigmoid(x)


def _softplus(x):
    return jnp.maximum(x, 0.0) + jnp.log1p(jnp.exp(-jnp.abs(x)))


def _gelu_tanh(x):
    return 0.5 * x * (1.0 + jnp.tanh(math.sqrt(2.0 / math.pi) * (x + 0.044715 * (x * x * x))))


def _iota(shape, dim):
    return lax.broadcasted_iota(jnp.int32, shape, dim)


def _ada_kernel(c_ref, w_ref, b_ref, o_ref):
    o_ref[...] = _mm(_silu(c_ref[...]), w_ref[...], HI) + b_ref[...]


def _ada(cc, w, b):
    r, d = cc.shape
    n = w.shape[1]
    tn = 1536
    return pl.pallas_call(
        _ada_kernel,
        grid=(n // tn,),
        in_specs=[pl.BlockSpec((r, d), lambda j: (0, 0)),
                  pl.BlockSpec((d, tn), lambda j: (0, j)),
                  pl.BlockSpec((1, tn), lambda j: (0, j))],
        out_specs=pl.BlockSpec((r, tn), lambda j: (0, j)),
        out_shape=jax.ShapeDtypeStruct((r, n), F32),
        compiler_params=_cparams(("arbitrary",)),
        name="ada_mod",
    )(cc, w, b.reshape(1, n))


def _mod_table(c, c_ctx, w, b):
    bsz, d = c.shape
    rows = ((bsz + 1 + SUBLANES - 1) // SUBLANES) * SUBLANES
    cc = jnp.zeros((rows, d), F32).at[:bsz].set(c).at[bsz].set(c_ctx)
    m = _ada(cc, w, b).reshape(rows, 6, d)
    tab = jnp.stack([jnp.broadcast_to(m[bsz], (bsz, 6, d)), m[:bsz]], axis=1)
    return tab.reshape(bsz * 2 * 6, 1, d)


def _nmm_kernel(x_ref, g_ref, sh_ref, sc_ref, w_ref, *o_refs, widths):
    x = x_ref[...]
    y = x * lax.rsqrt(jnp.mean(x * x, axis=-1, keepdims=True) + EPS) * g_ref[...]
    h = (y * (1.0 + sc_ref[0]) + sh_ref[0]).astype(BF16)
    off = 0
    for o_ref, n in zip(o_refs, widths):
        o_ref[...] = _mm(h, w_ref[:, off:off + n])
        off += n


def _norm_mod_proj(x, gain, tab, rowfn, w, widths):
    t, d = x.shape
    n = w.shape[1]
    tm = TOK_BLK
    return pl.pallas_call(
        functools.partial(_nmm_kernel, widths=widths),
        grid=(t // tm,),
        in_specs=[pl.BlockSpec((tm, d), lambda i: (i, 0)),
                  pl.BlockSpec((1, d), lambda i: (0, 0)),
                  pl.BlockSpec((1, 1, d), lambda i: (rowfn(i), 0, 0)),
                  pl.BlockSpec((1, 1, d), lambda i: (rowfn(i) + 1, 0, 0)),
                  pl.BlockSpec((d, n), lambda i: (0, 0))],
        out_specs=[pl.BlockSpec((tm, wd), lambda i: (i, 0)) for wd in widths],
        out_shape=[jax.ShapeDtypeStruct((t, wd), F32) for wd in widths],
        compiler_params=_cparams(("arbitrary",)),
        name="norm_mod_proj",
    )(x, gain.reshape(1, d), tab, tab, w)


def _conv_chunk(pad_ref, pbase, w, rows=CHUNK):
    blk = pad_ref[pl.ds(pbase - HALO, rows + 2 * HALO), :]
    n = rows + 2 * HALO
    lo, hi = HALO, HALO + rows
    xm1 = pltpu.roll(blk, 1, 0)[lo:hi]
    x0 = blk[lo:hi]
    xp1 = pltpu.roll(blk, n - 1, 0)[lo:hi]
    xp2 = pltpu.roll(blk, n - 2, 0)[lo:hi]
    return w[0:1] * xm1 + w[1:2] * x0 + w[2:3] * xp1 + w[3:4] * xp2


def _fill_padded(pad_ref, src_ref, n_ctx, n_lat):
    width = pad_ref.shape[1]
    z = jnp.zeros((HALO, width), F32)
    pad_ref[0:HALO, :] = z
    pad_ref[HALO + n_ctx:2 * HALO + n_ctx, :] = z
    pad_ref[2 * HALO + n_ctx + n_lat:3 * HALO + n_ctx + n_lat, :] = z
    pad_ref[HALO:HALO + n_ctx, :] = src_ref[0, 0:n_ctx, :]
    pad_ref[2 * HALO + n_ctx:2 * HALO + n_ctx + n_lat, :] = src_ref[0, n_ctx:n_ctx + n_lat, :]


def _padded_base(c, ncc, rows=CHUNK):
    return pl.multiple_of(HALO + c * rows + jnp.where(c >= ncc, HALO, 0), SUBLANES)


def _lane_col(g, lane_idx):
    lane = _iota(g.shape, 1)
    return jnp.sum(jnp.where(lane == lane_idx, g, 0.0), axis=1, keepdims=True)


def _tri(d):
    r = _iota((CHUNK, CHUNK), 0)
    c = _iota((CHUNK, CHUNK), 1)
    incl = (r >= c) if d == 0 else (r <= c)
    strict = (r > c) if d == 0 else (r < c)
    return incl, strict


def _pair_diff(a_rows, b_rows):
    lane = _iota((CHUNK, LANES), 1)
    a = jnp.where(lane == 0, a_rows, jnp.where(lane == 1, 1.0, 0.0))
    b = jnp.where(lane == 0, 1.0, jnp.where(lane == 1, b_rows, 0.0))
    return _nt(a, b, HI)


def _bwd_chunk(s, ncc, nct):
    return jnp.where(s < ncc, ncc - 1 - s, nct + ncc - 1 - s)


def _split3(x):
    x1 = x.astype(BF16)
    r1 = x - x1.astype(F32)
    x2 = r1.astype(BF16)
    x3 = (r1 - x2.astype(F32)).astype(BF16)
    return x1, x2, x3


def _mask_mm(mask, x):
    x1, x2, x3 = _split3(x)
    return _mm(mask, x1) + (_mm(mask, x2) + _mm(mask, x3))


def _mask_tn(x, mask):
    x1, x2, x3 = _split3(x)
    return _tn(x1, mask) + (_tn(x2, mask) + _tn(x3, mask))


def _gate_kernel(ab_ref, gp_ref, gc_ref, gr_ref, *, n_chunks, n_heads, kind):
    r = _iota((CHUNK, CHUNK), 0)
    c = _iota((CHUNK, CHUNK), 1)
    lower = (r >= c).astype(BF16)
    upper = (r <= c).astype(BF16)
    eye = (r == c).astype(BF16)
    lane = _iota((CHUNK, LANES), 1)
    rowg = _iota((LANES, CHUNK), 0)
    if kind == "gdn":
        fwd_lo, rev_lo, rev_hi = 0, n_heads, 2 * n_heads
    else:
        fwd_lo, rev_lo, rev_hi = 2 * n_heads, 3 * n_heads, 4 * n_heads
    p0 = gp_ref[0:1, :]
    p1 = gp_ref[1:2, :]

    def body(ci, carry):
        r0 = pl.multiple_of(ci * CHUNK, CHUNK)
        raw = ab_ref[0, pl.ds(r0, CHUNK), :]
        if kind == "gdn":
            g = jnp.where(lane < 2 * n_heads, -jnp.exp(p0) * _softplus(raw + p1), jax.nn.sigmoid(raw))
        else:
            g = jnp.where(lane < 2 * n_heads, raw + p0, -_softplus(-(raw + p1)))
        is_fwd = (lane >= fwd_lo) & (lane < rev_lo)
        is_rev = (lane >= rev_lo) & (lane < rev_hi)
        gc_ref[0, pl.ds(r0, CHUNK), :] = jnp.where(is_fwd, _mask_mm(lower, g), jnp.where(is_rev, _mask_mm(upper, g), g))
        row_fwd = (rowg >= fwd_lo) & (rowg < rev_lo)
        row_rev = (rowg >= rev_lo) & (rowg < rev_hi)
        rows = jnp.where(row_fwd, _mask_tn(g, upper), jnp.where(row_rev, _mask_tn(g, lower), _mask_tn(g, eye)))
        gr_ref[0, ci] = rows[0:2 * SUBLANES, :]
        return carry

    lax.fori_loop(0, n_chunks, body, 0)


def _gates(ab, gparams, n_heads, kind):
    bsz, L, _ = ab.shape
    nct = L // CHUNK
    assert 4 * n_heads <= 2 * SUBLANES
    return pl.pallas_call(
        functools.partial(_gate_kernel, n_chunks=nct, n_heads=n_heads, kind=kind),
        grid=(bsz,),
        in_specs=[pl.BlockSpec((1, L, LANES), lambda b: (b, 0, 0)),
                  pl.BlockSpec((SUBLANES, LANES), lambda b: (0, 0))],
        out_specs=[pl.BlockSpec((1, L, LANES), lambda b: (b, 0, 0)),
                   pl.BlockSpec((1, nct, 2 * SUBLANES, CHUNK), lambda b: (b, 0, 0, 0))],
        out_shape=[jax.ShapeDtypeStruct((bsz, L, LANES), F32),
                   jax.ShapeDtypeStruct((bsz, nct, 2 * SUBLANES, CHUNK), F32)],
        compiler_params=_cparams(("arbitrary",)),
        name=kind + "_gates",
    )(ab, gparams)


def _mm3(a, b):
    ah = a.astype(BF16)
    al = (a - ah.astype(F32)).astype(BF16)
    bh = b.astype(BF16)
    bl = (b - bh.astype(F32)).astype(BF16)
    return _mm(ah, bh) + (_mm(ah, bl) + _mm(al, bh))


def _gdn_kernel(q_ref, k_ref, v_ref, z_ref, gc_ref, gr_ref, cwq_ref, cwk_ref, cwv_ref, nw_ref, o_ref,
                pad_ref, qs_ref, ks_ref, vs_ref, acc_ref, u_ref, w_ref, qd_ref, kd_ref, qk_ref, gl_ref,
                *, n_ctx, n_lat, n_heads):
    h = pl.program_id(1)
    ncc = n_ctx // CHUNK
    nct = (n_ctx + n_lat) // CHUNK
    dk = qs_ref.shape[1]

    def prep(src_ref, cw_ref, dst_ref, mode):
        _fill_padded(pad_ref, src_ref, n_ctx, n_lat)
        w = cw_ref[...]

        def body(c, carry):
            y = _silu(_conv_chunk(pad_ref, _padded_base(c, n_ctx // ROW_BLK, ROW_BLK), w, ROW_BLK))
            if mode != "v":
                y = y * lax.rsqrt(jnp.sum(y * y, axis=-1, keepdims=True) + EPS)
            if mode == "q":
                y = y * (dk ** -0.5)
            dst_ref[pl.ds(pl.multiple_of(c * ROW_BLK, ROW_BLK), ROW_BLK), :] = y
            return carry

        lax.fori_loop(0, (n_ctx + n_lat) // ROW_BLK, body, 0)

    prep(q_ref, cwq_ref, qs_ref, "q")
    prep(k_ref, cwk_ref, ks_ref, "k")
    prep(v_ref, cwv_ref, vs_ref, "v")

    n_chain = 4
    big = n_chain * CHUNK
    rr = _iota((big, big), 0)
    cc = _iota((big, big), 1)
    log_chunk = CHUNK.bit_length() - 1
    same_blk = lax.shift_right_logical(rr, log_chunk) == lax.shift_right_logical(cc, log_chunk)
    bwd_blk = (lax.shift_right_logical(rr, log_chunk) & 1) == 1
    fwd_blk = jnp.logical_not(bwd_blk)
    strict_bd = same_blk & ((bwd_blk & (rr < cc)) | (fwd_blk & (rr > cc)))
    incl_bd = same_blk & ((bwd_blk & (rr <= cc)) | (fwd_blk & (rr >= cc)))

    def group_setup(c0):
        ks, rhs, gcols, bcols, grows = [], [], [], [], []
        chains = []
        for j in range(2):
            c = c0 + j
            r0 = pl.multiple_of(c * CHUNK, CHUNK)
            acc_ref[pl.ds(r0, CHUNK), :] = jnp.zeros((CHUNK, dk), F32)
            q = qs_ref[pl.ds(r0, CHUNK), :]
            k = ks_ref[pl.ds(r0, CHUNK), :]
            v = vs_ref[pl.ds(r0, CHUNK), :]
            gates = gc_ref[0, pl.ds(r0, CHUNK), :]
            grt = gr_ref[0, c]
            for d in range(2):
                gcol = _lane_col(gates, d * n_heads + h)
                bcol = _lane_col(gates, 2 * n_heads + d * n_heads + h)
                grow = jnp.sum(jnp.where(_iota(grt.shape, 0) == d * n_heads + h, grt, 0.0), axis=0, keepdims=True)
                gtot = gcol[CHUNK - 1:CHUNK, :] if d == 0 else gcol[0:1, :]
                eg = jnp.exp(gcol)
                ks.append(k)
                gcols.append(gcol)
                bcols.append(bcol)
                grows.append(grow)
                rhs.append(jnp.concatenate([bcol * v, (bcol * eg) * k], axis=1))
                chains.append((c, d, r0, q, k, gcol, grow, gtot, eg))
        kst = jnp.concatenate(ks, axis=0).astype(BF16)
        gam = jnp.where(incl_bd, jnp.exp(jnp.minimum(jnp.concatenate(gcols, axis=0) - jnp.concatenate(grows, axis=1),
                                                      0.0)), 0.0)
        m_bd = jnp.where(strict_bd, jnp.concatenate(bcols, axis=0) * _nt(kst, kst) * gam, 0.0)
        return chains, m_bd, jnp.concatenate(rhs, axis=0)

    def group_store(chains, sol):
        for b, (c, d, r0, q, k, gcol, grow, gtot, eg) in enumerate(chains):
            incl, _ = _tri(d)
            gamma = jnp.where(incl, jnp.exp(jnp.minimum(gcol - grow, 0.0)), 0.0)
            u_ref[d, pl.ds(r0, CHUNK), :] = sol[b * CHUNK:(b + 1) * CHUNK, :dk]
            w_ref[d, pl.ds(r0, CHUNK), :] = sol[b * CHUNK:(b + 1) * CHUNK, dk:].astype(BF16)
            qd_ref[d, pl.ds(r0, CHUNK), :] = (q * eg).astype(BF16)
            kd_ref[d, pl.ds(r0, CHUNK), :] = (k * jnp.exp(gtot - gcol)).astype(BF16)
            qk_ref[d, c] = (_nt(q.astype(BF16), k.astype(BF16)) * gamma).astype(BF16)
            gl_ref[d, c] = jnp.broadcast_to(jnp.exp(gtot), (SUBLANES, LANES))

    def prepare_body(c4, carry):
        groups = [group_setup(4 * c4), group_setup(4 * c4 + 2)]
        idx = range(len(groups))
        ms = [g[1] for g in groups]
        sols = [g[2] for g in groups]
        sols = [sols[i] - _mm3(ms[i], sols[i]) for i in idx]
        ps = [m.astype(BF16) for m in ms]
        for it in range(5):
            ps = [_mm(p, p) for p in ps]
            ps = [p.astype(BF16) for p in ps]
            sols = [sols[i] + _mm(ps[i], sols[i].astype(BF16)) for i in idx]
        for i in idx:
            group_store(groups[i][0], sols[i])
        return carry

    lax.fori_loop(0, nct // 4, prepare_body, 0)

    def step(s, carry):
        cs = (s, _bwd_chunk(s, ncc, nct))
        r0 = [pl.multiple_of(c * CHUNK, CHUNK) for c in cs]
        sb = [st.astype(BF16) for st in carry]
        ws = [_mm(w_ref[d, pl.ds(r0[d], CHUNK), :], sb[d]) for d in range(2)]
        qs = [_mm(qd_ref[d, pl.ds(r0[d], CHUNK), :], sb[d]) for d in range(2)]
        vb = [(u_ref[d, pl.ds(r0[d], CHUNK), :] - ws[d]).astype(BF16) for d in range(2)]
        kv = [_tn(kd_ref[d, pl.ds(r0[d], CHUNK), :], vb[d]) for d in range(2)]
        ov = [_mm(qk_ref[d, cs[d]], vb[d]) for d in range(2)]
        for d in range(2):
            acc_ref[pl.ds(r0[d], CHUNK), :] += qs[d] + ov[d]
        return tuple(carry[d] * gl_ref[d, cs[d]][0:1, :] + kv[d] for d in range(2))

    zero = jnp.zeros((dk, dk), F32)
    lax.fori_loop(0, nct, step, (zero, zero))

    nw = nw_ref[...]

    def out_body(c, carry):
        r0 = pl.multiple_of(c * ROW_BLK, ROW_BLK)
        o = acc_ref[pl.ds(r0, ROW_BLK), :]
        y = o * lax.rsqrt(jnp.mean(o * o, axis=-1, keepdims=True) + EPS) * nw
        o_ref[0, pl.ds(r0, ROW_BLK), :] = y * _silu(z_ref[0, pl.ds(r0, ROW_BLK), :])
        return carry

    lax.fori_loop(0, (n_ctx + n_lat) // ROW_BLK, out_body, 0)


def _gdn(qkv, z, ab, conv_w, gparams, norm_w, n_ctx, n_lat, n_heads):
    bsz, L, _ = qkv.shape
    dk = LANES
    seq = lambda off: pl.BlockSpec((1, L, dk), lambda b, h: (b, 0, off + h))
    cw = lambda off: pl.BlockSpec((CONV_W, dk), lambda b, h: (0, off + h))
    lp = L + 3 * HALO
    nct = L // CHUNK
    assert nct % 4 == 0 and n_ctx % ROW_BLK == 0 and n_lat % ROW_BLK == 0
    gc, gr = _gates(ab, gparams, n_heads, "gdn")
    return pl.pallas_call(
        functools.partial(_gdn_kernel, n_ctx=n_ctx, n_lat=n_lat, n_heads=n_heads),
        grid=(bsz, n_heads),
        in_specs=[seq(0), seq(n_heads), seq(2 * n_heads),
                  pl.BlockSpec((1, L, dk), lambda b, h: (b, 0, h)),
                  pl.BlockSpec((1, L, LANES), lambda b, h: (b, 0, 0)),
                  pl.BlockSpec((1, nct, 2 * SUBLANES, CHUNK), lambda b, h: (b, 0, 0, 0)),
                  cw(0), cw(n_heads), cw(2 * n_heads),
                  pl.BlockSpec((1, dk), lambda b, h: (0, 0))],
        out_specs=pl.BlockSpec((1, L, dk), lambda b, h: (b, 0, h)),
        out_shape=jax.ShapeDtypeStruct((bsz, L, n_heads * dk), F32),
        scratch_shapes=[pltpu.VMEM((lp, dk), F32)] + [pltpu.VMEM((L, dk), F32) for _ in range(4)]
        + [pltpu.VMEM((2, L, dk), F32)] + [pltpu.VMEM((2, L, dk), BF16) for _ in range(3)]
        + [pltpu.VMEM((2, nct, CHUNK, CHUNK), BF16), pltpu.VMEM((2, nct, SUBLANES, LANES), F32)],
        compiler_params=_cparams(("arbitrary", "arbitrary")),
        name="gdn_mixer",
    )(qkv, qkv, qkv, z, gc, gr, conv_w, conv_w, conv_w, norm_w.reshape(1, dk))


def _lru_kernel(x_ref, gt_ref, cw_ref, cb_ref, wa_ref, wx_ref, pv_ref, o_ref,
                pad_ref, xc_ref, a_ref, b_ref, acc_ref, *, n_ctx, n_lat):
    ncc = n_ctx // CHUNK
    nct = (n_ctx + n_lat) // CHUNK
    L = n_ctx + n_lat
    width = xc_ref.shape[1]
    _fill_padded(pad_ref, x_ref, n_ctx, n_lat)
    w = cw_ref[...]
    cb = cb_ref[...]

    def conv_body(c, carry):
        r0 = pl.multiple_of(c * CHUNK, CHUNK)
        xc_ref[pl.ds(r0, CHUNK), :] = _conv_chunk(pad_ref, _padded_base(c, ncc), w) + cb
        acc_ref[pl.ds(r0, CHUNK), :] = jnp.zeros((CHUNK, width), F32)
        return carry

    lax.fori_loop(0, nct, conv_body, 0)

    row = _iota((SUBLANES, width), 0)
    bias_a = [pv_ref[d, 0:1, :] for d in range(2)]
    bias_x = [pv_ref[d, 1:2, :] for d in range(2)]
    decay = [-LRU_C * _softplus(-pv_ref[d, 2:3, :]) for d in range(2)]

    def coef_body(c, carry):
        r0 = pl.multiple_of(c * CHUNK, CHUNK)
        x = xc_ref[pl.ds(r0, CHUNK), :]
        xb = x.astype(BF16)
        for d in range(2):
            r = jax.nn.sigmoid(_mm(xb, wa_ref[d, 0]) + bias_a[d])
            i = jax.nn.sigmoid(_mm(xb, wx_ref[d, 0]) + bias_x[d])
            log_a = decay[d] * r
            a_ref[d, pl.ds(r0, CHUNK), :] = jnp.exp(log_a)
            th = jnp.tanh(log_a)
            b_ref[d, pl.ds(r0, CHUNK), :] = jnp.sqrt(-2.0 * th / (1.0 - th)) * (i * x)
        return carry

    lax.fori_loop(0, nct, coef_body, 0)

    nt_ctx = n_ctx // SUBLANES
    nt_all = L // SUBLANES

    def scan_tile(t, d, hc):
        r0 = pl.multiple_of(t * SUBLANES, SUBLANES)
        a = a_ref[d, pl.ds(r0, SUBLANES), :]
        b = b_ref[d, pl.ds(r0, SUBLANES), :]
        for sh in (1, 2, 4):
            if d == 0:
                a_s = pltpu.roll(a, sh, 0)
                b_s = pltpu.roll(b, sh, 0)
                m = row >= sh
            else:
                a_s = pltpu.roll(a, SUBLANES - sh, 0)
                b_s = pltpu.roll(b, SUBLANES - sh, 0)
                m = row < SUBLANES - sh
            b = jnp.where(m, a * b_s + b, b)
            a = jnp.where(m, a * a_s, a)
        hcur = b + a * hc
        acc_ref[pl.ds(r0, SUBLANES), :] += hcur
        last = hcur[SUBLANES - 1:SUBLANES, :] if d == 0 else hcur[0:1, :]
        return jnp.broadcast_to(last, (SUBLANES, width))

    def scan_body(s, carry):
        hf, hb = carry
        tb = jnp.where(s < nt_ctx, nt_ctx - 1 - s, nt_all + nt_ctx - 1 - s)
        return scan_tile(s, 0, hf), scan_tile(tb, 1, hb)

    zero = jnp.zeros((SUBLANES, width), F32)
    lax.fori_loop(0, nt_all, scan_body, (zero, zero))


    def out_body(c, carry):
        r0 = pl.multiple_of(c * CHUNK, CHUNK)
        o_ref[0, pl.ds(r0, CHUNK), :] = acc_ref[pl.ds(r0, CHUNK), :] * _gelu_tanh(gt_ref[0, pl.ds(r0, CHUNK), :])
        return carry

    lax.fori_loop(0, nct, out_body, 0)


def _lru(xb, gate, conv_w, conv_b, wa, wx, pvec, n_ctx, n_lat):
    bsz, L, width = xb.shape
    ng = width // LANES
    lp = L + 3 * HALO
    seq = pl.BlockSpec((1, L, LANES), lambda b, j: (b, 0, j))
    return pl.pallas_call(
        functools.partial(_lru_kernel, n_ctx=n_ctx, n_lat=n_lat),
        grid=(bsz, ng),
        in_specs=[seq, seq,
                  pl.BlockSpec((CONV_W, LANES), lambda b, j: (0, j)),
                  pl.BlockSpec((1, LANES), lambda b, j: (0, j)),
                  pl.BlockSpec((2, 1, LANES, LANES), lambda b, j: (0, j, 0, 0)),
                  pl.BlockSpec((2, 1, LANES, LANES), lambda b, j: (0, j, 0, 0)),
                  pl.BlockSpec((2, SUBLANES, LANES), lambda b, j: (0, 0, j))],
        out_specs=seq,
        out_shape=jax.ShapeDtypeStruct((bsz, L, width), F32),
        scratch_shapes=[pltpu.VMEM((lp, LANES), F32), pltpu.VMEM((L, LANES), F32), pltpu.VMEM((2, L, LANES), F32),
                        pltpu.VMEM((2, L, LANES), F32), pltpu.VMEM((L, LANES), F32)],
        compiler_params=_cparams(("arbitrary", "arbitrary")),
        name="lru_mixer",
    )(xb, gate, conv_w, conv_b.reshape(1, width), wa, wx, pvec)


def _mlstm_kernel(qc_ref, kc_ref, vc_ref, gcc_ref, grc_ref, ql_ref, kl_ref, vl_ref, gcl_ref, grl_ref, og_ref, nw_ref,
                  o_ref, acc_ref, cc_ref, ms_ref, p2_ref, mi_ref, bc_ref, *, n_ctx, n_lat, n_heads, dqk):
    h = pl.program_id(1)
    ncc = n_ctx // CHUNK
    ncl = n_lat // CHUNK
    dv = LANES
    lane = _iota((CHUNK, LANES), 1)
    ones_blk = jnp.ones((CHUNK, LANES), F32)

    def prepare(refs, slot0, chains, with_out):
        q_ref, k_ref, v_ref, gcol_ref, grow_ref = refs
        idx = range(len(chains))
        r0 = [pl.multiple_of(c * CHUNK, CHUNK) for c, _ in chains]
        k = [k_ref[0, pl.ds(r0[i], CHUNK), :] * (dqk ** -0.5) for i in idx]
        v_ext = [jnp.concatenate([v_ref[0, pl.ds(r0[i], CHUNK), :], ones_blk], axis=1).astype(BF16) for i in idx]
        gates = [gcol_ref[0, pl.ds(r0[i], CHUNK), :] for i in idx]
        li = [_lane_col(gates[i], chains[i][1] * n_heads + h) for i in idx]
        bcum = [_lane_col(gates[i], 2 * n_heads + chains[i][1] * n_heads + h) for i in idx]
        btot = [bcum[i][CHUNK - 1:CHUNK, :] if chains[i][1] == 0 else bcum[i][0:1, :] for i in idx]
        w_end = [btot[i] - bcum[i] + li[i] for i in idx]
        m_chunk = [jnp.max(w_end[i], axis=0, keepdims=True) for i in idx]
        c_chunk = [_tn((k[i] * jnp.exp(w_end[i] - m_chunk[i])).astype(BF16), v_ext[i]) for i in idx]
        if with_out:
            qk = [_nt(q_ref[0, pl.ds(r0[i], CHUNK), :].astype(BF16), k[i].astype(BF16)) for i in idx]
            s_qk, m_intra = [], []
            for i in idx:
                c, d = chains[i]
                grt = grow_ref[0, c]
                rowi = _iota(grt.shape, 0)
                li_row = jnp.sum(jnp.where(rowi == d * n_heads + h, grt, 0.0), axis=0, keepdims=True)
                b_row = jnp.sum(jnp.where(rowi == 2 * n_heads + d * n_heads + h, grt, 0.0), axis=0, keepdims=True)
                incl, _ = _tri(d)
                log_d = jnp.where(incl, bcum[i] + (li_row - b_row), NEG_BIG)
                m_intra.append(jnp.max(log_d, axis=1, keepdims=True))
                s_qk.append((qk[i] * jnp.exp(log_d - m_intra[i])).astype(BF16))
            p2 = [_mm(s_qk[i], v_ext[i]) for i in idx]
        row8 = _iota((SUBLANES, LANES), 0)
        for i in idx:
            c, d = chains[i]
            cc_ref[d, slot0 + c] = c_chunk[i]
            ms_ref[d, slot0 + c] = jnp.where(row8 == 0, btot[i], m_chunk[i])
            if with_out:
                p2_ref[d, pl.ds(r0[i], CHUNK), :] = p2[i]
                mi_ref[d, pl.ds(r0[i], CHUNK), :] = jnp.broadcast_to(m_intra[i], (CHUNK, LANES))
                bc_ref[d, pl.ds(r0[i], CHUNK), :] = jnp.broadcast_to(bcum[i], (CHUNK, LANES))

    ctx_refs = (qc_ref, kc_ref, vc_ref, gcc_ref, grc_ref)
    lat_refs = (ql_ref, kl_ref, vl_ref, gcl_ref, grl_ref)

    def ctx_prep(c2, carry):
        prepare(ctx_refs, 0, [(2 * c2 + j, d) for j in range(2) for d in range(2)], False)
        return carry

    lax.fori_loop(0, ncc // 2, ctx_prep, 0)

    def lat_prep(c2, carry):
        for j in range(2):
            acc_ref[pl.ds(pl.multiple_of((2 * c2 + j) * CHUNK, CHUNK), CHUNK), :] = jnp.zeros((CHUNK, dv), F32)
        prepare(lat_refs, ncc, [(2 * c2 + j, d) for j in range(2) for d in range(2)], True)
        return carry

    lax.fori_loop(0, ncl // 2, lat_prep, 0)

    def advance(states, slots):
        out = []
        for d in range(2):
            cx, m_s = states[d]
            ms = ms_ref[d, slots[d]]
            btot = ms[0:1, :]
            m_chunk = ms[1:2, :]
            m_new = jnp.maximum(btot + m_s, m_chunk)
            f_old = jnp.exp(btot + m_s - m_new)
            f_new = jnp.exp(m_chunk - m_new)
            wide = lambda f: jnp.concatenate([f, f], axis=1)
            out.append((wide(f_old) * cx + wide(f_new) * cc_ref[d, slots[d]], m_new))
        return tuple(out)

    def ctx_step(s, carry):
        return advance(carry, (s, ncc - 1 - s))

    zero = (jnp.zeros((LANES, 2 * dv), F32), jnp.zeros((1, LANES), F32))
    carry = lax.fori_loop(0, ncc, ctx_step, (zero, zero))

    def lat_step(s, carry):
        cs = (s, ncl - 1 - s)
        r0 = [pl.multiple_of(c * CHUNK, CHUNK) for c in cs]
        p1 = [_mm(ql_ref[0, pl.ds(r0[d], CHUNK), :].astype(BF16), carry[d][0].astype(BF16)) for d in range(2)]
        for d in range(2):
            m_intra = mi_ref[d, pl.ds(r0[d], CHUNK), :]
            m_inter = bc_ref[d, pl.ds(r0[d], CHUNK), :] + carry[d][1]
            m_tot = jnp.maximum(m_inter, m_intra)
            w_inter = jnp.exp(m_inter - m_tot)
            w_intra = jnp.exp(m_intra - m_tot)
            p2 = p2_ref[d, pl.ds(r0[d], CHUNK), :]
            num = w_inter * p1[d][:, :dv] + w_intra * p2[:, :dv]
            den = w_inter * p1[d][:, dv:] + w_intra * p2[:, dv:]
            acc_ref[pl.ds(r0[d], CHUNK), :] += num / jnp.maximum(jnp.abs(den), jnp.exp(-m_tot))
        return advance(carry, (ncc + cs[0], ncc + cs[1]))

    lax.fori_loop(0, ncl, lat_step, carry)

    nw = nw_ref[...]

    def out_body(c, carry):
        r0 = pl.multiple_of(c * ROW_BLK, ROW_BLK)
        hs = acc_ref[pl.ds(r0, ROW_BLK), :]
        y = hs * lax.rsqrt(jnp.mean(hs * hs, axis=-1, keepdims=True) + EPS) * nw
        o_ref[0, pl.ds(r0, ROW_BLK), :] = y * jax.nn.sigmoid(og_ref[0, pl.ds(r0, ROW_BLK), :])
        return carry

    lax.fori_loop(0, n_lat // ROW_BLK, out_body, 0)


def _mlstm(qc, kc, vc, gc, ql, kl, vl, gl, og, gparams, norm_w, n_heads, dqk):
    bsz, n_ctx, _ = qc.shape
    n_lat = ql.shape[1]
    cs = pl.BlockSpec((1, n_ctx, LANES), lambda b, h: (b, 0, h))
    ls = pl.BlockSpec((1, n_lat, LANES), lambda b, h: (b, 0, h))
    assert n_ctx % (2 * CHUNK) == 0 and n_lat % (2 * CHUNK) == 0 and n_lat % ROW_BLK == 0
    gcc, grc = _gates(gc, gparams, n_heads, "mlstm")
    gcl, grl = _gates(gl, gparams, n_heads, "mlstm")
    row_spec = lambda n: pl.BlockSpec((1, n // CHUNK, 2 * SUBLANES, CHUNK), lambda b, h: (b, 0, 0, 0))
    return pl.pallas_call(
        functools.partial(_mlstm_kernel, n_ctx=n_ctx, n_lat=n_lat, n_heads=n_heads, dqk=dqk),
        grid=(bsz, n_heads),
        in_specs=[cs, cs, cs, pl.BlockSpec((1, n_ctx, LANES), lambda b, h: (b, 0, 0)), row_spec(n_ctx),
                  ls, ls, ls, pl.BlockSpec((1, n_lat, LANES), lambda b, h: (b, 0, 0)), row_spec(n_lat),
                  ls,
                  pl.BlockSpec((1, LANES), lambda b, h: (0, h))],
        out_specs=ls,
        out_shape=jax.ShapeDtypeStruct((bsz, n_lat, n_heads * LANES), F32),
        scratch_shapes=[pltpu.VMEM((n_lat, LANES), F32),
                        pltpu.VMEM((2, (n_ctx + n_lat) // CHUNK, LANES, 2 * LANES), F32),
                        pltpu.VMEM((2, (n_ctx + n_lat) // CHUNK, SUBLANES, LANES), F32),
                        pltpu.VMEM((2, n_lat, 2 * LANES), F32),
                        pltpu.VMEM((2, n_lat, LANES), F32),
                        pltpu.VMEM((2, n_lat, LANES), F32)],
        compiler_params=_cparams(("arbitrary", "arbitrary")),
        name="mlstm_mixer",
    )(qc, kc, vc, gcc, grc, ql, kl, vl, gcl, grl, og, norm_w.reshape(1, -1))


def _pool_kernel(x_ref, w_ref, sc_ref, o_ref, *, seg):
    tm = x_ref.shape[0]
    r = _iota((tm, tm), 0)
    c = _iota((tm, tm), 1)
    sh = seg.bit_length() - 1
    same = lax.shift_right_logical(r, sh) == lax.shift_right_logical(c, sh)
    t = r & (seg - 1)
    s = c & (seg - 1)
    for gi, wsz in enumerate(POOL_SIZES):
        lo = jnp.maximum(t - wsz // 2, 0)
        hi = jnp.minimum(t - wsz // 2 + wsz, seg)
        inwin = same & (s >= lo) & (s < hi)
        pmat = jnp.where(inwin, 1.0 / (hi - lo).astype(F32), 0.0) - jnp.where(r == c, 1.0, 0.0)
        x = x_ref[:, gi * LANES:(gi + 1) * LANES]
        pooled = _mm3(pmat, x)
        y = _mm(pooled.astype(BF16), w_ref[gi].astype(BF16))
        o_ref[:, gi * LANES:(gi + 1) * LANES] = y * sc_ref[:, gi * LANES:(gi + 1) * LANES]


def _pool(x, w_grp, scale, seg):
    t, width = x.shape
    tm = TOK_BLK
    assert seg & (seg - 1) == 0 and tm % seg == 0
    return pl.pallas_call(
        functools.partial(_pool_kernel, seg=seg),
        grid=(t // tm,),
        in_specs=[pl.BlockSpec((tm, width), lambda i: (i, 0)),
                  pl.BlockSpec(w_grp.shape, lambda i: (0, 0, 0)),
                  pl.BlockSpec((1, width), lambda i: (0, 0))],
        out_specs=pl.BlockSpec((tm, width), lambda i: (i, 0)),
        out_shape=jax.ShapeDtypeStruct((t, width), F32),
        compiler_params=_cparams(("arbitrary",)),
        name="pool_mixer",
    )(x, w_grp, scale.reshape(1, width))


def _outproj_kernel(x_ref, ya_ref, yb_ref, wa_ref, wb_ref, g1_ref, gn_ref, sh_ref, sc_ref, xo_ref, h_ref):
    y = _mm(ya_ref[...].astype(BF16), wa_ref[...]) + _mm(yb_ref[...].astype(BF16), wb_ref[...])
    xn = x_ref[...] + g1_ref[0] * y
    xo_ref[...] = xn
    hn = xn * lax.rsqrt(jnp.mean(xn * xn, axis=-1, keepdims=True) + EPS) * gn_ref[...]
    h_ref[...] = (hn * (1.0 + sc_ref[0]) + sh_ref[0]).astype(BF16)


def _outproj(x, ya, yb, w_out, gain, tab, rowfn):
    t, d = x.shape
    wa_n = ya.shape[1]
    tm = TOK_BLK
    w = w_out.astype(BF16)
    mod = lambda k: pl.BlockSpec((1, 1, d), lambda i: (rowfn(i) + k, 0, 0))
    return pl.pallas_call(
        _outproj_kernel,
        grid=(t // tm,),
        in_specs=[pl.BlockSpec((tm, d), lambda i: (i, 0)),
                  pl.BlockSpec((tm, wa_n), lambda i: (i, 0)),
                  pl.BlockSpec((tm, yb.shape[1]), lambda i: (i, 0)),
                  pl.BlockSpec((wa_n, d), lambda i: (0, 0)),
                  pl.BlockSpec((yb.shape[1], d), lambda i: (0, 0)),
                  mod(2),
                  pl.BlockSpec((1, d), lambda i: (0, 0)),
                  mod(3), mod(4)],
        out_specs=[pl.BlockSpec((tm, d), lambda i: (i, 0)), pl.BlockSpec((tm, d), lambda i: (i, 0))],
        out_shape=[jax.ShapeDtypeStruct((t, d), F32), jax.ShapeDtypeStruct((t, d), BF16)],
        compiler_params=_cparams(("arbitrary",)),
        name="out_proj",
    )(x, ya, yb, w[:wa_n], w[wa_n:], tab, gain.reshape(1, d), tab, tab)


def _hyperbola_pairs():
    return [(j1, j2) for j1 in range(PEER_TOPK) for j2 in range(PEER_TOPK) if (j1 + 1) * (j2 + 1) <= PEER_TOPK]


def _topk_rows(s, k, break_ties):
    n, t = s.shape
    rowi = _iota((n, t), 0).astype(F32)
    rank = jnp.full((n, t), float(2 * k), F32)
    cur = s
    vals = []
    for j in range(k):
        m = jnp.max(cur, axis=0, keepdims=True)
        sel = cur == m
        if break_ties:
            sel = rowi == jnp.min(jnp.where(sel, rowi, float(n)), axis=0, keepdims=True)
        rank = jnp.where(sel, float(j), rank)
        cur = jnp.where(sel, -jnp.inf, cur)
        vals.append(m)
    return vals, rank


def _candidate_stage(v1, v2, break_ties):
    pairs = _hyperbola_pairs()
    cand = jnp.concatenate([v1[j1] + v2[j2] for j1, j2 in pairs], axis=0)
    _, crank = _topk_rows(cand, PEER_TOPK, break_ties)
    chosen = crank < PEER_TOPK
    zsum = jnp.sum(jnp.where(chosen, jnp.exp(cand - (v1[0] + v2[0])), 0.0), axis=0, keepdims=True)
    counts = jnp.where(chosen, 1.0, 0.0)
    n_by_rank = []
    row = 0
    for j1 in range(PEER_TOPK):
        width = PEER_TOPK // (j1 + 1)
        n_by_rank.append(jnp.sum(counts[row:row + width], axis=0, keepdims=True))
        row += width
    return n_by_rank, zsum, jnp.sum(counts, axis=0, keepdims=True)


def _route(s1, s2):
    v1, rank1 = _topk_rows(s1, PEER_TOPK, True)
    v2, rank2 = _topk_rows(s2, PEER_TOPK, True)
    n_by_rank, zsum, _ = _candidate_stage(v1, v2, True)
    n1 = jnp.zeros_like(rank1)
    for j1 in range(PEER_TOPK):
        n1 = jnp.where(rank1 == float(j1), n_by_rank[j1], n1)
    e2 = jnp.where(rank2 < PEER_TOPK, jnp.exp(s2 - v2[0]), 0.0)
    e1 = jnp.where(rank1 < PEER_TOPK, jnp.exp(s1 - v1[0]), 0.0) / zsum
    return rank2, e2, n1, e1


def _sorted_top(s):
    n = s.shape[0] // SUBLANES
    a = [s[SUBLANES * j:SUBLANES * (j + 1), :] for j in range(n)]

    def cex(i, l):
        a[i], a[l] = jnp.maximum(a[i], a[l]), jnp.minimum(a[i], a[l])

    k = 2
    while k <= n:
        j = k // 2
        while j >= 1:
            for i in range(n):
                l = i ^ j
                if l > i:
                    if (i & k) == 0:
                        cex(i, l)
                    else:
                        cex(l, i)
            j //= 2
        k *= 2
    for shift in (4, 2, 1):
        a = [jnp.maximum(a[j], pltpu.roll(a[n - 1 - j], shift, 0)) for j in range(n)]
        j = n // 2
        while j >= 1:
            for i in range(n):
                l = i ^ j
                if l > i:
                    cex(i, l)
            j //= 2
    return a


def _by_rank(bits, table):
    level = list(table)
    for g in reversed(bits):
        level = [jnp.where(g, level[2 * i + 1], level[2 * i]) for i in range(len(level) // 2)]
    return level[0]


def _rank_bits(x, v):
    g8 = v[7] > x
    g4 = jnp.where(g8, v[11], v[3]) > x
    g2 = _by_rank([g8, g4], [v[1], v[5], v[9], v[13]]) > x
    g1 = _by_rank([g8, g4, g2], [v[2 * i] for i in range(8)]) > x
    return [g8, g4, g2, g1]


def _route_untied(s1, s2):
    v1 = _sorted_top(s1)
    v2 = _sorted_top(s2)
    n_by_rank, zsum, _ = _candidate_stage([v[0:1, :] for v in v1], [v[0:1, :] for v in v2], True)
    shape = v1[0].shape
    n_tab = [jnp.broadcast_to(nj, shape) for nj in n_by_rank]
    inv_z = jnp.broadcast_to(1.0 / zsum, shape)
    rank2, e2, n1, e1 = [], [], [], []
    cnt1 = jnp.zeros(shape, F32)
    cnt2 = jnp.zeros(shape, F32)
    for r in range(s1.shape[0] // SUBLANES):
        x1 = s1[SUBLANES * r:SUBLANES * (r + 1), :]
        x2 = s2[SUBLANES * r:SUBLANES * (r + 1), :]
        in1 = x1 >= v1[PEER_TOPK - 1]
        in2 = x2 >= v2[PEER_TOPK - 1]
        n1.append(jnp.where(in1, _by_rank(_rank_bits(x1, v1), n_tab), 0.0))
        e1.append(jnp.where(in1, jnp.exp(x1 - v1[0]) * inv_z, 0.0))
        g8, g4, g2, g1 = _rank_bits(x2, v2)
        rk = (jnp.where(g8, 8.0, 0.0) + jnp.where(g4, 4.0, 0.0)) + (jnp.where(g2, 2.0, 0.0) + jnp.where(g1, 1.0, 0.0))
        rank2.append(jnp.where(in2, rk, float(2 * PEER_TOPK)))
        e2.append(jnp.where(in2, jnp.exp(x2 - v2[0]), 0.0))
        cnt1 = cnt1 + jnp.where(in1, 1.0, 0.0)
        cnt2 = cnt2 + jnp.where(in2, 1.0, 0.0)
    tied = jnp.zeros((1, shape[1]), F32)
    for cnt in (cnt1, cnt2):
        tied = tied + jnp.abs(jnp.sum(cnt, axis=0, keepdims=True) - float(PEER_TOPK))
    for v in (v1, v2):
        for j in range(PEER_TOPK - 1):
            tied = tied + jnp.where(v[j][0:1, :] == v[j + 1][0:1, :], 1.0, 0.0)
    cat = lambda parts: jnp.concatenate(parts, axis=0)
    return cat(rank2), cat(e2), cat(n1), cat(e1), tied


def _peer_select_kernel(h_ref, wq_ref, keys_ref, r2_ref, e2_ref, n1_ref, e1_ref):
    q = _mm(h_ref[...], wq_ref[...])
    s1 = _nt(keys_ref[0, 0], q[:, :N_KEYS], HI)
    s2 = _nt(keys_ref[0, 1], q[:, N_KEYS:], HI)

    def emit(rank2, e2, n1, e1):
        r2_ref[0] = rank2.astype(r2_ref.dtype)
        e2_ref[0] = e2.astype(e2_ref.dtype)
        n1_ref[0] = n1
        e1_ref[0] = e1

    rank2, e2, n1, e1, tied = _route_untied(s1, s2)
    emit(rank2, e2, n1, e1)

    @pl.when(jnp.max(tied) > 0.0)
    def _():
        emit(*_route(s1, s2))


def _peer_select(h, wq, keys):
    t, d = h.shape
    nh = keys.shape[0]
    qd = wq.shape[1] // nh
    tm = PEER_BLK
    ospec = pl.BlockSpec((1, N_KEYS, tm), lambda i, hh: (hh, 0, i))
    return pl.pallas_call(
        _peer_select_kernel,
        grid=(t // tm, nh),
        in_specs=[pl.BlockSpec((tm, d), lambda i, hh: (i, 0)),
                  pl.BlockSpec((d, qd), lambda i, hh: (0, hh)),
                  pl.BlockSpec((1, 2, N_KEYS, qd // 2), lambda i, hh: (hh, 0, 0, 0))],
        out_specs=[ospec, ospec, ospec, ospec],
        out_shape=[jax.ShapeDtypeStruct((nh, N_KEYS, t), dt) for dt in (BF16, BF16, F32, F32)],
        compiler_params=_cparams(("arbitrary", "arbitrary")),
        name="peer_select",
    )(h, wq, keys)


def _peer_dense_kernel(h_ref, u_ref, vt_ref, r2_ref, e2_ref, n1_ref, e1_ref, x_ref, g2a_ref, g2b_ref, fn_ref, o_ref,
                       acc_ref, wact_ref, *, n_heads, final):
    eb = pl.program_id(1)
    n_eb = pl.num_programs(1) - 1
    cur = eb & 1
    prev = 1 - cur

    @pl.when(eb == 0)
    def _():
        acc_ref[...] = jnp.zeros_like(acc_ref)
        wact_ref[1] = jnp.zeros(wact_ref.shape[1:], BF16)

    @pl.when(eb < n_eb)
    def _():
        hb = h_ref[...]
        n_part = 4
        part_rows = EXP_BLK // n_part
        acts = [_nt(u_ref[p * part_rows:(p + 1) * part_rows, :], hb) for p in range(n_part)]
        upd = _mm(vt_ref[...], wact_ref[prev])
        rows_per = EXP_BLK // N_KEYS
        i1_base = pl.multiple_of(eb * rows_per, rows_per)
        n1_tiles = [n1_ref[hh, pl.ds(i1_base, rows_per), :] for hh in range(n_heads)]
        e1_tiles = [e1_ref[hh, pl.ds(i1_base, rows_per), :] for hh in range(n_heads)]
        zero = jnp.zeros((), BF16)
        for r in range(rows_per):
            wgt = None
            for hh in range(n_heads):
                shape = r2_ref.shape[1:]
                n1row = jnp.broadcast_to(n1_tiles[hh][r:r + 1, :], shape).astype(BF16)
                e1row = jnp.broadcast_to(e1_tiles[hh][r:r + 1, :], shape).astype(BF16)
                term = jnp.where(r2_ref[hh] < n1row, e2_ref[hh], zero) * e1row
                wgt = term if wgt is None else wgt + term
            off = (r * N_KEYS) % part_rows
            a = acts[(r * N_KEYS) // part_rows][off:off + N_KEYS, :]
            wact_ref[cur, r * N_KEYS:(r + 1) * N_KEYS, :] = _gelu_tanh(a.astype(BF16)) * wgt
        acc_ref[...] += upd

    @pl.when(eb == n_eb)
    def _():
        acc_ref[...] += _mm(vt_ref[...], wact_ref[prev])
        for part, g2_ref in enumerate((g2a_ref, g2b_ref)):
            rows = slice(part * TOK_BLK, (part + 1) * TOK_BLK)
            y = x_ref[rows, :] + g2_ref[0] * acc_ref[:, rows].T
            if final:
                y = y * lax.rsqrt(jnp.mean(y * y, axis=-1, keepdims=True) + EPS) * fn_ref[...]
            o_ref[rows, :] = y


def _peer_dense(h, u, vt, r2, e2, n1, e1, x, tab, rowfn, final_gain, final):
    t, d = h.shape
    n_eb = u.shape[0] // EXP_BLK
    nh = r2.shape[0]
    tm = PEER_BLK
    sel = pl.BlockSpec((nh, N_KEYS, tm), lambda i, e: (0, 0, i))
    g2 = lambda part: pl.BlockSpec((1, 1, d), lambda i, e: (rowfn(2 * i + part) + 5, 0, 0))
    return pl.pallas_call(
        functools.partial(_peer_dense_kernel, n_heads=nh, final=final),
        grid=(t // tm, n_eb + 1),
        in_specs=[pl.BlockSpec((tm, d), lambda i, e: (i, 0)),
                  pl.BlockSpec((EXP_BLK, d), lambda i, e: (jnp.minimum(e, n_eb - 1), 0)),
                  pl.BlockSpec((d, EXP_BLK), lambda i, e: (0, jnp.maximum(e - 1, 0))),
                  sel, sel, sel, sel,
                  pl.BlockSpec((tm, d), lambda i, e: (i, 0)),
                  g2(0), g2(1),
                  pl.BlockSpec((1, d), lambda i, e: (0, 0))],
        out_specs=pl.BlockSpec((tm, d), lambda i, e: (i, 0)),
        out_shape=jax.ShapeDtypeStruct((t, d), F32),
        scratch_shapes=[pltpu.VMEM((d, tm), F32), pltpu.VMEM((2, EXP_BLK, tm), BF16)],
        compiler_params=_cparams(("arbitrary", "arbitrary")),
        name="peer_dense",
    )(h, u, vt, r2, e2, n1, e1, x, tab, tab, final_gain.reshape(1, d))


def _peer(h, x, wq, keys, u_tab, v_tab, tab, rowfn, final_gain, final):
    r2, e2, n1, e1 = _peer_select(h, wq.astype(BF16), keys)
    return _peer_dense(h, u_tab.astype(BF16), v_tab.T.astype(BF16), r2, e2, n1, e1, x, tab, rowfn, final_gain, final)


def _lane_vec(vals, rows=SUBLANES):
    out = jnp.zeros((rows, LANES), F32)
    for r, v in enumerate(vals):
        out = out.at[r, :v.shape[0]].set(v.astype(F32))
    return out


def _block_diag_groups(w):
    two, nb, bd, _ = w.shape
    per = LANES // bd
    wg = w.reshape(two, nb // per, per, bd, bd)
    eye = jnp.eye(per, dtype=w.dtype)
    return jnp.einsum("dgpij,pq->dgpiqj", wg, eye).reshape(two, nb // per, LANES, LANES)


def kernel(x, c, ctx, c_ctx, ada_w, ada_b, norm_mix, norm_ffn, final_norm, peer_wq, peer_keys, peer_u, peer_v,
           ev_w_in, ev_w_out, a_conv, a_alog, a_dtb, a_norm, b_conv_w, b_conv_b, b_wa, b_ba, b_wx, b_bx, b_lam,
           od_w_in, od_w_out, c_ibias, c_fbias, c_norm, d_w, d_scale):
    bsz, seq, dm = x.shape
    n_ctx = ctx.shape[1]
    L = n_ctx + seq
    rows = seq // GRID_W
    blk_per = L // TOK_BLK

    tab0 = _mod_table(c, c_ctx, ada_w[0], ada_b[0])
    row0 = lambda i: ((i // blk_per) * 2 + jnp.minimum(i % blk_per, 1)) * 6
    xcat = jnp.concatenate([ctx, x], axis=1).reshape(bsz * L, dm)

    a_heads = a_alog.shape[-1]
    a_width = a_heads * LANES
    a_qkv = 3 * a_width
    b_width = b_lam.shape[-1] * b_lam.shape[-2]
    w_in = ev_w_in[0]
    a_cols = a_qkv + a_width + 4 * a_heads
    w_ab = jnp.pad(w_in[:, a_qkv + a_width:a_cols], ((0, 0), (0, LANES - 4 * a_heads)))
    w0 = jnp.concatenate([w_in[:, :a_qkv], w_in[:, a_qkv:a_qkv + a_width], w_in[:, a_cols:a_cols + b_width],
                          w_in[:, a_cols + b_width:], w_ab], axis=1).astype(BF16)
    qkv, z, xb, gate, ab = _norm_mod_proj(xcat, norm_mix[0], tab0, row0, w0, (a_qkv, a_width, b_width, b_width, LANES))

    gparams = _lane_vec([a_alog[0].reshape(-1), a_dtb[0].reshape(-1)])
    ya = _gdn(qkv.reshape(bsz, L, a_qkv), z.reshape(bsz, L, a_width), ab.reshape(bsz, L, LANES),
              a_conv[0], gparams, a_norm[0], n_ctx, seq, a_heads)

    pvec = jnp.zeros((2, SUBLANES, b_width), F32)
    pvec = pvec.at[:, 0].set(b_ba[0].reshape(2, b_width)).at[:, 1].set(b_bx[0].reshape(2, b_width))
    pvec = pvec.at[:, 2].set(b_lam[0].reshape(2, b_width))
    yb = _lru(xb.reshape(bsz, L, b_width), gate.reshape(bsz, L, b_width), b_conv_w[0], b_conv_b[0],
              _block_diag_groups(b_wa[0]).astype(BF16), _block_diag_groups(b_wx[0]).astype(BF16), pvec, n_ctx, seq)

    x1, h1 = _outproj(xcat, ya.reshape(bsz * L, a_width), yb.reshape(bsz * L, b_width), ev_w_out[0], norm_ffn[0],
                      tab0, row0)
    x2 = _peer(h1, x1, peer_wq[0], peer_keys[0], peer_u[0], peer_v[0], tab0, row0, final_norm, False)
    x2 = x2.reshape(bsz, L, dm)

    tab1 = _mod_table(c, c_ctx, ada_w[1], ada_b[1])
    xc = x2[:, :n_ctx].reshape(bsz * n_ctx, dm)
    xl = x2[:, n_ctx:].reshape(bsz, rows, GRID_W, dm).transpose(0, 2, 1, 3).reshape(bsz * seq, dm)
    lat_per = seq // TOK_BLK
    ctx_per = n_ctx // TOK_BLK
    row_lat = lambda i: ((i // lat_per) * 2 + 1) * 6
    row_ctx = lambda i: ((i // ctx_per) * 2) * 6

    c_heads = c_ibias.shape[-1]
    c_width = c_norm.shape[-1]
    d_width = d_scale.shape[-1]
    w_in1 = od_w_in[0]
    n_state = w_in1.shape[1] - c_width - d_width
    c_qk = (n_state - c_width - 4 * c_heads) // 2
    dqk = c_qk // c_heads

    def pad_heads(wcols):
        return jnp.pad(wcols.reshape(dm, c_heads, dqk), ((0, 0), (0, 0), (0, LANES - dqk))).reshape(dm, c_heads * LANES)

    w_gate = jnp.pad(w_in1[:, 2 * c_qk + c_width:n_state], ((0, 0), (0, LANES - 4 * c_heads)))
    w1 = jnp.concatenate([pad_heads(w_in1[:, :c_qk]), pad_heads(w_in1[:, c_qk:2 * c_qk]),
                          w_in1[:, 2 * c_qk:2 * c_qk + c_width], w_in1[:, n_state:n_state + c_width],
                          w_in1[:, n_state + c_width:], w_gate], axis=1).astype(BF16)
    hw = c_heads * LANES
    widths1 = (hw, hw, c_width, c_width, d_width, LANES)
    ql, kl, vl, og, dl, gl = _norm_mod_proj(xl, norm_mix[1], tab1, row_lat, w1, widths1)
    qc, kc, vc, _, _, gc = _norm_mod_proj(xc, norm_mix[1], tab1, row_ctx, w1, widths1)

    gparams1 = _lane_vec([jnp.concatenate([c_ibias[0].reshape(-1), jnp.zeros((2 * c_heads,), F32)]),
                          jnp.concatenate([jnp.zeros((2 * c_heads,), F32), c_fbias[0].reshape(-1)])])
    r3 = lambda a, n: a.reshape(bsz, n, a.shape[-1])
    yc1 = _mlstm(r3(qc, n_ctx), r3(kc, n_ctx), r3(vc, n_ctx), r3(gc, n_ctx),
                 r3(ql, seq), r3(kl, seq), r3(vl, seq), r3(gl, seq), r3(og, seq), gparams1, c_norm[0], c_heads, dqk)
    yd1 = _pool(dl, d_w[0], d_scale[0], rows)

    x3, h3 = _outproj(xl, yc1.reshape(bsz * seq, c_width), yd1, od_w_out[0], norm_ffn[1], tab1, row_lat)
    out = _peer(h3, x3, peer_wq[1], peer_keys[1], peer_u[1], peer_v[1], tab1, row_lat, final_norm, True)
    return out.reshape(bsz, GRID_W, rows, dm).transpose(0, 2, 1, 3).reshape(bsz, seq, dm)
```

```python
import functools
import math

import jax
import jax.numpy as jnp
import numpy as np
from jax import lax
from jax.experimental import pallas as pl
from jax.experimental.pallas import tpu as pltpu

F32 = jnp.float32
BF16 = jnp.bfloat16
HI = lax.Precision.HIGHEST

EPS = 1e-6
GRID_W = 64
CHUNK = 64
LANES = 128
SUBLANES = 8
HALO = 8
CONV_W = 4
LRU_C = 8.0
PEER_TOPK = 16
N_KEYS = 128
POOL_SIZES = (2, 4, 8, 16)
TOK_BLK = 256
PEER_BLK = 2 * TOK_BLK
ROW_BLK = 256
EXP_BLK = 1024
NEG_BIG = -1e30
VMEM_LIMIT = 48 * 1024 * 1024


def _cparams(sem):
    return pltpu.CompilerParams(dimension_semantics=sem, vmem_limit_bytes=VMEM_LIMIT)


def _nt(a, b, precision=None):
    return lax.dot_general(a, b, (((1,), (1,)), ((), ())), precision=precision, preferred_element_type=F32)


def _tn(a, b, precision=None):
    return lax.dot_general(a, b, (((0,), (0,)), ((), ())), precision=precision, preferred_element_type=F32)


def _mm(a, b, precision=None):
    return jnp.dot(a, b, precision=precision, preferred_element_type=F32)


def _silu(x):
    return x * jax.nn.sigmoid(x)


def _softplus(x):
    return jnp.maximum(x, 0.0) + jnp.log1p(jnp.exp(-jnp.abs(x)))


def _gelu_tanh(x):
    return 0.5 * x * (1.0 + jnp.tanh(math.sqrt(2.0 / math.pi) * (x + 0.044715 * (x * x * x))))


def _iota(shape, dim):
    return lax.broadcasted_iota(jnp.int32, shape, dim)


def _ada_kernel(c_ref, w_ref, b_ref, o_ref):
    o_ref[...] = _mm(_silu(c_ref[...]), w_ref[...], HI) + b_ref[...]


def _ada(cc, w, b):
    r, d = cc.shape
    n = w.shape[1]
    tn = 1536
    return pl.pallas_call(
        _ada_kernel,
        grid=(n // tn,),
        in_specs=[pl.BlockSpec((r, d), lambda j: (0, 0)),
                  pl.BlockSpec((d, tn), lambda j: (0, j)),
                  pl.BlockSpec((1, tn), lambda j: (0, j))],
        out_specs=pl.BlockSpec((r, tn), lambda j: (0, j)),
        out_shape=jax.ShapeDtypeStruct((r, n), F32),
        compiler_params=_cparams(("arbitrary",)),
        name="ada_mod",
    )(cc, w, b.reshape(1, n))


def _mod_table(c, c_ctx, w, b):
    bsz, d = c.shape
    rows = ((bsz + 1 + SUBLANES - 1) // SUBLANES) * SUBLANES
    cc = jnp.zeros((rows, d), F32).at[:bsz].set(c).at[bsz].set(c_ctx)
    m = _ada(cc, w, b).reshape(rows, 6, d)
    tab = jnp.stack([jnp.broadcast_to(m[bsz], (bsz, 6, d)), m[:bsz]], axis=1)
    return tab.reshape(bsz * 2 * 6, 1, d)


def _nmm_kernel(x_ref, g_ref, sh_ref, sc_ref, w_ref, *o_refs, widths):
    x = x_ref[...]
    y = x * lax.rsqrt(jnp.mean(x * x, axis=-1, keepdims=True) + EPS) * g_ref[...]
    h = (y * (1.0 + sc_ref[0]) + sh_ref[0]).astype(BF16)
    off = 0
    for o_ref, n in zip(o_refs, widths):
        o_ref[...] = _mm(h, w_ref[:, off:off + n])
        off += n


def _norm_mod_proj(x, gain, tab, rowfn, w, widths):
    t, d = x.shape
    n = w.shape[1]
    tm = TOK_BLK
    return pl.pallas_call(
        functools.partial(_nmm_kernel, widths=widths),
        grid=(t // tm,),
        in_specs=[pl.BlockSpec((tm, d), lambda i: (i, 0)),
                  pl.BlockSpec((1, d), lambda i: (0, 0)),
                  pl.BlockSpec((1, 1, d), lambda i: (rowfn(i), 0, 0)),
                  pl.BlockSpec((1, 1, d), lambda i: (rowfn(i) + 1, 0, 0)),
                  pl.BlockSpec((d, n), lambda i: (0, 0))],
        out_specs=[pl.BlockSpec((tm, wd), lambda i: (i, 0)) for wd in widths],
        out_shape=[jax.ShapeDtypeStruct((t, wd), F32) for wd in widths],
        compiler_params=_cparams(("arbitrary",)),
        name="norm_mod_proj",
    )(x, gain.reshape(1, d), tab, tab, w)


def _conv_chunk(pad_ref, pbase, w, rows=CHUNK):
    blk = pad_ref[pl.ds(pbase - HALO, rows + 2 * HALO), :]
    n = rows + 2 * HALO
    lo, hi = HALO, HALO + rows
    xm1 = pltpu.roll(blk, 1, 0)[lo:hi]
    x0 = blk[lo:hi]
    xp1 = pltpu.roll(blk, n - 1, 0)[lo:hi]
    xp2 = pltpu.roll(blk, n - 2, 0)[lo:hi]
    return w[0:1] * xm1 + w[1:2] * x0 + w[2:3] * xp1 + w[3:4] * xp2


def _fill_padded(pad_ref, src_ref, n_ctx, n_lat):
    width = pad_ref.shape[1]
    z = jnp.zeros((HALO, width), F32)
    pad_ref[0:HALO, :] = z
    pad_ref[HALO + n_ctx:2 * HALO + n_ctx, :] = z
    pad_ref[2 * HALO + n_ctx + n_lat:3 * HALO + n_ctx + n_lat, :] = z
    pad_ref[HALO:HALO + n_ctx, :] = src_ref[0, 0:n_ctx, :]
    pad_ref[2 * HALO + n_ctx:2 * HALO + n_ctx + n_lat, :] = src_ref[0, n_ctx:n_ctx + n_lat, :]


def _padded_base(c, ncc, rows=CHUNK):
    return pl.multiple_of(HALO + c * rows + jnp.where(c >= ncc, HALO, 0), SUBLANES)


def _lane_col(g, lane_idx):
    lane = _iota(g.shape, 1)
    return jnp.sum(jnp.where(lane == lane_idx, g, 0.0), axis=1, keepdims=True)


def _tri(d):
    r = _iota((CHUNK, CHUNK), 0)
    c = _iota((CHUNK, CHUNK), 1)
    incl = (r >= c) if d == 0 else (r <= c)
    strict = (r > c) if d == 0 else (r < c)
    return incl, strict


def _pair_diff(a_rows, b_rows):
    lane = _iota((CHUNK, LANES), 1)
    a = jnp.where(lane == 0, a_rows, jnp.where(lane == 1, 1.0, 0.0))
    b = jnp.where(lane == 0, 1.0, jnp.where(lane == 1, b_rows, 0.0))
    return _nt(a, b, HI)


def _bwd_chunk(s, ncc, nct):
    return jnp.where(s < ncc, ncc - 1 - s, nct + ncc - 1 - s)


def _split3(x):
    x1 = x.astype(BF16)
    r1 = x - x1.astype(F32)
    x2 = r1.astype(BF16)
    x3 = (r1 - x2.astype(F32)).astype(BF16)
    return x1, x2, x3


def _mask_mm(mask, x):
    x1, x2, x3 = _split3(x)
    return _mm(mask, x1) + (_mm(mask, x2) + _mm(mask, x3))


def _mask_tn(x, mask):
    x1, x2, x3 = _split3(x)
    return _tn(x1, mask) + (_tn(x2, mask) + _tn(x3, mask))


def _gate_kernel(ab_ref, gp_ref, gc_ref, gr_ref, *, n_chunks, n_heads, kind):
    r = _iota((CHUNK, CHUNK), 0)
    c = _iota((CHUNK, CHUNK), 1)
    lower = (r >= c).astype(BF16)
    upper = (r <= c).astype(BF16)
    eye = (r == c).astype(BF16)
    lane = _iota((CHUNK, LANES), 1)
    rowg = _iota((LANES, CHUNK), 0)
    if kind == "gdn":
        fwd_lo, rev_lo, rev_hi = 0, n_heads, 2 * n_heads
    else:
        fwd_lo, rev_lo, rev_hi = 2 * n_heads, 3 * n_heads, 4 * n_heads
    p0 = gp_ref[0:1, :]
    p1 = gp_ref[1:2, :]

    def body(ci, carry):
        r0 = pl.multiple_of(ci * CHUNK, CHUNK)
        raw = ab_ref[0, pl.ds(r0, CHUNK), :]
        if kind == "gdn":
            g = jnp.where(lane < 2 * n_heads, -jnp.exp(p0) * _softplus(raw + p1), jax.nn.sigmoid(raw))
        else:
            g = jnp.where(lane < 2 * n_heads, raw + p0, -_softplus(-(raw + p1)))
        is_fwd = (lane >= fwd_lo) & (lane < rev_lo)
        is_rev = (lane >= rev_lo) & (lane < rev_hi)
        gc_ref[0, pl.ds(r0, CHUNK), :] = jnp.where(is_fwd, _mask_mm(lower, g), jnp.where(is_rev, _mask_mm(upper, g), g))
        row_fwd = (rowg >= fwd_lo) & (rowg < rev_lo)
        row_rev = (rowg >= rev_lo) & (rowg < rev_hi)
        rows = jnp.where(row_fwd, _mask_tn(g, upper), jnp.where(row_rev, _mask_tn(g, lower), _mask_tn(g, eye)))
        gr_ref[0, ci] = rows[0:2 * SUBLANES, :]
        return carry

    lax.fori_loop(0, n_chunks, body, 0)


def _gates(ab, gparams, n_heads, kind):
    bsz, L, _ = ab.shape
    nct = L // CHUNK
    assert 4 * n_heads <= 2 * SUBLANES
    return pl.pallas_call(
        functools.partial(_gate_kernel, n_chunks=nct, n_heads=n_heads, kind=kind),
        grid=(bsz,),
        in_specs=[pl.BlockSpec((1, L, LANES), lambda b: (b, 0, 0)),
                  pl.BlockSpec((SUBLANES, LANES), lambda b: (0, 0))],
        out_specs=[pl.BlockSpec((1, L, LANES), lambda b: (b, 0, 0)),
                   pl.BlockSpec((1, nct, 2 * SUBLANES, CHUNK), lambda b: (b, 0, 0, 0))],
        out_shape=[jax.ShapeDtypeStruct((bsz, L, LANES), F32),
                   jax.ShapeDtypeStruct((bsz, nct, 2 * SUBLANES, CHUNK), F32)],
        compiler_params=_cparams(("arbitrary",)),
        name=kind + "_gates",
    )(ab, gparams)


def _mm3(a, b):
    ah = a.astype(BF16)
    al = (a - ah.astype(F32)).astype(BF16)
    bh = b.astype(BF16)
    bl = (b - bh.astype(F32)).astype(BF16)
    return _mm(ah, bh) + (_mm(ah, bl) + _mm(al, bh))


def _gdn_kernel(q_ref, k_ref, v_ref, z_ref, gc_ref, gr_ref, cwq_ref, cwk_ref, cwv_ref, nw_ref, o_ref,
                pad_ref, qs_ref, ks_ref, vs_ref, acc_ref, u_ref, w_ref, qd_ref, kd_ref, qk_ref, gl_ref,
                *, n_ctx, n_lat, n_heads):
    h = pl.program_id(1)
    ncc = n_ctx // CHUNK
    nct = (n_ctx + n_lat) // CHUNK
    dk = qs_ref.shape[1]

    def prep(src_ref, cw_ref, dst_ref, mode):
        _fill_padded(pad_ref, src_ref, n_ctx, n_lat)
        w = cw_ref[...]

        def body(c, carry):
            y = _silu(_conv_chunk(pad_ref, _padded_base(c, n_ctx // ROW_BLK, ROW_BLK), w, ROW_BLK))
            if mode != "v":
                y = y * lax.rsqrt(jnp.sum(y * y, axis=-1, keepdims=True) + EPS)
            if mode == "q":
                y = y * (dk ** -0.5)
            dst_ref[pl.ds(pl.multiple_of(c * ROW_BLK, ROW_BLK), ROW_BLK), :] = y
            return carry

        lax.fori_loop(0, (n_ctx + n_lat) // ROW_BLK, body, 0)

    prep(q_ref, cwq_ref, qs_ref, "q")
    prep(k_ref, cwk_ref, ks_ref, "k")
    prep(v_ref, cwv_ref, vs_ref, "v")

    n_chain = 4
    big = n_chain * CHUNK
    rr = _iota((big, big), 0)
    cc = _iota((big, big), 1)
    log_chunk = CHUNK.bit_length() - 1
    same_blk = lax.shift_right_logical(rr, log_chunk) == lax.shift_right_logical(cc, log_chunk)
    bwd_blk = (lax.shift_right_logical(rr, log_chunk) & 1) == 1
    fwd_blk = jnp.logical_not(bwd_blk)
    strict_bd = same_blk & ((bwd_blk & (rr < cc)) | (fwd_blk & (rr > cc)))
    incl_bd = same_blk & ((bwd_blk & (rr <= cc)) | (fwd_blk & (rr >= cc)))

    def group_setup(c0):
        ks, rhs, gcols, bcols, grows = [], [], [], [], []
        chains = []
        for j in range(2):
            c = c0 + j
            r0 = pl.multiple_of(c * CHUNK, CHUNK)
            acc_ref[pl.ds(r0, CHUNK), :] = jnp.zeros((CHUNK, dk), F32)
            q = qs_ref[pl.ds(r0, CHUNK), :]
            k = ks_ref[pl.ds(r0, CHUNK), :]
            v = vs_ref[pl.ds(r0, CHUNK), :]
            gates = gc_ref[0, pl.ds(r0, CHUNK), :]
            grt = gr_ref[0, c]
            for d in range(2):
                gcol = _lane_col(gates, d * n_heads + h)
                bcol = _lane_col(gates, 2 * n_heads + d * n_heads + h)
                grow = jnp.sum(jnp.where(_iota(grt.shape, 0) == d * n_heads + h, grt, 0.0), axis=0, keepdims=True)
                gtot = gcol[CHUNK - 1:CHUNK, :] if d == 0 else gcol[0:1, :]
                eg = jnp.exp(gcol)
                ks.append(k)
                gcols.append(gcol)
                bcols.append(bcol)
                grows.append(grow)
                rhs.append(jnp.concatenate([bcol * v, (bcol * eg) * k], axis=1))
                chains.append((c, d, r0, q, k, gcol, grow, gtot, eg))
        kst = jnp.concatenate(ks, axis=0).astype(BF16)
        gam = jnp.where(incl_bd, jnp.exp(jnp.minimum(jnp.concatenate(gcols, axis=0) - jnp.concatenate(grows, axis=1),
                                                      0.0)), 0.0)
        m_bd = jnp.where(strict_bd, jnp.concatenate(bcols, axis=0) * _nt(kst, kst) * gam, 0.0)
        return chains, m_bd, jnp.concatenate(rhs, axis=0)

    def group_store(chains, sol):
        for b, (c, d, r0, q, k, gcol, grow, gtot, eg) in enumerate(chains):
            incl, _ = _tri(d)
            gamma = jnp.where(incl, jnp.exp(jnp.minimum(gcol - grow, 0.0)), 0.0)
            u_ref[d, pl.ds(r0, CHUNK), :] = sol[b * CHUNK:(b + 1) * CHUNK, :dk]
            w_ref[d, pl.ds(r0, CHUNK), :] = sol[b * CHUNK:(b + 1) * CHUNK, dk:].astype(BF16)
            qd_ref[d, pl.ds(r0, CHUNK), :] = (q * eg).astype(BF16)
            kd_ref[d, pl.ds(r0, CHUNK), :] = (k * jnp.exp(gtot - gcol)).astype(BF16)
            qk_ref[d, c] = (_nt(q.astype(BF16), k.astype(BF16)) * gamma).astype(BF16)
            gl_ref[d, c] = jnp.broadcast_to(jnp.exp(gtot), (SUBLANES, LANES))

    def prepare_body(c4, carry):
        groups = [group_setup(4 * c4), group_setup(4 * c4 + 2)]
        idx = range(len(groups))
        ms = [g[1] for g in groups]
        sols = [g[2] for g in groups]
        sols = [sols[i] - _mm3(ms[i], sols[i]) for i in idx]
        ps = [m.astype(BF16) for m in ms]
        for it in range(5):
            ps = [_mm(p, p) for p in ps]
            ps = [p.astype(BF16) for p in ps]
            sols = [sols[i] + _mm(ps[i], sols[i].astype(BF16)) for i in idx]
        for i in idx:
            group_store(groups[i][0], sols[i])
        return carry

    lax.fori_loop(0, nct // 4, prepare_body, 0)

    def step(s, carry):
        cs = (s, _bwd_chunk(s, ncc, nct))
        r0 = [pl.multiple_of(c * CHUNK, CHUNK) for c in cs]
        sb = [st.astype(BF16) for st in carry]
        ws = [_mm(w_ref[d, pl.ds(r0[d], CHUNK), :], sb[d]) for d in range(2)]
        qs = [_mm(qd_ref[d, pl.ds(r0[d], CHUNK), :], sb[d]) for d in range(2)]
        vb = [(u_ref[d, pl.ds(r0[d], CHUNK), :] - ws[d]).astype(BF16) for d in range(2)]
        kv = [_tn(kd_ref[d, pl.ds(r0[d], CHUNK), :], vb[d]) for d in range(2)]
        ov = [_mm(qk_ref[d, cs[d]], vb[d]) for d in range(2)]
        for d in range(2):
            acc_ref[pl.ds(r0[d], CHUNK), :] += qs[d] + ov[d]
        return tuple(carry[d] * gl_ref[d, cs[d]][0:1, :] + kv[d] for d in range(2))

    zero = jnp.zeros((dk, dk), F32)
    lax.fori_loop(0, nct, step, (zero, zero))

    nw = nw_ref[...]

    def out_body(c, carry):
        r0 = pl.multiple_of(c * ROW_BLK, ROW_BLK)
        o = acc_ref[pl.ds(r0, ROW_BLK), :]
        y = o * lax.rsqrt(jnp.mean(o * o, axis=-1, keepdims=True) + EPS) * nw
        o_ref[0, pl.ds(r0, ROW_BLK), :] = y * _silu(z_ref[0, pl.ds(r0, ROW_BLK), :])
        return carry

    lax.fori_loop(0, (n_ctx + n_lat) // ROW_BLK, out_body, 0)


def _gdn(qkv, z, ab, conv_w, gparams, norm_w, n_ctx, n_lat, n_heads):
    bsz, L, _ = qkv.shape
    dk = LANES
    seq = lambda off: pl.BlockSpec((1, L, dk), lambda b, h: (b, 0, off + h))
    cw = lambda off: pl.BlockSpec((CONV_W, dk), lambda b, h: (0, off + h))
    lp = L + 3 * HALO
    nct = L // CHUNK
    assert nct % 4 == 0 and n_ctx % ROW_BLK == 0 and n_lat % ROW_BLK == 0
    gc, gr = _gates(ab, gparams, n_heads, "gdn")
    return pl.pallas_call(
        functools.partial(_gdn_kernel, n_ctx=n_ctx, n_lat=n_lat, n_heads=n_heads),
        grid=(bsz, n_heads),
        in_specs=[seq(0), seq(n_heads), seq(2 * n_heads),
                  pl.BlockSpec((1, L, dk), lambda b, h: (b, 0, h)),
                  pl.BlockSpec((1, L, LANES), lambda b, h: (b, 0, 0)),
                  pl.BlockSpec((1, nct, 2 * SUBLANES, CHUNK), lambda b, h: (b, 0, 0, 0)),
                  cw(0), cw(n_heads), cw(2 * n_heads),
                  pl.BlockSpec((1, dk), lambda b, h: (0, 0))],
        out_specs=pl.BlockSpec((1, L, dk), lambda b, h: (b, 0, h)),
        out_shape=jax.ShapeDtypeStruct((bsz, L, n_heads * dk), F32),
        scratch_shapes=[pltpu.VMEM((lp, dk), F32)] + [pltpu.VMEM((L, dk), F32) for _ in range(4)]
        + [pltpu.VMEM((2, L, dk), F32)] + [pltpu.VMEM((2, L, dk), BF16) for _ in range(3)]
        + [pltpu.VMEM((2, nct, CHUNK, CHUNK), BF16), pltpu.VMEM((2, nct, SUBLANES, LANES), F32)],
        compiler_params=_cparams(("arbitrary", "arbitrary")),
        name="gdn_mixer",
    )(qkv, qkv, qkv, z, gc, gr, conv_w, conv_w, conv_w, norm_w.reshape(1, dk))


def _lru_kernel(x_ref, gt_ref, cw_ref, cb_ref, wa_ref, wx_ref, pv_ref, o_ref,
                pad_ref, xc_ref, a_ref, b_ref, acc_ref, *, n_ctx, n_lat):
    ncc = n_ctx // CHUNK
    nct = (n_ctx + n_lat) // CHUNK
    L = n_ctx + n_lat
    width = xc_ref.shape[1]
    _fill_padded(pad_ref, x_ref, n_ctx, n_lat)
    w = cw_ref[...]
    cb = cb_ref[...]

    def conv_body(c, carry):
        r0 = pl.multiple_of(c * CHUNK, CHUNK)
        xc_ref[pl.ds(r0, CHUNK), :] = _conv_chunk(pad_ref, _padded_base(c, ncc), w) + cb
        acc_ref[pl.ds(r0, CHUNK), :] = jnp.zeros((CHUNK, width), F32)
        return carry

    lax.fori_loop(0, nct, conv_body, 0)

    row = _iota((SUBLANES, width), 0)
    bias_a = [pv_ref[d, 0:1, :] for d in range(2)]
    bias_x = [pv_ref[d, 1:2, :] for d in range(2)]
    decay = [-LRU_C * _softplus(-pv_ref[d, 2:3, :]) for d in range(2)]

    def coef_body(c, carry):
        r0 = pl.multiple_of(c * CHUNK, CHUNK)
        x = xc_ref[pl.ds(r0, CHUNK), :]
        xb = x.astype(BF16)
        for d in range(2):
            r = jax.nn.sigmoid(_mm(xb, wa_ref[d, 0]) + bias_a[d])
            i = jax.nn.sigmoid(_mm(xb, wx_ref[d, 0]) + bias_x[d])
            log_a = decay[d] * r
            a_ref[d, pl.ds(r0, CHUNK), :] = jnp.exp(log_a)
            th = jnp.tanh(log_a)
            b_ref[d, pl.ds(r0, CHUNK), :] = jnp.sqrt(-2.0 * th / (1.0 - th)) * (i * x)
        return carry

    lax.fori_loop(0, nct, coef_body, 0)

    nt_ctx = n_ctx // SUBLANES
    nt_all = L // SUBLANES

    def scan_tile(t, d, hc):
        r0 = pl.multiple_of(t * SUBLANES, SUBLANES)
        a = a_ref[d, pl.ds(r0, SUBLANES), :]
        b = b_ref[d, pl.ds(r0, SUBLANES), :]
        for sh in (1, 2, 4):
            if d == 0:
                a_s = pltpu.roll(a, sh, 0)
                b_s = pltpu.roll(b, sh, 0)
                m = row >= sh
            else:
                a_s = pltpu.roll(a, SUBLANES - sh, 0)
                b_s = pltpu.roll(b, SUBLANES - sh, 0)
                m = row < SUBLANES - sh
            b = jnp.where(m, a * b_s + b, b)
            a = jnp.where(m, a * a_s, a)
        hcur = b + a * hc
        acc_ref[pl.ds(r0, SUBLANES), :] += hcur
        last = hcur[SUBLANES - 1:SUBLANES, :] if d == 0 else hcur[0:1, :]
        return jnp.broadcast_to(last, (SUBLANES, width))

    def scan_body(s, carry):
        hf, hb = carry
        tb = jnp.where(s < nt_ctx, nt_ctx - 1 - s, nt_all + nt_ctx - 1 - s)
        return scan_tile(s, 0, hf), scan_tile(tb, 1, hb)

    zero = jnp.zeros((SUBLANES, width), F32)
    lax.fori_loop(0, nt_all, scan_body, (zero, zero))


    def out_body(c, carry):
        r0 = pl.multiple_of(c * CHUNK, CHUNK)
        o_ref[0, pl.ds(r0, CHUNK), :] = acc_ref[pl.ds(r0, CHUNK), :] * _gelu_tanh(gt_ref[0, pl.ds(r0, CHUNK), :])
        return carry

    lax.fori_loop(0, nct, out_body, 0)


def _lru(xb, gate, conv_w, conv_b, wa, wx, pvec, n_ctx, n_lat):
    bsz, L, width = xb.shape
    ng = width // LANES
    lp = L + 3 * HALO
    seq = pl.BlockSpec((1, L, LANES), lambda b, j: (b, 0, j))
    return pl.pallas_call(
        functools.partial(_lru_kernel, n_ctx=n_ctx, n_lat=n_lat),
        grid=(bsz, ng),
        in_specs=[seq, seq,
                  pl.BlockSpec((CONV_W, LANES), lambda b, j: (0, j)),
                  pl.BlockSpec((1, LANES), lambda b, j: (0, j)),
                  pl.BlockSpec((2, 1, LANES, LANES), lambda b, j: (0, j, 0, 0)),
                  pl.BlockSpec((2, 1, LANES, LANES), lambda b, j: (0, j, 0, 0)),
                  pl.BlockSpec((2, SUBLANES, LANES), lambda b, j: (0, 0, j))],
        out_specs=seq,
        out_shape=jax.ShapeDtypeStruct((bsz, L, width), F32),
        scratch_shapes=[pltpu.VMEM((lp, LANES), F32), pltpu.VMEM((L, LANES), F32), pltpu.VMEM((2, L, LANES), F32),
                        pltpu.VMEM((2, L, LANES), F32), pltpu.VMEM((L, LANES), F32)],
        compiler_params=_cparams(("arbitrary", "arbitrary")),
        name="lru_mixer",
    )(xb, gate, conv_w, conv_b.reshape(1, width), wa, wx, pvec)


def _mlstm_kernel(qc_ref, kc_ref, vc_ref, gcc_ref, grc_ref, ql_ref, kl_ref, vl_ref, gcl_ref, grl_ref, og_ref, nw_ref,
                  o_ref, acc_ref, cc_ref, ms_ref, p2_ref, mi_ref, bc_ref, *, n_ctx, n_lat, n_heads, dqk):
    h = pl.program_id(1)
    ncc = n_ctx // CHUNK
    ncl = n_lat // CHUNK
    dv = LANES
    lane = _iota((CHUNK, LANES), 1)
    ones_blk = jnp.ones((CHUNK, LANES), F32)

    def prepare(refs, slot0, chains, with_out):
        q_ref, k_ref, v_ref, gcol_ref, grow_ref = refs
        idx = range(len(chains))
        r0 = [pl.multiple_of(c * CHUNK, CHUNK) for c, _ in chains]
        k = [k_ref[0, pl.ds(r0[i], CHUNK), :] * (dqk ** -0.5) for i in idx]
        v_ext = [jnp.concatenate([v_ref[0, pl.ds(r0[i], CHUNK), :], ones_blk], axis=1).astype(BF16) for i in idx]
        gates = [gcol_ref[0, pl.ds(r0[i], CHUNK), :] for i in idx]
        li = [_lane_col(gates[i], chains[i][1] * n_heads + h) for i in idx]
        bcum = [_lane_col(gates[i], 2 * n_heads + chains[i][1] * n_heads + h) for i in idx]
        btot = [bcum[i][CHUNK - 1:CHUNK, :] if chains[i][1] == 0 else bcum[i][0:1, :] for i in idx]
        w_end = [btot[i] - bcum[i] + li[i] for i in idx]
        m_chunk = [jnp.max(w_end[i], axis=0, keepdims=True) for i in idx]
        c_chunk = [_tn((k[i] * jnp.exp(w_end[i] - m_chunk[i])).astype(BF16), v_ext[i]) for i in idx]
        if with_out:
            qk = [_nt(q_ref[0, pl.ds(r0[i], CHUNK), :].astype(BF16), k[i].astype(BF16)) for i in idx]
            s_qk, m_intra = [], []
            for i in idx:
                c, d = chains[i]
                grt = grow_ref[0, c]
                rowi = _iota(grt.shape, 0)
                li_row = jnp.sum(jnp.where(rowi == d * n_heads + h, grt, 0.0), axis=0, keepdims=True)
                b_row = jnp.sum(jnp.where(rowi == 2 * n_heads + d * n_heads + h, grt, 0.0), axis=0, keepdims=True)
                incl, _ = _tri(d)
                log_d = jnp.where(incl, bcum[i] + (li_row - b_row), NEG_BIG)
                m_intra.append(jnp.max(log_d, axis=1, keepdims=True))
                s_qk.append((qk[i] * jnp.exp(log_d - m_intra[i])).astype(BF16))
            p2 = [_mm(s_qk[i], v_ext[i]) for i in idx]
        row8 = _iota((SUBLANES, LANES), 0)
        for i in idx:
            c, d = chains[i]
            cc_ref[d, slot0 + c] = c_chunk[i]
            ms_ref[d, slot0 + c] = jnp.where(row8 == 0, btot[i], m_chunk[i])
            if with_out:
                p2_ref[d, pl.ds(r0[i], CHUNK), :] = p2[i]
                mi_ref[d, pl.ds(r0[i], CHUNK), :] = jnp.broadcast_to(m_intra[i], (CHUNK, LANES))
                bc_ref[d, pl.ds(r0[i], CHUNK), :] = jnp.broadcast_to(bcum[i], (CHUNK, LANES))

    ctx_refs = (qc_ref, kc_ref, vc_ref, gcc_ref, grc_ref)
    lat_refs = (ql_ref, kl_ref, vl_ref, gcl_ref, grl_ref)

    def ctx_prep(c2, carry):
        prepare(ctx_refs, 0, [(2 * c2 + j, d) for j in range(2) for d in range(2)], False)
        return carry

    lax.fori_loop(0, ncc // 2, ctx_prep, 0)

    def lat_prep(c2, carry):
        for j in range(2):
            acc_ref[pl.ds(pl.multiple_of((2 * c2 + j) * CHUNK, CHUNK), CHUNK), :] = jnp.zeros((CHUNK, dv), F32)
        prepare(lat_refs, ncc, [(2 * c2 + j, d) for j in range(2) for d in range(2)], True)
        return carry

    lax.fori_loop(0, ncl // 2, lat_prep, 0)

    def advance(states, slots):
        out = []
        for d in range(2):
            cx, m_s = states[d]
            ms = ms_ref[d, slots[d]]
            btot = ms[0:1, :]
            m_chunk = ms[1:2, :]
            m_new = jnp.maximum(btot + m_s, m_chunk)
            f_old = jnp.exp(btot + m_s - m_new)
            f_new = jnp.exp(m_chunk - m_new)
            wide = lambda f: jnp.concatenate([f, f], axis=1)
            out.append((wide(f_old) * cx + wide(f_new) * cc_ref[d, slots[d]], m_new))
        return tuple(out)

    def ctx_step(s, carry):
        return advance(carry, (s, ncc - 1 - s))

    zero = (jnp.zeros((LANES, 2 * dv), F32), jnp.zeros((1, LANES), F32))
    carry = lax.fori_loop(0, ncc, ctx_step, (zero, zero))

    def lat_step(s, carry):
        cs = (s, ncl - 1 - s)
        r0 = [pl.multiple_of(c * CHUNK, CHUNK) for c in cs]
        p1 = [_mm(ql_ref[0, pl.ds(r0[d], CHUNK), :].astype(BF16), carry[d][0].astype(BF16)) for d in range(2)]
        for d in range(2):
            m_intra = mi_ref[d, pl.ds(r0[d], CHUNK), :]
            m_inter = bc_ref[d, pl.ds(r0[d], CHUNK), :] + carry[d][1]
            m_tot = jnp.maximum(m_inter, m_intra)
            w_inter = jnp.exp(m_inter - m_tot)
            w_intra = jnp.exp(m_intra - m_tot)
            p2 = p2_ref[d, pl.ds(r0[d], CHUNK), :]
            num = w_inter * p1[d][:, :dv] + w_intra * p2[:, :dv]
            den = w_inter * p1[d][:, dv:] + w_intra * p2[:, dv:]
            acc_ref[pl.ds(r0[d], CHUNK), :] += num / jnp.maximum(jnp.abs(den), jnp.exp(-m_tot))
        return advance(carry, (ncc + cs[0], ncc + cs[1]))

    lax.fori_loop(0, ncl, lat_step, carry)

    nw = nw_ref[...]

    def out_body(c, carry):
        r0 = pl.multiple_of(c * ROW_BLK, ROW_BLK)
        hs = acc_ref[pl.ds(r0, ROW_BLK), :]
        y = hs * lax.rsqrt(jnp.mean(hs * hs, axis=-1, keepdims=True) + EPS) * nw
        o_ref[0, pl.ds(r0, ROW_BLK), :] = y * jax.nn.sigmoid(og_ref[0, pl.ds(r0, ROW_BLK), :])
        return carry

    lax.fori_loop(0, n_lat // ROW_BLK, out_body, 0)


def _mlstm(qc, kc, vc, gc, ql, kl, vl, gl, og, gparams, norm_w, n_heads, dqk):
    bsz, n_ctx, _ = qc.shape
    n_lat = ql.shape[1]
    cs = pl.BlockSpec((1, n_ctx, LANES), lambda b, h: (b, 0, h))
    ls = pl.BlockSpec((1, n_lat, LANES), lambda b, h: (b, 0, h))
    assert n_ctx % (2 * CHUNK) == 0 and n_lat % (2 * CHUNK) == 0 and n_lat % ROW_BLK == 0
    gcc, grc = _gates(gc, gparams, n_heads, "mlstm")
    gcl, grl = _gates(gl, gparams, n_heads, "mlstm")
    row_spec = lambda n: pl.BlockSpec((1, n // CHUNK, 2 * SUBLANES, CHUNK), lambda b, h: (b, 0, 0, 0))
    return pl.pallas_call(
        functools.partial(_mlstm_kernel, n_ctx=n_ctx, n_lat=n_lat, n_heads=n_heads, dqk=dqk),
        grid=(bsz, n_heads),
        in_specs=[cs, cs, cs, pl.BlockSpec((1, n_ctx, LANES), lambda b, h: (b, 0, 0)), row_spec(n_ctx),
                  ls, ls, ls, pl.BlockSpec((1, n_lat, LANES), lambda b, h: (b, 0, 0)), row_spec(n_lat),
                  ls,
                  pl.BlockSpec((1, LANES), lambda b, h: (0, h))],
        out_specs=ls,
        out_shape=jax.ShapeDtypeStruct((bsz, n_lat, n_heads * LANES), F32),
        scratch_shapes=[pltpu.VMEM((n_lat, LANES), F32),
                        pltpu.VMEM((2, (n_ctx + n_lat) // CHUNK, LANES, 2 * LANES), F32),
                        pltpu.VMEM((2, (n_ctx + n_lat) // CHUNK, SUBLANES, LANES), F32),
                        pltpu.VMEM((2, n_lat, 2 * LANES), F32),
                        pltpu.VMEM((2, n_lat, LANES), F32),
                        pltpu.VMEM((2, n_lat, LANES), F32)],
        compiler_params=_cparams(("arbitrary", "arbitrary")),
        name="mlstm_mixer",
    )(qc, kc, vc, gcc, grc, ql, kl, vl, gcl, grl, og, norm_w.reshape(1, -1))


def _pool_kernel(x_ref, w_ref, sc_ref, o_ref, *, seg):
    tm = x_ref.shape[0]
    r = _iota((tm, tm), 0)
    c = _iota((tm, tm), 1)
    sh = seg.bit_length() - 1
    same = lax.shift_right_logical(r, sh) == lax.shift_right_logical(c, sh)
    t = r & (seg - 1)
    s = c & (seg - 1)
    for gi, wsz in enumerate(POOL_SIZES):
        lo = jnp.maximum(t - wsz // 2, 0)
        hi = jnp.minimum(t - wsz // 2 + wsz, seg)
        inwin = same & (s >= lo) & (s < hi)
        pmat = jnp.where(inwin, 1.0 / (hi - lo).astype(F32), 0.0) - jnp.where(r == c, 1.0, 0.0)
        x = x_ref[:, gi * LANES:(gi + 1) * LANES]
        pooled = _mm3(pmat, x)
        y = _mm(pooled.astype(BF16), w_ref[gi].astype(BF16))
        o_ref[:, gi * LANES:(gi + 1) * LANES] = y * sc_ref[:, gi * LANES:(gi + 1) * LANES]


def _pool(x, w_grp, scale, seg):
    t, width = x.shape
    tm = TOK_BLK
    assert seg & (seg - 1) == 0 and tm % seg == 0
    return pl.pallas_call(
        functools.partial(_pool_kernel, seg=seg),
        grid=(t // tm,),
        in_specs=[pl.BlockSpec((tm, width), lambda i: (i, 0)),
                  pl.BlockSpec(w_grp.shape, lambda i: (0, 0, 0)),
                  pl.BlockSpec((1, width), lambda i: (0, 0))],
        out_specs=pl.BlockSpec((tm, width), lambda i: (i, 0)),
        out_shape=jax.ShapeDtypeStruct((t, width), F32),
        compiler_params=_cparams(("arbitrary",)),
        name="pool_mixer",
    )(x, w_grp, scale.reshape(1, width))


def _outproj_kernel(x_ref, ya_ref, yb_ref, wa_ref, wb_ref, g1_ref, gn_ref, sh_ref, sc_ref, xo_ref, h_ref):
    y = _mm(ya_ref[...].astype(BF16), wa_ref[...]) + _mm(yb_ref[...].astype(BF16), wb_ref[...])
    xn = x_ref[...] + g1_ref[0] * y
    xo_ref[...] = xn
    hn = xn * lax.rsqrt(jnp.mean(xn * xn, axis=-1, keepdims=True) + EPS) * gn_ref[...]
    h_ref[...] = (hn * (1.0 + sc_ref[0]) + sh_ref[0]).astype(BF16)


def _outproj(x, ya, yb, w_out, gain, tab, rowfn):
    t, d = x.shape
    wa_n = ya.shape[1]
    tm = TOK_BLK
    w = w_out.astype(BF16)
    mod = lambda k: pl.BlockSpec((1, 1, d), lambda i: (rowfn(i) + k, 0, 0))
    return pl.pallas_call(
        _outproj_kernel,
        grid=(t // tm,),
        in_specs=[pl.BlockSpec((tm, d), lambda i: (i, 0)),
                  pl.BlockSpec((tm, wa_n), lambda i: (i, 0)),
                  pl.BlockSpec((tm, yb.shape[1]), lambda i: (i, 0)),
                  pl.BlockSpec((wa_n, d), lambda i: (0, 0)),
                  pl.BlockSpec((yb.shape[1], d), lambda i: (0, 0)),
                  mod(2),
                  pl.BlockSpec((1, d), lambda i: (0, 0)),
                  mod(3), mod(4)],
        out_specs=[pl.BlockSpec((tm, d), lambda i: (i, 0)), pl.BlockSpec((tm, d), lambda i: (i, 0))],
        out_shape=[jax.ShapeDtypeStruct((t, d), F32), jax.ShapeDtypeStruct((t, d), BF16)],
        compiler_params=_cparams(("arbitrary",)),
        name="out_proj",
    )(x, ya, yb, w[:wa_n], w[wa_n:], tab, gain.reshape(1, d), tab, tab)


def _hyperbola_pairs():
    return [(j1, j2) for j1 in range(PEER_TOPK) for j2 in range(PEER_TOPK) if (j1 + 1) * (j2 + 1) <= PEER_TOPK]


def _topk_rows(s, k, break_ties):
    n, t = s.shape
    rowi = _iota((n, t), 0).astype(F32)
    rank = jnp.full((n, t), float(2 * k), F32)
    cur = s
    vals = []
    for j in range(k):
        m = jnp.max(cur, axis=0, keepdims=True)
        sel = cur == m
        if break_ties:
            sel = rowi == jnp.min(jnp.where(sel, rowi, float(n)), axis=0, keepdims=True)
        rank = jnp.where(sel, float(j), rank)
        cur = jnp.where(sel, -jnp.inf, cur)
        vals.append(m)
    return vals, rank


def _candidate_stage(v1, v2, break_ties):
    pairs = _hyperbola_pairs()
    cand = jnp.concatenate([v1[j1] + v2[j2] for j1, j2 in pairs], axis=0)
    _, crank = _topk_rows(cand, PEER_TOPK, break_ties)
    chosen = crank < PEER_TOPK
    zsum = jnp.sum(jnp.where(chosen, jnp.exp(cand - (v1[0] + v2[0])), 0.0), axis=0, keepdims=True)
    counts = jnp.where(chosen, 1.0, 0.0)
    n_by_rank = []
    row = 0
    for j1 in range(PEER_TOPK):
        width = PEER_TOPK // (j1 + 1)
        n_by_rank.append(jnp.sum(counts[row:row + width], axis=0, keepdims=True))
        row += width
    return n_by_rank, zsum, jnp.sum(counts, axis=0, keepdims=True)


def _route(s1, s2):
    v1, rank1 = _topk_rows(s1, PEER_TOPK, True)
    v2, rank2 = _topk_rows(s2, PEER_TOPK, True)
    n_by_rank, zsum, _ = _candidate_stage(v1, v2, True)
    n1 = jnp.zeros_like(rank1)
    for j1 in range(PEER_TOPK):
        n1 = jnp.where(rank1 == float(j1), n_by_rank[j1], n1)
    e2 = jnp.where(rank2 < PEER_TOPK, jnp.exp(s2 - v2[0]), 0.0)
    e1 = jnp.where(rank1 < PEER_TOPK, jnp.exp(s1 - v1[0]), 0.0) / zsum
    return rank2, e2, n1, e1


def _sorted_top(s):
    n = s.shape[0] // SUBLANES
    a = [s[SUBLANES * j:SUBLANES * (j + 1), :] for j in range(n)]

    def cex(i, l):
        a[i], a[l] = jnp.maximum(a[i], a[l]), jnp.minimum(a[i], a[l])

    k = 2
    while k <= n:
        j = k // 2
        while j >= 1:
            for i in range(n):
                l = i ^ j
                if l > i:
                    if (i & k) == 0:
                        cex(i, l)
                    else:
                        cex(l, i)
            j //= 2
        k *= 2
    for shift in (4, 2, 1):
        a = [jnp.maximum(a[j], pltpu.roll(a[n - 1 - j], shift, 0)) for j in range(n)]
        j = n // 2
        while j >= 1:
            for i in range(n):
                l = i ^ j
                if l > i:
                    cex(i, l)
            j //= 2
    return a


def _by_rank(bits, table):
    level = list(table)
    for g in reversed(bits):
        level = [jnp.where(g, level[2 * i + 1], level[2 * i]) for i in range(len(level) // 2)]
    return level[0]


def _rank_bits(x, v):
    g8 = v[7] > x
    g4 = jnp.where(g8, v[11], v[3]) > x
    g2 = _by_rank([g8, g4], [v[1], v[5], v[9], v[13]]) > x
    g1 = _by_rank([g8, g4, g2], [v[2 * i] for i in range(8)]) > x
    return [g8, g4, g2, g1]


def _route_untied(s1, s2):
    v1 = _sorted_top(s1)
    v2 = _sorted_top(s2)
    n_by_rank, zsum, _ = _candidate_stage([v[0:1, :] for v in v1], [v[0:1, :] for v in v2], True)
    shape = v1[0].shape
    n_tab = [jnp.broadcast_to(nj, shape) for nj in n_by_rank]
    inv_z = jnp.broadcast_to(1.0 / zsum, shape)
    rank2, e2, n1, e1 = [], [], [], []
    cnt1 = jnp.zeros(shape, F32)
    cnt2 = jnp.zeros(shape, F32)
    for r in range(s1.shape[0] // SUBLANES):
        x1 = s1[SUBLANES * r:SUBLANES * (r + 1), :]
        x2 = s2[SUBLANES * r:SUBLANES * (r + 1), :]
        in1 = x1 >= v1[PEER_TOPK - 1]
        in2 = x2 >= v2[PEER_TOPK - 1]
        n1.append(jnp.where(in1, _by_rank(_rank_bits(x1, v1), n_tab), 0.0))
        e1.append(jnp.where(in1, jnp.exp(x1 - v1[0]) * inv_z, 0.0))
        g8, g4, g2, g1 = _rank_bits(x2, v2)
        rk = (jnp.where(g8, 8.0, 0.0) + jnp.where(g4, 4.0, 0.0)) + (jnp.where(g2, 2.0, 0.0) + jnp.where(g1, 1.0, 0.0))
        rank2.append(jnp.where(in2, rk, float(2 * PEER_TOPK)))
        e2.append(jnp.where(in2, jnp.exp(x2 - v2[0]), 0.0))
        cnt1 = cnt1 + jnp.where(in1, 1.0, 0.0)
        cnt2 = cnt2 + jnp.where(in2, 1.0, 0.0)
    tied = jnp.zeros((1, shape[1]), F32)
    for cnt in (cnt1, cnt2):
        tied = tied + jnp.abs(jnp.sum(cnt, axis=0, keepdims=True) - float(PEER_TOPK))
    for v in (v1, v2):
        for j in range(PEER_TOPK - 1):
            tied = tied + jnp.where(v[j][0:1, :] == v[j + 1][0:1, :], 1.0, 0.0)
    cat = lambda parts: jnp.concatenate(parts, axis=0)
    return cat(rank2), cat(e2), cat(n1), cat(e1), tied


def _peer_select_kernel(h_ref, wq_ref, keys_ref, r2_ref, e2_ref, n1_ref, e1_ref):
    q = _mm(h_ref[...], wq_ref[0])
    s1 = _nt(keys_ref[0, 0], q[:, :N_KEYS], HI)
    s2 = _nt(keys_ref[0, 1], q[:, N_KEYS:], HI)

    def emit(rank2, e2, n1, e1):
        r2_ref[0] = rank2.astype(r2_ref.dtype)
        e2_ref[0] = e2.astype(e2_ref.dtype)
        n1_ref[0] = n1
        e1_ref[0] = e1

    rank2, e2, n1, e1, tied = _route_untied(s1, s2)
    emit(rank2, e2, n1, e1)

    @pl.when(jnp.max(tied) > 0.0)
    def _():
        emit(*_route(s1, s2))


def _peer_select(h, wq, keys):
    t, d = h.shape
    nh = keys.shape[0]
    qd = wq.shape[2]
    tm = PEER_BLK
    ospec = pl.BlockSpec((1, N_KEYS, tm), lambda i, hh: (hh, 0, i))
    return pl.pallas_call(
        _peer_select_kernel,
        grid=(t // tm, nh),
        in_specs=[pl.BlockSpec((tm, d), lambda i, hh: (i, 0)),
                  pl.BlockSpec((1, d, qd), lambda i, hh: (hh, 0, 0)),
                  pl.BlockSpec((1, 2, N_KEYS, qd // 2), lambda i, hh: (hh, 0, 0, 0))],
        out_specs=[ospec, ospec, ospec, ospec],
        out_shape=[jax.ShapeDtypeStruct((nh, N_KEYS, t), dt) for dt in (BF16, BF16, F32, F32)],
        compiler_params=_cparams(("arbitrary", "arbitrary")),
        name="peer_select",
    )(h, wq, keys)


def _peer_dense_kernel(h_ref, u_ref, vt_ref, r2_ref, e2_ref, n1_ref, e1_ref, x_ref, g2a_ref, g2b_ref, fn_ref, o_ref,
                       acc_ref, wact_ref, *, n_heads, final):
    eb = pl.program_id(1)
    n_eb = pl.num_programs(1) - 1
    cur = eb & 1
    prev = 1 - cur

    @pl.when(eb == 0)
    def _():
        acc_ref[...] = jnp.zeros_like(acc_ref)
        wact_ref[1] = jnp.zeros(wact_ref.shape[1:], BF16)

    @pl.when(eb < n_eb)
    def _():
        hb = h_ref[...]
        n_part = 4
        part_rows = EXP_BLK // n_part
        acts = [_nt(u_ref[p * part_rows:(p + 1) * part_rows, :], hb) for p in range(n_part)]
        upd = _mm(vt_ref[0], wact_ref[prev])
        rows_per = EXP_BLK // N_KEYS
        i1_base = pl.multiple_of(eb * rows_per, rows_per)
        n1_tiles = [n1_ref[hh, pl.ds(i1_base, rows_per), :] for hh in range(n_heads)]
        e1_tiles = [e1_ref[hh, pl.ds(i1_base, rows_per), :] for hh in range(n_heads)]
        zero = jnp.zeros((), BF16)
        for r in range(rows_per):
            wgt = None
            for hh in range(n_heads):
                shape = r2_ref.shape[1:]
                n1row = jnp.broadcast_to(n1_tiles[hh][r:r + 1, :], shape).astype(BF16)
                e1row = jnp.broadcast_to(e1_tiles[hh][r:r + 1, :], shape).astype(BF16)
                term = jnp.where(r2_ref[hh] < n1row, e2_ref[hh], zero) * e1row
                wgt = term if wgt is None else wgt + term
            off = (r * N_KEYS) % part_rows
            a = acts[(r * N_KEYS) // part_rows][off:off + N_KEYS, :]
            wact_ref[cur, r * N_KEYS:(r + 1) * N_KEYS, :] = _gelu_tanh(a.astype(BF16)) * wgt
        acc_ref[...] += upd

    @pl.when(eb == n_eb)
    def _():
        acc_ref[...] += _mm(vt_ref[0], wact_ref[prev])
        for part, g2_ref in enumerate((g2a_ref, g2b_ref)):
            rows = slice(part * TOK_BLK, (part + 1) * TOK_BLK)
            y = x_ref[rows, :] + g2_ref[0] * acc_ref[:, rows].T
            if final:
                y = y * lax.rsqrt(jnp.mean(y * y, axis=-1, keepdims=True) + EPS) * fn_ref[...]
            o_ref[rows, :] = y


def _peer_dense(h, u, vt, r2, e2, n1, e1, x, tab, rowfn, final_gain, final):
    t, d = h.shape
    n_eb = u.shape[0] // EXP_BLK
    nh = r2.shape[0]
    tm = PEER_BLK
    sel = pl.BlockSpec((nh, N_KEYS, tm), lambda i, e: (0, 0, i))
    g2 = lambda part: pl.BlockSpec((1, 1, d), lambda i, e: (rowfn(2 * i + part) + 5, 0, 0))
    return pl.pallas_call(
        functools.partial(_peer_dense_kernel, n_heads=nh, final=final),
        grid=(t // tm, n_eb + 1),
        in_specs=[pl.BlockSpec((tm, d), lambda i, e: (i, 0)),
                  pl.BlockSpec((EXP_BLK, d), lambda i, e: (jnp.minimum(e, n_eb - 1), 0)),
                  pl.BlockSpec((1, d, EXP_BLK), lambda i, e: (jnp.maximum(e - 1, 0), 0, 0)),
                  sel, sel, sel, sel,
                  pl.BlockSpec((tm, d), lambda i, e: (i, 0)),
                  g2(0), g2(1),
                  pl.BlockSpec((1, d), lambda i, e: (0, 0))],
        out_specs=pl.BlockSpec((tm, d), lambda i, e: (i, 0)),
        out_shape=jax.ShapeDtypeStruct((t, d), F32),
        scratch_shapes=[pltpu.VMEM((d, tm), F32), pltpu.VMEM((2, EXP_BLK, tm), BF16)],
        compiler_params=_cparams(("arbitrary", "arbitrary")),
        name="peer_dense",
    )(h, u, vt, r2, e2, n1, e1, x, tab, tab, final_gain.reshape(1, d))


def _peer(h, x, wq, keys, u_tab, v_tab, tab, rowfn, final_gain, final):
    d = h.shape[1]
    nh = keys.shape[0]
    wq_h = wq.reshape(d, nh, -1).transpose(1, 0, 2).astype(BF16)
    vt = v_tab.reshape(-1, EXP_BLK, d).transpose(0, 2, 1).astype(BF16)
    r2, e2, n1, e1 = _peer_select(h, wq_h, keys)
    return _peer_dense(h, u_tab.astype(BF16), vt, r2, e2, n1, e1, x, tab, rowfn, final_gain, final)


def _lane_vec(vals, rows=SUBLANES):
    out = jnp.zeros((rows, LANES), F32)
    for r, v in enumerate(vals):
        out = out.at[r, :v.shape[0]].set(v.astype(F32))
    return out


def _block_diag_groups(w):
    two, nb, bd, _ = w.shape
    per = LANES // bd
    wg = w.reshape(two, nb // per, per, bd, bd)
    eye = jnp.eye(per, dtype=w.dtype)
    return jnp.einsum("dgpij,pq->dgpiqj", wg, eye).reshape(two, nb // per, LANES, LANES)


def kernel(x, c, ctx, c_ctx, ada_w, ada_b, norm_mix, norm_ffn, final_norm, peer_wq, peer_keys, peer_u, peer_v,
           ev_w_in, ev_w_out, a_conv, a_alog, a_dtb, a_norm, b_conv_w, b_conv_b, b_wa, b_ba, b_wx, b_bx, b_lam,
           od_w_in, od_w_out, c_ibias, c_fbias, c_norm, d_w, d_scale):
    bsz, seq, dm = x.shape
    n_ctx = ctx.shape[1]
    L = n_ctx + seq
    rows = seq // GRID_W
    blk_per = L // TOK_BLK

    tab0 = _mod_table(c, c_ctx, ada_w[0], ada_b[0])
    row0 = lambda i: ((i // blk_per) * 2 + jnp.minimum(i % blk_per, 1)) * 6
    xcat = jnp.concatenate([ctx, x], axis=1).reshape(bsz * L, dm)

    a_heads = a_alog.shape[-1]
    a_width = a_heads * LANES
    a_qkv = 3 * a_width
    b_width = b_lam.shape[-1] * b_lam.shape[-2]
    w_in = ev_w_in[0]
    a_cols = a_qkv + a_width + 4 * a_heads
    w_ab = jnp.pad(w_in[:, a_qkv + a_width:a_cols], ((0, 0), (0, LANES - 4 * a_heads)))
    w0 = jnp.concatenate([w_in[:, :a_qkv], w_in[:, a_qkv:a_qkv + a_width], w_in[:, a_cols:a_cols + b_width],
                          w_in[:, a_cols + b_width:], w_ab], axis=1).astype(BF16)
    qkv, z, xb, gate, ab = _norm_mod_proj(xcat, norm_mix[0], tab0, row0, w0, (a_qkv, a_width, b_width, b_width, LANES))

    gparams = _lane_vec([a_alog[0].reshape(-1), a_dtb[0].reshape(-1)])
    ya = _gdn(qkv.reshape(bsz, L, a_qkv), z.reshape(bsz, L, a_width), ab.reshape(bsz, L, LANES),
              a_conv[0], gparams, a_norm[0], n_ctx, seq, a_heads)

    pvec = jnp.zeros((2, SUBLANES, b_width), F32)
    pvec = pvec.at[:, 0].set(b_ba[0].reshape(2, b_width)).at[:, 1].set(b_bx[0].reshape(2, b_width))
    pvec = pvec.at[:, 2].set(b_lam[0].reshape(2, b_width))
    yb = _lru(xb.reshape(bsz, L, b_width), gate.reshape(bsz, L, b_width), b_conv_w[0], b_conv_b[0],
              _block_diag_groups(b_wa[0]).astype(BF16), _block_diag_groups(b_wx[0]).astype(BF16), pvec, n_ctx, seq)

    x1, h1 = _outproj(xcat, ya.reshape(bsz * L, a_width), yb.reshape(bsz * L, b_width), ev_w_out[0], norm_ffn[0],
                      tab0, row0)
    x2 = _peer(h1, x1, peer_wq[0], peer_keys[0], peer_u[0], peer_v[0], tab0, row0, final_norm, False)
    x2 = x2.reshape(bsz, L, dm)

    tab1 = _mod_table(c, c_ctx, ada_w[1], ada_b[1])
    xc = x2[:, :n_ctx].reshape(bsz * n_ctx, dm)
    xl = x2[:, n_ctx:].reshape(bsz, rows, GRID_W, dm).transpose(0, 2, 1, 3).reshape(bsz * seq, dm)
    lat_per = seq // TOK_BLK
    ctx_per = n_ctx // TOK_BLK
    row_lat = lambda i: ((i // lat_per) * 2 + 1) * 6
    row_ctx = lambda i: ((i // ctx_per) * 2) * 6

    c_heads = c_ibias.shape[-1]
    c_width = c_norm.shape[-1]
    d_width = d_scale.shape[-1]
    w_in1 = od_w_in[0]
    n_state = w_in1.shape[1] - c_width - d_width
    c_qk = (n_state - c_width - 4 * c_heads) // 2
    dqk = c_qk // c_heads

    def pad_heads(wcols):
        return jnp.pad(wcols.reshape(dm, c_heads, dqk), ((0, 0), (0, 0), (0, LANES - dqk))).reshape(dm, c_heads * LANES)

    w_gate = jnp.pad(w_in1[:, 2 * c_qk + c_width:n_state], ((0, 0), (0, LANES - 4 * c_heads)))
    w1 = jnp.concatenate([pad_heads(w_in1[:, :c_qk]), pad_heads(w_in1[:, c_qk:2 * c_qk]),
                          w_in1[:, 2 * c_qk:2 * c_qk + c_width], w_in1[:, n_state:n_state + c_width],
                          w_in1[:, n_state + c_width:], w_gate], axis=1).astype(BF16)
    hw = c_heads * LANES
    widths1 = (hw, hw, c_width, c_width, d_width, LANES)
    ql, kl, vl, og, dl, gl = _norm_mod_proj(xl, norm_mix[1], tab1, row_lat, w1, widths1)
    qc, kc, vc, _, _, gc = _norm_mod_proj(xc, norm_mix[1], tab1, row_ctx, w1, widths1)

    gparams1 = _lane_vec([jnp.concatenate([c_ibias[0].reshape(-1), jnp.zeros((2 * c_heads,), F32)]),
                          jnp.concatenate([jnp.zeros((2 * c_heads,), F32), c_fbias[0].reshape(-1)])])
    r3 = lambda a, n: a.reshape(bsz, n, a.shape[-1])
    yc1 = _mlstm(r3(qc, n_ctx), r3(kc, n_ctx), r3(vc, n_ctx), r3(gc, n_ctx),
                 r3(ql, seq), r3(kl, seq), r3(vl, seq), r3(gl, seq), r3(og, seq), gparams1, c_norm[0], c_heads, dqk)
    yd1 = _pool(dl, d_w[0], d_scale[0], rows)

    x3, h3 = _outproj(xl, yc1.reshape(bsz * seq, c_width), yd1, od_w_out[0], norm_ffn[1], tab1, row_lat)
    out = _peer(h3, x3, peer_wq[1], peer_keys[1], peer_u[1], peer_v[1], tab1, row_lat, final_norm, True)
    return out.reshape(bsz, GRID_W, rows, dm).transpose(0, 2, 1, 3).reshape(bsz, seq, dm)
```

```python
import functools
import math

import jax
import jax.numpy as jnp
import numpy as np
from jax import lax
from jax.experimental import pallas as pl
from jax.experimental.pallas import tpu as pltpu

F32 = jnp.float32
BF16 = jnp.bfloat16
HI = lax.Precision.HIGHEST

EPS = 1e-6
GRID_W = 64
CHUNK = 64
LANES = 128
SUBLANES = 8
HALO = 8
CONV_W = 4
LRU_C = 8.0
PEER_TOPK = 16
N_KEYS = 128
POOL_SIZES = (2, 4, 8, 16)
TOK_BLK = 256
PEER_BLK = 2 * TOK_BLK
ROW_BLK = 256
EXP_BLK = 2048
NEG_BIG = -1e30
VMEM_LIMIT = 56 * 1024 * 1024


def _cparams(sem):
    return pltpu.CompilerParams(dimension_semantics=sem, vmem_limit_bytes=VMEM_LIMIT)


def _nt(a, b, precision=None):
    return lax.dot_general(a, b, (((1,), (1,)), ((), ())), precision=precision, preferred_element_type=F32)


def _tn(a, b, precision=None):
    return lax.dot_general(a, b, (((0,), (0,)), ((), ())), precision=precision, preferred_element_type=F32)


def _mm(a, b, precision=None):
    return jnp.dot(a, b, precision=precision, preferred_element_type=F32)


def _silu(x):
    return x * jax.nn.sigmoid(x)


def _softplus(x):
    return jnp.maximum(x, 0.0) + jnp.log1p(jnp.exp(-jnp.abs(x)))


def _gelu_tanh(x):
    return 0.5 * x * (1.0 + jnp.tanh(math.sqrt(2.0 / math.pi) * (x + 0.044715 * (x * x * x))))


def _iota(shape, dim):
    return lax.broadcasted_iota(jnp.int32, shape, dim)


def _ada_kernel(c_ref, w_ref, b_ref, o_ref):
    o_ref[...] = _mm(_silu(c_ref[...]), w_ref[...], HI) + b_ref[...]


def _ada(cc, w, b):
    r, d = cc.shape
    n = w.shape[1]
    tn = 1536
    return pl.pallas_call(
        _ada_kernel,
        grid=(n // tn,),
        in_specs=[pl.BlockSpec((r, d), lambda j: (0, 0)),
                  pl.BlockSpec((d, tn), lambda j: (0, j)),
                  pl.BlockSpec((1, tn), lambda j: (0, j))],
        out_specs=pl.BlockSpec((r, tn), lambda j: (0, j)),
        out_shape=jax.ShapeDtypeStruct((r, n), F32),
        compiler_params=_cparams(("arbitrary",)),
        name="ada_mod",
    )(cc, w, b.reshape(1, n))


def _mod_table(c, c_ctx, w, b):
    bsz, d = c.shape
    rows = ((bsz + 1 + SUBLANES - 1) // SUBLANES) * SUBLANES
    cc = jnp.zeros((rows, d), F32).at[:bsz].set(c).at[bsz].set(c_ctx)
    m = _ada(cc, w, b).reshape(rows, 6, d)
    tab = jnp.stack([jnp.broadcast_to(m[bsz], (bsz, 6, d)), m[:bsz]], axis=1)
    return tab.reshape(bsz * 2 * 6, 1, d)


def _nmm_kernel(x_ref, g_ref, sh_ref, sc_ref, w_ref, *o_refs, widths):
    x = x_ref[...]
    y = x * lax.rsqrt(jnp.mean(x * x, axis=-1, keepdims=True) + EPS) * g_ref[...]
    h = (y * (1.0 + sc_ref[0]) + sh_ref[0]).astype(BF16)
    off = 0
    for o_ref, n in zip(o_refs, widths):
        o_ref[...] = _mm(h, w_ref[:, off:off + n])
        off += n


def _norm_mod_proj(x, gain, tab, rowfn, w, widths):
    t, d = x.shape
    n = w.shape[1]
    tm = TOK_BLK
    return pl.pallas_call(
        functools.partial(_nmm_kernel, widths=widths),
        grid=(t // tm,),
        in_specs=[pl.BlockSpec((tm, d), lambda i: (i, 0)),
                  pl.BlockSpec((1, d), lambda i: (0, 0)),
                  pl.BlockSpec((1, 1, d), lambda i: (rowfn(i), 0, 0)),
                  pl.BlockSpec((1, 1, d), lambda i: (rowfn(i) + 1, 0, 0)),
                  pl.BlockSpec((d, n), lambda i: (0, 0))],
        out_specs=[pl.BlockSpec((tm, wd), lambda i: (i, 0)) for wd in widths],
        out_shape=[jax.ShapeDtypeStruct((t, wd), F32) for wd in widths],
        compiler_params=_cparams(("arbitrary",)),
        name="norm_mod_proj",
    )(x, gain.reshape(1, d), tab, tab, w)


def _conv_chunk(pad_ref, pbase, w, rows=CHUNK):
    blk = pad_ref[pl.ds(pbase - HALO, rows + 2 * HALO), :]
    n = rows + 2 * HALO
    lo, hi = HALO, HALO + rows
    xm1 = pltpu.roll(blk, 1, 0)[lo:hi]
    x0 = blk[lo:hi]
    xp1 = pltpu.roll(blk, n - 1, 0)[lo:hi]
    xp2 = pltpu.roll(blk, n - 2, 0)[lo:hi]
    return w[0:1] * xm1 + w[1:2] * x0 + w[2:3] * xp1 + w[3:4] * xp2


def _fill_padded(pad_ref, src_ref, n_ctx, n_lat):
    width = pad_ref.shape[1]
    z = jnp.zeros((HALO, width), F32)
    pad_ref[0:HALO, :] = z
    pad_ref[HALO + n_ctx:2 * HALO + n_ctx, :] = z
    pad_ref[2 * HALO + n_ctx + n_lat:3 * HALO + n_ctx + n_lat, :] = z
    pad_ref[HALO:HALO + n_ctx, :] = src_ref[0, 0:n_ctx, :]
    pad_ref[2 * HALO + n_ctx:2 * HALO + n_ctx + n_lat, :] = src_ref[0, n_ctx:n_ctx + n_lat, :]


def _padded_base(c, ncc, rows=CHUNK):
    return pl.multiple_of(HALO + c * rows + jnp.where(c >= ncc, HALO, 0), SUBLANES)


def _lane_col(g, lane_idx):
    lane = _iota(g.shape, 1)
    return jnp.sum(jnp.where(lane == lane_idx, g, 0.0), axis=1, keepdims=True)


def _tri(d):
    r = _iota((CHUNK, CHUNK), 0)
    c = _iota((CHUNK, CHUNK), 1)
    incl = (r >= c) if d == 0 else (r <= c)
    strict = (r > c) if d == 0 else (r < c)
    return incl, strict


def _pair_diff(a_rows, b_rows):
    lane = _iota((CHUNK, LANES), 1)
    a = jnp.where(lane == 0, a_rows, jnp.where(lane == 1, 1.0, 0.0))
    b = jnp.where(lane == 0, 1.0, jnp.where(lane == 1, b_rows, 0.0))
    return _nt(a, b, HI)


def _bwd_chunk(s, ncc, nct):
    return jnp.where(s < ncc, ncc - 1 - s, nct + ncc - 1 - s)


def _split3(x):
    x1 = x.astype(BF16)
    r1 = x - x1.astype(F32)
    x2 = r1.astype(BF16)
    x3 = (r1 - x2.astype(F32)).astype(BF16)
    return x1, x2, x3


def _mask_mm(mask, x):
    x1, x2, x3 = _split3(x)
    return _mm(mask, x1) + (_mm(mask, x2) + _mm(mask, x3))


def _mask_tn(x, mask):
    x1, x2, x3 = _split3(x)
    return _tn(x1, mask) + (_tn(x2, mask) + _tn(x3, mask))


def _gate_kernel(ab_ref, gp_ref, gc_ref, gr_ref, *, n_chunks, n_heads, kind):
    r = _iota((CHUNK, CHUNK), 0)
    c = _iota((CHUNK, CHUNK), 1)
    lower = (r >= c).astype(BF16)
    upper = (r <= c).astype(BF16)
    eye = (r == c).astype(BF16)
    lane = _iota((CHUNK, LANES), 1)
    rowg = _iota((LANES, CHUNK), 0)
    if kind == "gdn":
        fwd_lo, rev_lo, rev_hi = 0, n_heads, 2 * n_heads
    else:
        fwd_lo, rev_lo, rev_hi = 2 * n_heads, 3 * n_heads, 4 * n_heads
    p0 = gp_ref[0:1, :]
    p1 = gp_ref[1:2, :]

    def body(ci, carry):
        r0 = pl.multiple_of(ci * CHUNK, CHUNK)
        raw = ab_ref[0, pl.ds(r0, CHUNK), :]
        if kind == "gdn":
            g = jnp.where(lane < 2 * n_heads, -jnp.exp(p0) * _softplus(raw + p1), jax.nn.sigmoid(raw))
        else:
            g = jnp.where(lane < 2 * n_heads, raw + p0, -_softplus(-(raw + p1)))
        is_fwd = (lane >= fwd_lo) & (lane < rev_lo)
        is_rev = (lane >= rev_lo) & (lane < rev_hi)
        gc_ref[0, pl.ds(r0, CHUNK), :] = jnp.where(is_fwd, _mask_mm(lower, g), jnp.where(is_rev, _mask_mm(upper, g), g))
        row_fwd = (rowg >= fwd_lo) & (rowg < rev_lo)
        row_rev = (rowg >= rev_lo) & (rowg < rev_hi)
        rows = jnp.where(row_fwd, _mask_tn(g, upper), jnp.where(row_rev, _mask_tn(g, lower), _mask_tn(g, eye)))
        gr_ref[0, ci] = rows[0:2 * SUBLANES, :]
        return carry

    lax.fori_loop(0, n_chunks, body, 0)


def _gates(ab, gparams, n_heads, kind):
    bsz, L, _ = ab.shape
    nct = L // CHUNK
    assert 4 * n_heads <= 2 * SUBLANES
    return pl.pallas_call(
        functools.partial(_gate_kernel, n_chunks=nct, n_heads=n_heads, kind=kind),
        grid=(bsz,),
        in_specs=[pl.BlockSpec((1, L, LANES), lambda b: (b, 0, 0)),
                  pl.BlockSpec((SUBLANES, LANES), lambda b: (0, 0))],
        out_specs=[pl.BlockSpec((1, L, LANES), lambda b: (b, 0, 0)),
                   pl.BlockSpec((1, nct, 2 * SUBLANES, CHUNK), lambda b: (b, 0, 0, 0))],
        out_shape=[jax.ShapeDtypeStruct((bsz, L, LANES), F32),
                   jax.ShapeDtypeStruct((bsz, nct, 2 * SUBLANES, CHUNK), F32)],
        compiler_params=_cparams(("arbitrary",)),
        name=kind + "_gates",
    )(ab, gparams)


def _mm3(a, b):
    ah = a.astype(BF16)
    al = (a - ah.astype(F32)).astype(BF16)
    bh = b.astype(BF16)
    bl = (b - bh.astype(F32)).astype(BF16)
    return _mm(ah, bh) + (_mm(ah, bl) + _mm(al, bh))


def _gdn_kernel(q_ref, k_ref, v_ref, z_ref, gc_ref, gr_ref, cwq_ref, cwk_ref, cwv_ref, nw_ref, o_ref,
                pad_ref, qs_ref, ks_ref, vs_ref, acc_ref, u_ref, w_ref, qd_ref, kd_ref, qk_ref, gl_ref,
                *, n_ctx, n_lat, n_heads):
    h = pl.program_id(1)
    ncc = n_ctx // CHUNK
    nct = (n_ctx + n_lat) // CHUNK
    dk = qs_ref.shape[1]

    def prep(src_ref, cw_ref, dst_ref, mode):
        _fill_padded(pad_ref, src_ref, n_ctx, n_lat)
        w = cw_ref[...]

        def body(c, carry):
            y = _silu(_conv_chunk(pad_ref, _padded_base(c, n_ctx // ROW_BLK, ROW_BLK), w, ROW_BLK))
            if mode != "v":
                y = y * lax.rsqrt(jnp.sum(y * y, axis=-1, keepdims=True) + EPS)
            if mode == "q":
                y = y * (dk ** -0.5)
            dst_ref[pl.ds(pl.multiple_of(c * ROW_BLK, ROW_BLK), ROW_BLK), :] = y
            return carry

        lax.fori_loop(0, (n_ctx + n_lat) // ROW_BLK, body, 0)

    prep(q_ref, cwq_ref, qs_ref, "q")
    prep(k_ref, cwk_ref, ks_ref, "k")
    prep(v_ref, cwv_ref, vs_ref, "v")

    n_chain = 4
    big = n_chain * CHUNK
    rr = _iota((big, big), 0)
    cc = _iota((big, big), 1)
    log_chunk = CHUNK.bit_length() - 1
    same_blk = lax.shift_right_logical(rr, log_chunk) == lax.shift_right_logical(cc, log_chunk)
    bwd_blk = (lax.shift_right_logical(rr, log_chunk) & 1) == 1
    fwd_blk = jnp.logical_not(bwd_blk)
    strict_bd = same_blk & ((bwd_blk & (rr < cc)) | (fwd_blk & (rr > cc)))
    incl_bd = same_blk & ((bwd_blk & (rr <= cc)) | (fwd_blk & (rr >= cc)))

    def group_setup(c0):
        ks, rhs, gcols, bcols, grows = [], [], [], [], []
        chains = []
        for j in range(2):
            c = c0 + j
            r0 = pl.multiple_of(c * CHUNK, CHUNK)
            acc_ref[pl.ds(r0, CHUNK), :] = jnp.zeros((CHUNK, dk), F32)
            q = qs_ref[pl.ds(r0, CHUNK), :]
            k = ks_ref[pl.ds(r0, CHUNK), :]
            v = vs_ref[pl.ds(r0, CHUNK), :]
            gates = gc_ref[0, pl.ds(r0, CHUNK), :]
            grt = gr_ref[0, c]
            for d in range(2):
                gcol = _lane_col(gates, d * n_heads + h)
                bcol = _lane_col(gates, 2 * n_heads + d * n_heads + h)
                grow = jnp.sum(jnp.where(_iota(grt.shape, 0) == d * n_heads + h, grt, 0.0), axis=0, keepdims=True)
                gtot = gcol[CHUNK - 1:CHUNK, :] if d == 0 else gcol[0:1, :]
                eg = jnp.exp(gcol)
                ks.append(k)
                gcols.append(gcol)
                bcols.append(bcol)
                grows.append(grow)
                rhs.append(jnp.concatenate([bcol * v, (bcol * eg) * k], axis=1))
                chains.append((c, d, r0, q, k, gcol, grow, gtot, eg))
        kst = jnp.concatenate(ks, axis=0).astype(BF16)
        gam = jnp.where(incl_bd, jnp.exp(jnp.minimum(jnp.concatenate(gcols, axis=0) - jnp.concatenate(grows, axis=1),
                                                      0.0)), 0.0)
        m_bd = jnp.where(strict_bd, jnp.concatenate(bcols, axis=0) * _nt(kst, kst) * gam, 0.0)
        return chains, m_bd, jnp.concatenate(rhs, axis=0)

    def group_store(chains, sol):
        for b, (c, d, r0, q, k, gcol, grow, gtot, eg) in enumerate(chains):
            incl, _ = _tri(d)
            gamma = jnp.where(incl, jnp.exp(jnp.minimum(gcol - grow, 0.0)), 0.0)
            u_ref[d, pl.ds(r0, CHUNK), :] = sol[b * CHUNK:(b + 1) * CHUNK, :dk]
            w_ref[d, pl.ds(r0, CHUNK), :] = sol[b * CHUNK:(b + 1) * CHUNK, dk:].astype(BF16)
            qd_ref[d, pl.ds(r0, CHUNK), :] = (q * eg).astype(BF16)
            kd_ref[d, pl.ds(r0, CHUNK), :] = (k * jnp.exp(gtot - gcol)).astype(BF16)
            qk_ref[d, c] = (_nt(q.astype(BF16), k.astype(BF16)) * gamma).astype(BF16)
            gl_ref[d, c] = jnp.broadcast_to(jnp.exp(gtot), (SUBLANES, LANES))

    def prepare_body(c4, carry):
        groups = [group_setup(4 * c4), group_setup(4 * c4 + 2)]
        idx = range(len(groups))
        ms = [g[1] for g in groups]
        sols = [g[2] for g in groups]
        sols = [sols[i] - _mm3(ms[i], sols[i]) for i in idx]
        ps = [m.astype(BF16) for m in ms]
        for it in range(5):
            ps = [_mm(p, p) for p in ps]
            ps = [p.astype(BF16) for p in ps]
            sols = [sols[i] + _mm(ps[i], sols[i].astype(BF16)) for i in idx]
        for i in idx:
            group_store(groups[i][0], sols[i])
        return carry

    lax.fori_loop(0, nct // 4, prepare_body, 0)

    def step(s, carry):
        cs = (s, _bwd_chunk(s, ncc, nct))
        r0 = [pl.multiple_of(c * CHUNK, CHUNK) for c in cs]
        sb = [st.astype(BF16) for st in carry]
        ws = [_mm(w_ref[d, pl.ds(r0[d], CHUNK), :], sb[d]) for d in range(2)]
        qs = [_mm(qd_ref[d, pl.ds(r0[d], CHUNK), :], sb[d]) for d in range(2)]
        vb = [(u_ref[d, pl.ds(r0[d], CHUNK), :] - ws[d]).astype(BF16) for d in range(2)]
        kv = [_tn(kd_ref[d, pl.ds(r0[d], CHUNK), :], vb[d]) for d in range(2)]
        ov = [_mm(qk_ref[d, cs[d]], vb[d]) for d in range(2)]
        for d in range(2):
            acc_ref[pl.ds(r0[d], CHUNK), :] += qs[d] + ov[d]
        return tuple(carry[d] * gl_ref[d, cs[d]][0:1, :] + kv[d] for d in range(2))

    zero = jnp.zeros((dk, dk), F32)
    lax.fori_loop(0, nct, step, (zero, zero))

    nw = nw_ref[...]

    def out_body(c, carry):
        r0 = pl.multiple_of(c * ROW_BLK, ROW_BLK)
        o = acc_ref[pl.ds(r0, ROW_BLK), :]
        y = o * lax.rsqrt(jnp.mean(o * o, axis=-1, keepdims=True) + EPS) * nw
        o_ref[0, pl.ds(r0, ROW_BLK), :] = y * _silu(z_ref[0, pl.ds(r0, ROW_BLK), :])
        return carry

    lax.fori_loop(0, (n_ctx + n_lat) // ROW_BLK, out_body, 0)


def _gdn(qkv, z, ab, conv_w, gparams, norm_w, n_ctx, n_lat, n_heads):
    bsz, L, _ = qkv.shape
    dk = LANES
    seq = lambda off: pl.BlockSpec((1, L, dk), lambda b, h: (b, 0, off + h))
    cw = lambda off: pl.BlockSpec((CONV_W, dk), lambda b, h: (0, off + h))
    lp = L + 3 * HALO
    nct = L // CHUNK
    assert nct % 4 == 0 and n_ctx % ROW_BLK == 0 and n_lat % ROW_BLK == 0
    gc, gr = _gates(ab, gparams, n_heads, "gdn")
    return pl.pallas_call(
        functools.partial(_gdn_kernel, n_ctx=n_ctx, n_lat=n_lat, n_heads=n_heads),
        grid=(bsz, n_heads),
        in_specs=[seq(0), seq(n_heads), seq(2 * n_heads),
                  pl.BlockSpec((1, L, dk), lambda b, h: (b, 0, h)),
                  pl.BlockSpec((1, L, LANES), lambda b, h: (b, 0, 0)),
                  pl.BlockSpec((1, nct, 2 * SUBLANES, CHUNK), lambda b, h: (b, 0, 0, 0)),
                  cw(0), cw(n_heads), cw(2 * n_heads),
                  pl.BlockSpec((1, dk), lambda b, h: (0, 0))],
        out_specs=pl.BlockSpec((1, L, dk), lambda b, h: (b, 0, h)),
        out_shape=jax.ShapeDtypeStruct((bsz, L, n_heads * dk), F32),
        scratch_shapes=[pltpu.VMEM((lp, dk), F32)] + [pltpu.VMEM((L, dk), F32) for _ in range(4)]
        + [pltpu.VMEM((2, L, dk), F32)] + [pltpu.VMEM((2, L, dk), BF16) for _ in range(3)]
        + [pltpu.VMEM((2, nct, CHUNK, CHUNK), BF16), pltpu.VMEM((2, nct, SUBLANES, LANES), F32)],
        compiler_params=_cparams(("arbitrary", "arbitrary")),
        name="gdn_mixer",
    )(qkv, qkv, qkv, z, gc, gr, conv_w, conv_w, conv_w, norm_w.reshape(1, dk))


def _lru_kernel(x_ref, gt_ref, cw_ref, cb_ref, wa_ref, wx_ref, pv_ref, o_ref,
                pad_ref, xc_ref, a_ref, b_ref, acc_ref, *, n_ctx, n_lat):
    ncc = n_ctx // CHUNK
    nct = (n_ctx + n_lat) // CHUNK
    L = n_ctx + n_lat
    width = xc_ref.shape[1]
    _fill_padded(pad_ref, x_ref, n_ctx, n_lat)
    w = cw_ref[...]
    cb = cb_ref[...]

    def conv_body(c, carry):
        r0 = pl.multiple_of(c * CHUNK, CHUNK)
        xc_ref[pl.ds(r0, CHUNK), :] = _conv_chunk(pad_ref, _padded_base(c, ncc), w) + cb
        acc_ref[pl.ds(r0, CHUNK), :] = jnp.zeros((CHUNK, width), F32)
        return carry

    lax.fori_loop(0, nct, conv_body, 0)

    row = _iota((SUBLANES, width), 0)
    bias_a = [pv_ref[d, 0:1, :] for d in range(2)]
    bias_x = [pv_ref[d, 1:2, :] for d in range(2)]
    decay = [-LRU_C * _softplus(-pv_ref[d, 2:3, :]) for d in range(2)]

    def coef_body(c, carry):
        r0 = pl.multiple_of(c * ROW_BLK, ROW_BLK)
        x = xc_ref[pl.ds(r0, ROW_BLK), :]
        xb = x.astype(BF16)
        for d in range(2):
            r = jax.nn.sigmoid(_mm(xb, wa_ref[d, 0]) + bias_a[d])
            i = jax.nn.sigmoid(_mm(xb, wx_ref[d, 0]) + bias_x[d])
            log_a = decay[d] * r
            a_ref[d, pl.ds(r0, ROW_BLK), :] = jnp.exp(log_a)
            th = jnp.tanh(log_a)
            b_ref[d, pl.ds(r0, ROW_BLK), :] = jnp.sqrt(-2.0 * th / (1.0 - th)) * (i * x)
        return carry

    lax.fori_loop(0, L // ROW_BLK, coef_body, 0)

    nt_ctx = n_ctx // SUBLANES
    nt_all = L // SUBLANES

    def scan_tile(t, d, hc):
        r0 = pl.multiple_of(t * SUBLANES, SUBLANES)
        a = a_ref[d, pl.ds(r0, SUBLANES), :]
        b = b_ref[d, pl.ds(r0, SUBLANES), :]
        for sh in (1, 2, 4):
            if d == 0:
                a_s = pltpu.roll(a, sh, 0)
                b_s = pltpu.roll(b, sh, 0)
                m = row >= sh
            else:
                a_s = pltpu.roll(a, SUBLANES - sh, 0)
                b_s = pltpu.roll(b, SUBLANES - sh, 0)
                m = row < SUBLANES - sh
            b = jnp.where(m, a * b_s + b, b)
            a = jnp.where(m, a * a_s, a)
        hcur = b + a * hc
        acc_ref[pl.ds(r0, SUBLANES), :] += hcur
        last = hcur[SUBLANES - 1:SUBLANES, :] if d == 0 else hcur[0:1, :]
        return jnp.broadcast_to(last, (SUBLANES, width))

    def scan_body(s, carry):
        hf, hb = carry
        tb = jnp.where(s < nt_ctx, nt_ctx - 1 - s, nt_all + nt_ctx - 1 - s)
        return scan_tile(s, 0, hf), scan_tile(tb, 1, hb)

    zero = jnp.zeros((SUBLANES, width), F32)
    lax.fori_loop(0, nt_all, scan_body, (zero, zero))


    def out_body(c, carry):
        r0 = pl.multiple_of(c * CHUNK, CHUNK)
        o_ref[0, pl.ds(r0, CHUNK), :] = acc_ref[pl.ds(r0, CHUNK), :] * _gelu_tanh(gt_ref[0, pl.ds(r0, CHUNK), :])
        return carry

    lax.fori_loop(0, nct, out_body, 0)


def _lru(xb, gate, conv_w, conv_b, wa, wx, pvec, n_ctx, n_lat):
    bsz, L, width = xb.shape
    ng = width // LANES
    lp = L + 3 * HALO
    assert L % ROW_BLK == 0
    seq = pl.BlockSpec((1, L, LANES), lambda b, j: (b, 0, j))
    return pl.pallas_call(
        functools.partial(_lru_kernel, n_ctx=n_ctx, n_lat=n_lat),
        grid=(bsz, ng),
        in_specs=[seq, seq,
                  pl.BlockSpec((CONV_W, LANES), lambda b, j: (0, j)),
                  pl.BlockSpec((1, LANES), lambda b, j: (0, j)),
                  pl.BlockSpec((2, 1, LANES, LANES), lambda b, j: (0, j, 0, 0)),
                  pl.BlockSpec((2, 1, LANES, LANES), lambda b, j: (0, j, 0, 0)),
                  pl.BlockSpec((2, SUBLANES, LANES), lambda b, j: (0, 0, j))],
        out_specs=seq,
        out_shape=jax.ShapeDtypeStruct((bsz, L, width), F32),
        scratch_shapes=[pltpu.VMEM((lp, LANES), F32), pltpu.VMEM((L, LANES), F32), pltpu.VMEM((2, L, LANES), F32),
                        pltpu.VMEM((2, L, LANES), F32), pltpu.VMEM((L, LANES), F32)],
        compiler_params=_cparams(("arbitrary", "arbitrary")),
        name="lru_mixer",
    )(xb, gate, conv_w, conv_b.reshape(1, width), wa, wx, pvec)


def _mlstm_kernel(qc_ref, kc_ref, vc_ref, gcc_ref, grc_ref, ql_ref, kl_ref, vl_ref, gcl_ref, grl_ref, og_ref, nw_ref,
                  o_ref, acc_ref, cc_ref, ms_ref, p2_ref, mi_ref, bc_ref, *, n_ctx, n_lat, n_heads, dqk):
    h = pl.program_id(1)
    ncc = n_ctx // CHUNK
    ncl = n_lat // CHUNK
    dv = LANES
    lane = _iota((CHUNK, LANES), 1)
    ones_blk = jnp.ones((CHUNK, LANES), F32)

    def prepare(refs, slot0, chains, with_out):
        q_ref, k_ref, v_ref, gcol_ref, grow_ref = refs
        idx = range(len(chains))
        r0 = [pl.multiple_of(c * CHUNK, CHUNK) for c, _ in chains]
        k = [k_ref[0, pl.ds(r0[i], CHUNK), :] * (dqk ** -0.5) for i in idx]
        v_ext = [jnp.concatenate([v_ref[0, pl.ds(r0[i], CHUNK), :], ones_blk], axis=1).astype(BF16) for i in idx]
        gates = [gcol_ref[0, pl.ds(r0[i], CHUNK), :] for i in idx]
        li = [_lane_col(gates[i], chains[i][1] * n_heads + h) for i in idx]
        bcum = [_lane_col(gates[i], 2 * n_heads + chains[i][1] * n_heads + h) for i in idx]
        btot = [bcum[i][CHUNK - 1:CHUNK, :] if chains[i][1] == 0 else bcum[i][0:1, :] for i in idx]
        w_end = [btot[i] - bcum[i] + li[i] for i in idx]
        m_chunk = [jnp.max(w_end[i], axis=0, keepdims=True) for i in idx]
        c_chunk = [_tn((k[i] * jnp.exp(w_end[i] - m_chunk[i])).astype(BF16), v_ext[i]) for i in idx]
        if with_out:
            qk = [_nt(q_ref[0, pl.ds(r0[i], CHUNK), :].astype(BF16), k[i].astype(BF16)) for i in idx]
            s_qk, m_intra = [], []
            for i in idx:
                c, d = chains[i]
                grt = grow_ref[0, c]
                rowi = _iota(grt.shape, 0)
                li_row = jnp.sum(jnp.where(rowi == d * n_heads + h, grt, 0.0), axis=0, keepdims=True)
                b_row = jnp.sum(jnp.where(rowi == 2 * n_heads + d * n_heads + h, grt, 0.0), axis=0, keepdims=True)
                incl, _ = _tri(d)
                log_d = jnp.where(incl, bcum[i] + (li_row - b_row), NEG_BIG)
                m_intra.append(jnp.max(log_d, axis=1, keepdims=True))
                s_qk.append((qk[i] * jnp.exp(log_d - m_intra[i])).astype(BF16))
            p2 = [_mm(s_qk[i], v_ext[i]) for i in idx]
        row8 = _iota((SUBLANES, LANES), 0)
        for i in idx:
            c, d = chains[i]
            cc_ref[d, slot0 + c] = c_chunk[i]
            ms_ref[d, slot0 + c] = jnp.where(row8 == 0, btot[i], m_chunk[i])
            if with_out:
                p2_ref[d, pl.ds(r0[i], CHUNK), :] = p2[i]
                mi_ref[d, pl.ds(r0[i], CHUNK), :] = jnp.broadcast_to(m_intra[i], (CHUNK, LANES))
                bc_ref[d, pl.ds(r0[i], CHUNK), :] = jnp.broadcast_to(bcum[i], (CHUNK, LANES))

    ctx_refs = (qc_ref, kc_ref, vc_ref, gcc_ref, grc_ref)
    lat_refs = (ql_ref, kl_ref, vl_ref, gcl_ref, grl_ref)

    def ctx_prep(c2, carry):
        prepare(ctx_refs, 0, [(2 * c2 + j, d) for j in range(2) for d in range(2)], False)
        return carry

    lax.fori_loop(0, ncc // 2, ctx_prep, 0)

    def lat_prep(c4, carry):
        for j in range(4):
            acc_ref[pl.ds(pl.multiple_of((4 * c4 + j) * CHUNK, CHUNK), CHUNK), :] = jnp.zeros((CHUNK, dv), F32)
        prepare(lat_refs, ncc, [(4 * c4 + j, d) for j in range(4) for d in range(2)], True)
        return carry

    lax.fori_loop(0, ncl // 4, lat_prep, 0)

    def advance(states, slots):
        out = []
        for d in range(2):
            cx, m_s = states[d]
            ms = ms_ref[d, slots[d]]
            btot = ms[0:1, :]
            m_chunk = ms[1:2, :]
            m_new = jnp.maximum(btot + m_s, m_chunk)
            f_old = jnp.exp(btot + m_s - m_new)
            f_new = jnp.exp(m_chunk - m_new)
            wide = lambda f: jnp.concatenate([f, f], axis=1)
            out.append((wide(f_old) * cx + wide(f_new) * cc_ref[d, slots[d]], m_new))
        return tuple(out)

    def ctx_step(s, carry):
        return advance(carry, (s, ncc - 1 - s))

    zero = (jnp.zeros((LANES, 2 * dv), F32), jnp.zeros((1, LANES), F32))
    carry = lax.fori_loop(0, ncc, ctx_step, (zero, zero))

    def lat_step(s, carry):
        cs = (s, ncl - 1 - s)
        r0 = [pl.multiple_of(c * CHUNK, CHUNK) for c in cs]
        p1 = [_mm(ql_ref[0, pl.ds(r0[d], CHUNK), :].astype(BF16), carry[d][0].astype(BF16)) for d in range(2)]
        for d in range(2):
            m_intra = mi_ref[d, pl.ds(r0[d], CHUNK), :]
            m_inter = bc_ref[d, pl.ds(r0[d], CHUNK), :] + carry[d][1]
            m_tot = jnp.maximum(m_inter, m_intra)
            w_inter = jnp.exp(m_inter - m_tot)
            w_intra = jnp.exp(m_intra - m_tot)
            p2 = p2_ref[d, pl.ds(r0[d], CHUNK), :]
            num = w_inter * p1[d][:, :dv] + w_intra * p2[:, :dv]
            den = w_inter * p1[d][:, dv:] + w_intra * p2[:, dv:]
            acc_ref[pl.ds(r0[d], CHUNK), :] += num / jnp.maximum(jnp.abs(den), jnp.exp(-m_tot))
        return advance(carry, (ncc + cs[0], ncc + cs[1]))

    lax.fori_loop(0, ncl, lat_step, carry)

    nw = nw_ref[...]

    def out_body(c, carry):
        r0 = pl.multiple_of(c * ROW_BLK, ROW_BLK)
        hs = acc_ref[pl.ds(r0, ROW_BLK), :]
        y = hs * lax.rsqrt(jnp.mean(hs * hs, axis=-1, keepdims=True) + EPS) * nw
        o_ref[0, pl.ds(r0, ROW_BLK), :] = y * jax.nn.sigmoid(og_ref[0, pl.ds(r0, ROW_BLK), :])
        return carry

    lax.fori_loop(0, n_lat // ROW_BLK, out_body, 0)


def _mlstm(qc, kc, vc, gc, ql, kl, vl, gl, og, gparams, norm_w, n_heads, dqk):
    bsz, n_ctx, _ = qc.shape
    n_lat = ql.shape[1]
    cs = pl.BlockSpec((1, n_ctx, LANES), lambda b, h: (b, 0, h))
    ls = pl.BlockSpec((1, n_lat, LANES), lambda b, h: (b, 0, h))
    assert n_ctx % (2 * CHUNK) == 0 and n_lat % (4 * CHUNK) == 0 and n_lat % ROW_BLK == 0
    gcc, grc = _gates(gc, gparams, n_heads, "mlstm")
    gcl, grl = _gates(gl, gparams, n_heads, "mlstm")
    row_spec = lambda n: pl.BlockSpec((1, n // CHUNK, 2 * SUBLANES, CHUNK), lambda b, h: (b, 0, 0, 0))
    return pl.pallas_call(
        functools.partial(_mlstm_kernel, n_ctx=n_ctx, n_lat=n_lat, n_heads=n_heads, dqk=dqk),
        grid=(bsz, n_heads),
        in_specs=[cs, cs, cs, pl.BlockSpec((1, n_ctx, LANES), lambda b, h: (b, 0, 0)), row_spec(n_ctx),
                  ls, ls, ls, pl.BlockSpec((1, n_lat, LANES), lambda b, h: (b, 0, 0)), row_spec(n_lat),
                  ls,
                  pl.BlockSpec((1, LANES), lambda b, h: (0, h))],
        out_specs=ls,
        out_shape=jax.ShapeDtypeStruct((bsz, n_lat, n_heads * LANES), F32),
        scratch_shapes=[pltpu.VMEM((n_lat, LANES), F32),
                        pltpu.VMEM((2, (n_ctx + n_lat) // CHUNK, LANES, 2 * LANES), F32),
                        pltpu.VMEM((2, (n_ctx + n_lat) // CHUNK, SUBLANES, LANES), F32),
                        pltpu.VMEM((2, n_lat, 2 * LANES), F32),
                        pltpu.VMEM((2, n_lat, LANES), F32),
                        pltpu.VMEM((2, n_lat, LANES), F32)],
        compiler_params=_cparams(("arbitrary", "arbitrary")),
        name="mlstm_mixer",
    )(qc, kc, vc, gcc, grc, ql, kl, vl, gcl, grl, og, norm_w.reshape(1, -1))


def _pool_kernel(x_ref, w_ref, sc_ref, o_ref, *, seg):
    tm = x_ref.shape[0]
    r = _iota((tm, tm), 0)
    c = _iota((tm, tm), 1)
    sh = seg.bit_length() - 1
    same = lax.shift_right_logical(r, sh) == lax.shift_right_logical(c, sh)
    t = r & (seg - 1)
    s = c & (seg - 1)
    for gi, wsz in enumerate(POOL_SIZES):
        lo = jnp.maximum(t - wsz // 2, 0)
        hi = jnp.minimum(t - wsz // 2 + wsz, seg)
        inwin = same & (s >= lo) & (s < hi)
        pmat = jnp.where(inwin, 1.0 / (hi - lo).astype(F32), 0.0) - jnp.where(r == c, 1.0, 0.0)
        x = x_ref[:, gi * LANES:(gi + 1) * LANES]
        pooled = _mm3(pmat, x)
        y = _mm(pooled.astype(BF16), w_ref[gi].astype(BF16))
        o_ref[:, gi * LANES:(gi + 1) * LANES] = y * sc_ref[:, gi * LANES:(gi + 1) * LANES]


def _pool(x, w_grp, scale, seg):
    t, width = x.shape
    tm = TOK_BLK
    assert seg & (seg - 1) == 0 and tm % seg == 0
    return pl.pallas_call(
        functools.partial(_pool_kernel, seg=seg),
        grid=(t // tm,),
        in_specs=[pl.BlockSpec((tm, width), lambda i: (i, 0)),
                  pl.BlockSpec(w_grp.shape, lambda i: (0, 0, 0)),
                  pl.BlockSpec((1, width), lambda i: (0, 0))],
        out_specs=pl.BlockSpec((tm, width), lambda i: (i, 0)),
        out_shape=jax.ShapeDtypeStruct((t, width), F32),
        compiler_params=_cparams(("arbitrary",)),
        name="pool_mixer",
    )(x, w_grp, scale.reshape(1, width))


def _outproj_kernel(x_ref, ya_ref, yb_ref, wa_ref, wb_ref, g1_ref, gn_ref, sh_ref, sc_ref, xo_ref, h_ref):
    y = _mm(ya_ref[...].astype(BF16), wa_ref[...]) + _mm(yb_ref[...].astype(BF16), wb_ref[...])
    xn = x_ref[...] + g1_ref[0] * y
    xo_ref[...] = xn
    hn = xn * lax.rsqrt(jnp.mean(xn * xn, axis=-1, keepdims=True) + EPS) * gn_ref[...]
    h_ref[...] = (hn * (1.0 + sc_ref[0]) + sh_ref[0]).astype(BF16)


def _outproj(x, ya, yb, w_out, gain, tab, rowfn):
    t, d = x.shape
    wa_n = ya.shape[1]
    tm = TOK_BLK
    w = w_out.astype(BF16)
    mod = lambda k: pl.BlockSpec((1, 1, d), lambda i: (rowfn(i) + k, 0, 0))
    return pl.pallas_call(
        _outproj_kernel,
        grid=(t // tm,),
        in_specs=[pl.BlockSpec((tm, d), lambda i: (i, 0)),
                  pl.BlockSpec((tm, wa_n), lambda i: (i, 0)),
                  pl.BlockSpec((tm, yb.shape[1]), lambda i: (i, 0)),
                  pl.BlockSpec((wa_n, d), lambda i: (0, 0)),
                  pl.BlockSpec((yb.shape[1], d), lambda i: (0, 0)),
                  mod(2),
                  pl.BlockSpec((1, d), lambda i: (0, 0)),
                  mod(3), mod(4)],
        out_specs=[pl.BlockSpec((tm, d), lambda i: (i, 0)), pl.BlockSpec((tm, d), lambda i: (i, 0))],
        out_shape=[jax.ShapeDtypeStruct((t, d), F32), jax.ShapeDtypeStruct((t, d), BF16)],
        compiler_params=_cparams(("arbitrary",)),
        name="out_proj",
    )(x, ya, yb, w[:wa_n], w[wa_n:], tab, gain.reshape(1, d), tab, tab)


def _hyperbola_pairs():
    return [(j1, j2) for j1 in range(PEER_TOPK) for j2 in range(PEER_TOPK) if (j1 + 1) * (j2 + 1) <= PEER_TOPK]


def _topk_rows(s, k, break_ties):
    n, t = s.shape
    rowi = _iota((n, t), 0).astype(F32)
    rank = jnp.full((n, t), float(2 * k), F32)
    cur = s
    vals = []
    for j in range(k):
        m = jnp.max(cur, axis=0, keepdims=True)
        sel = cur == m
        if break_ties:
            sel = rowi == jnp.min(jnp.where(sel, rowi, float(n)), axis=0, keepdims=True)
        rank = jnp.where(sel, float(j), rank)
        cur = jnp.where(sel, -jnp.inf, cur)
        vals.append(m)
    return vals, rank


def _candidate_stage(v1, v2, break_ties):
    pairs = _hyperbola_pairs()
    cand = jnp.concatenate([v1[j1] + v2[j2] for j1, j2 in pairs], axis=0)
    _, crank = _topk_rows(cand, PEER_TOPK, break_ties)
    chosen = crank < PEER_TOPK
    zsum = jnp.sum(jnp.where(chosen, jnp.exp(cand - (v1[0] + v2[0])), 0.0), axis=0, keepdims=True)
    counts = jnp.where(chosen, 1.0, 0.0)
    n_by_rank = []
    row = 0
    for j1 in range(PEER_TOPK):
        width = PEER_TOPK // (j1 + 1)
        n_by_rank.append(jnp.sum(counts[row:row + width], axis=0, keepdims=True))
        row += width
    return n_by_rank, zsum, jnp.sum(counts, axis=0, keepdims=True)


def _route(s1, s2):
    v1, rank1 = _topk_rows(s1, PEER_TOPK, True)
    v2, rank2 = _topk_rows(s2, PEER_TOPK, True)
    n_by_rank, zsum, _ = _candidate_stage(v1, v2, True)
    n1 = jnp.zeros_like(rank1)
    for j1 in range(PEER_TOPK):
        n1 = jnp.where(rank1 == float(j1), n_by_rank[j1], n1)
    e2 = jnp.where(rank2 < PEER_TOPK, jnp.exp(s2 - v2[0]), 0.0)
    e1 = jnp.where(rank1 < PEER_TOPK, jnp.exp(s1 - v1[0]), 0.0) / zsum
    return rank2, e2, n1, e1


def _sorted_top(s):
    n = s.shape[0] // SUBLANES
    a = [s[SUBLANES * j:SUBLANES * (j + 1), :] for j in range(n)]

    def cex(i, l):
        a[i], a[l] = jnp.maximum(a[i], a[l]), jnp.minimum(a[i], a[l])

    k = 2
    while k <= n:
        j = k // 2
        while j >= 1:
            for i in range(n):
                l = i ^ j
                if l > i:
                    if (i & k) == 0:
                        cex(i, l)
                    else:
                        cex(l, i)
            j //= 2
        k *= 2
    for shift in (4, 2, 1):
        a = [jnp.maximum(a[j], pltpu.roll(a[n - 1 - j], shift, 0)) for j in range(n)]
        j = n // 2
        while j >= 1:
            for i in range(n):
                l = i ^ j
                if l > i:
                    cex(i, l)
            j //= 2
    return a


def _by_rank(bits, table):
    level = list(table)
    for g in reversed(bits):
        level = [jnp.where(g, level[2 * i + 1], level[2 * i]) for i in range(len(level) // 2)]
    return level[0]


def _rank_bits(x, v):
    g8 = v[7] > x
    g4 = jnp.where(g8, v[11], v[3]) > x
    g2 = _by_rank([g8, g4], [v[1], v[5], v[9], v[13]]) > x
    g1 = _by_rank([g8, g4, g2], [v[2 * i] for i in range(8)]) > x
    return [g8, g4, g2, g1]


def _route_untied(s1, s2):
    v1 = _sorted_top(s1)
    v2 = _sorted_top(s2)
    n_by_rank, zsum, _ = _candidate_stage([v[0:1, :] for v in v1], [v[0:1, :] for v in v2], True)
    shape = v1[0].shape
    n_tab = [jnp.broadcast_to(nj, shape) for nj in n_by_rank]
    inv_z = jnp.broadcast_to(1.0 / zsum, shape)
    rank2, e2, n1, e1 = [], [], [], []
    cnt1 = jnp.zeros(shape, F32)
    cnt2 = jnp.zeros(shape, F32)
    for r in range(s1.shape[0] // SUBLANES):
        x1 = s1[SUBLANES * r:SUBLANES * (r + 1), :]
        x2 = s2[SUBLANES * r:SUBLANES * (r + 1), :]
        in1 = x1 >= v1[PEER_TOPK - 1]
        in2 = x2 >= v2[PEER_TOPK - 1]
        n1.append(jnp.where(in1, _by_rank(_rank_bits(x1, v1), n_tab), 0.0))
        e1.append(jnp.where(in1, jnp.exp(x1 - v1[0]) * inv_z, 0.0))
        g8, g4, g2, g1 = _rank_bits(x2, v2)
        rk = (jnp.where(g8, 8.0, 0.0) + jnp.where(g4, 4.0, 0.0)) + (jnp.where(g2, 2.0, 0.0) + jnp.where(g1, 1.0, 0.0))
        rank2.append(jnp.where(in2, rk, float(2 * PEER_TOPK)))
        e2.append(jnp.where(in2, jnp.exp(x2 - v2[0]), 0.0))
        cnt1 = cnt1 + jnp.where(in1, 1.0, 0.0)
        cnt2 = cnt2 + jnp.where(in2, 1.0, 0.0)
    tied = jnp.zeros((1, shape[1]), F32)
    for cnt in (cnt1, cnt2):
        tied = tied + jnp.abs(jnp.sum(cnt, axis=0, keepdims=True) - float(PEER_TOPK))
    for v in (v1, v2):
        for j in range(PEER_TOPK - 1):
            tied = tied + jnp.where(v[j][0:1, :] == v[j + 1][0:1, :], 1.0, 0.0)
    cat = lambda parts: jnp.concatenate(parts, axis=0)
    return cat(rank2), cat(e2), cat(n1), cat(e1), tied


def _peer_select_kernel(h_ref, wq_ref, keys_ref, r2_ref, e2_ref, n1_ref, e1_ref):
    q = _mm(h_ref[...], wq_ref[0])
    s1 = _nt(keys_ref[0, 0], q[:, :N_KEYS], HI)
    s2 = _nt(keys_ref[0, 1], q[:, N_KEYS:], HI)

    def emit(rank2, e2, n1, e1):
        r2_ref[0] = rank2.astype(r2_ref.dtype)
        e2_ref[0] = e2.astype(e2_ref.dtype)
        n1_ref[0] = n1
        e1_ref[0] = e1

    rank2, e2, n1, e1, tied = _route_untied(s1, s2)
    emit(rank2, e2, n1, e1)

    @pl.when(jnp.max(tied) > 0.0)
    def _():
        emit(*_route(s1, s2))


def _peer_select(h, wq, keys):
    t, d = h.shape
    nh = keys.shape[0]
    qd = wq.shape[2]
    tm = PEER_BLK
    ospec = pl.BlockSpec((1, N_KEYS, tm), lambda i, hh: (hh, 0, i))
    return pl.pallas_call(
        _peer_select_kernel,
        grid=(t // tm, nh),
        in_specs=[pl.BlockSpec((tm, d), lambda i, hh: (i, 0)),
                  pl.BlockSpec((1, d, qd), lambda i, hh: (hh, 0, 0)),
                  pl.BlockSpec((1, 2, N_KEYS, qd // 2), lambda i, hh: (hh, 0, 0, 0))],
        out_specs=[ospec, ospec, ospec, ospec],
        out_shape=[jax.ShapeDtypeStruct((nh, N_KEYS, t), dt) for dt in (BF16, BF16, F32, F32)],
        compiler_params=_cparams(("arbitrary", "arbitrary")),
        name="peer_select",
    )(h, wq, keys)


def _peer_dense_kernel(h_ref, u_ref, vt_ref, r2_ref, e2_ref, n1_ref, e1_ref, x_ref, g2a_ref, g2b_ref, fn_ref, o_ref,
                       acc_ref, wact_ref, *, n_heads, final):
    eb = pl.program_id(1)
    n_eb = pl.num_programs(1) - 1
    cur = eb & 1
    prev = 1 - cur

    @pl.when(eb == 0)
    def _():
        acc_ref[...] = jnp.zeros_like(acc_ref)
        wact_ref[1] = jnp.zeros(wact_ref.shape[1:], BF16)

    @pl.when(eb < n_eb)
    def _():
        hb = h_ref[...]
        part_rows = 2 * N_KEYS
        n_part = EXP_BLK // part_rows
        acts = [_nt(u_ref[p * part_rows:(p + 1) * part_rows, :], hb) for p in range(n_part)]
        upd = _mm(vt_ref[0], wact_ref[prev])
        rows_per = EXP_BLK // N_KEYS
        i1_base = pl.multiple_of(eb * rows_per, rows_per)
        n1_tiles = [n1_ref[hh, pl.ds(i1_base, rows_per), :] for hh in range(n_heads)]
        e1_tiles = [e1_ref[hh, pl.ds(i1_base, rows_per), :] for hh in range(n_heads)]
        zero = jnp.zeros((), BF16)
        for r in range(rows_per):
            wgt = None
            for hh in range(n_heads):
                shape = r2_ref.shape[1:]
                n1row = jnp.broadcast_to(n1_tiles[hh][r:r + 1, :], shape).astype(BF16)
                e1row = jnp.broadcast_to(e1_tiles[hh][r:r + 1, :], shape).astype(BF16)
                term = jnp.where(r2_ref[hh] < n1row, e2_ref[hh], zero) * e1row
                wgt = term if wgt is None else wgt + term
            off = (r * N_KEYS) % part_rows
            a = acts[(r * N_KEYS) // part_rows][off:off + N_KEYS, :]
            wact_ref[cur, r * N_KEYS:(r + 1) * N_KEYS, :] = _gelu_tanh(a.astype(BF16)) * wgt
        acc_ref[...] += upd

    @pl.when(eb == n_eb)
    def _():
        acc_ref[...] += _mm(vt_ref[0], wact_ref[prev])
        for part, g2_ref in enumerate((g2a_ref, g2b_ref)):
            rows = slice(part * TOK_BLK, (part + 1) * TOK_BLK)
            y = x_ref[rows, :] + g2_ref[0] * acc_ref[:, rows].T
            if final:
                y = y * lax.rsqrt(jnp.mean(y * y, axis=-1, keepdims=True) + EPS) * fn_ref[...]
            o_ref[rows, :] = y


def _peer_dense(h, u, vt, r2, e2, n1, e1, x, tab, rowfn, final_gain, final):
    t, d = h.shape
    n_eb = u.shape[0] // EXP_BLK
    nh = r2.shape[0]
    tm = PEER_BLK
    sel = pl.BlockSpec((nh, N_KEYS, tm), lambda i, e: (0, 0, i))
    g2 = lambda part: pl.BlockSpec((1, 1, d), lambda i, e: (rowfn(2 * i + part) + 5, 0, 0))
    return pl.pallas_call(
        functools.partial(_peer_dense_kernel, n_heads=nh, final=final),
        grid=(t // tm, n_eb + 1),
        in_specs=[pl.BlockSpec((tm, d), lambda i, e: (i, 0)),
                  pl.BlockSpec((EXP_BLK, d), lambda i, e: (jnp.minimum(e, n_eb - 1), 0)),
                  pl.BlockSpec((1, d, EXP_BLK), lambda i, e: (jnp.maximum(e - 1, 0), 0, 0)),
                  sel, sel, sel, sel,
                  pl.BlockSpec((tm, d), lambda i, e: (i, 0)),
                  g2(0), g2(1),
                  pl.BlockSpec((1, d), lambda i, e: (0, 0))],
        out_specs=pl.BlockSpec((tm, d), lambda i, e: (i, 0)),
        out_shape=jax.ShapeDtypeStruct((t, d), F32),
        scratch_shapes=[pltpu.VMEM((d, tm), F32), pltpu.VMEM((2, EXP_BLK, tm), BF16)],
        compiler_params=_cparams(("arbitrary", "arbitrary")),
        name="peer_dense",
    )(h, u, vt, r2, e2, n1, e1, x, tab, tab, final_gain.reshape(1, d))


def _peer(h, x, wq, keys, u_tab, v_tab, tab, rowfn, final_gain, final):
    d = h.shape[1]
    nh = keys.shape[0]
    wq_h = wq.reshape(d, nh, -1).transpose(1, 0, 2).astype(BF16)
    vt = v_tab.reshape(-1, EXP_BLK, d).transpose(0, 2, 1).astype(BF16)
    r2, e2, n1, e1 = _peer_select(h, wq_h, keys)
    return _peer_dense(h, u_tab.astype(BF16), vt, r2, e2, n1, e1, x, tab, rowfn, final_gain, final)


def _lane_vec(vals, rows=SUBLANES):
    out = jnp.zeros((rows, LANES), F32)
    for r, v in enumerate(vals):
        out = out.at[r, :v.shape[0]].set(v.astype(F32))
    return out


def _block_diag_groups(w):
    two, nb, bd, _ = w.shape
    per = LANES // bd
    wg = w.reshape(two, nb // per, per, bd, bd)
    eye = jnp.eye(per, dtype=w.dtype)
    return jnp.einsum("dgpij,pq->dgpiqj", wg, eye).reshape(two, nb // per, LANES, LANES)


def kernel(x, c, ctx, c_ctx, ada_w, ada_b, norm_mix, norm_ffn, final_norm, peer_wq, peer_keys, peer_u, peer_v,
           ev_w_in, ev_w_out, a_conv, a_alog, a_dtb, a_norm, b_conv_w, b_conv_b, b_wa, b_ba, b_wx, b_bx, b_lam,
           od_w_in, od_w_out, c_ibias, c_fbias, c_norm, d_w, d_scale):
    bsz, seq, dm = x.shape
    n_ctx = ctx.shape[1]
    L = n_ctx + seq
    rows = seq // GRID_W
    blk_per = L // TOK_BLK

    tab0 = _mod_table(c, c_ctx, ada_w[0], ada_b[0])
    row0 = lambda i: ((i // blk_per) * 2 + jnp.minimum(i % blk_per, 1)) * 6
    xcat = jnp.concatenate([ctx, x], axis=1).reshape(bsz * L, dm)

    a_heads = a_alog.shape[-1]
    a_width = a_heads * LANES
    a_qkv = 3 * a_width
    b_width = b_lam.shape[-1] * b_lam.shape[-2]
    w_in = ev_w_in[0]
    a_cols = a_qkv + a_width + 4 * a_heads
    w_ab = jnp.pad(w_in[:, a_qkv + a_width:a_cols], ((0, 0), (0, LANES - 4 * a_heads)))
    w0 = jnp.concatenate([w_in[:, :a_qkv], w_in[:, a_qkv:a_qkv + a_width], w_in[:, a_cols:a_cols + b_width],
                          w_in[:, a_cols + b_width:], w_ab], axis=1).astype(BF16)
    qkv, z, xb, gate, ab = _norm_mod_proj(xcat, norm_mix[0], tab0, row0, w0, (a_qkv, a_width, b_width, b_width, LANES))

    gparams = _lane_vec([a_alog[0].reshape(-1), a_dtb[0].reshape(-1)])
    ya = _gdn(qkv.reshape(bsz, L, a_qkv), z.reshape(bsz, L, a_width), ab.reshape(bsz, L, LANES),
              a_conv[0], gparams, a_norm[0], n_ctx, seq, a_heads)

    pvec = jnp.zeros((2, SUBLANES, b_width), F32)
    pvec = pvec.at[:, 0].set(b_ba[0].reshape(2, b_width)).at[:, 1].set(b_bx[0].reshape(2, b_width))
    pvec = pvec.at[:, 2].set(b_lam[0].reshape(2, b_width))
    yb = _lru(xb.reshape(bsz, L, b_width), gate.reshape(bsz, L, b_width), b_conv_w[0], b_conv_b[0],
              _block_diag_groups(b_wa[0]).astype(BF16), _block_diag_groups(b_wx[0]).astype(BF16), pvec, n_ctx, seq)

    x1, h1 = _outproj(xcat, ya.reshape(bsz * L, a_width), yb.reshape(bsz * L, b_width), ev_w_out[0], norm_ffn[0],
                      tab0, row0)
    x2 = _peer(h1, x1, peer_wq[0], peer_keys[0], peer_u[0], peer_v[0], tab0, row0, final_norm, False)
    x2 = x2.reshape(bsz, L, dm)

    tab1 = _mod_table(c, c_ctx, ada_w[1], ada_b[1])
    xc = x2[:, :n_ctx].reshape(bsz * n_ctx, dm)
    xl = x2[:, n_ctx:].reshape(bsz, rows, GRID_W, dm).transpose(0, 2, 1, 3).reshape(bsz * seq, dm)
    lat_per = seq // TOK_BLK
    ctx_per = n_ctx // TOK_BLK
    row_lat = lambda i: ((i // lat_per) * 2 + 1) * 6
    row_ctx = lambda i: ((i // ctx_per) * 2) * 6

    c_heads = c_ibias.shape[-1]
    c_width = c_norm.shape[-1]
    d_width = d_scale.shape[-1]
    w_in1 = od_w_in[0]
    n_state = w_in1.shape[1] - c_width - d_width
    c_qk = (n_state - c_width - 4 * c_heads) // 2
    dqk = c_qk // c_heads

    def pad_heads(wcols):
        return jnp.pad(wcols.reshape(dm, c_heads, dqk), ((0, 0), (0, 0), (0, LANES - dqk))).reshape(dm, c_heads * LANES)

    w_gate = jnp.pad(w_in1[:, 2 * c_qk + c_width:n_state], ((0, 0), (0, LANES - 4 * c_heads)))
    w1 = jnp.concatenate([pad_heads(w_in1[:, :c_qk]), pad_heads(w_in1[:, c_qk:2 * c_qk]),
                          w_in1[:, 2 * c_qk:2 * c_qk + c_width], w_in1[:, n_state:n_state + c_width],
                          w_in1[:, n_state + c_width:], w_gate], axis=1).astype(BF16)
    hw = c_heads * LANES
    widths1 = (hw, hw, c_width, c_width, d_width, LANES)
    ql, kl, vl, og, dl, gl = _norm_mod_proj(xl, norm_mix[1], tab1, row_lat, w1, widths1)
    qc, kc, vc, _, _, gc = _norm_mod_proj(xc, norm_mix[1], tab1, row_ctx, w1, widths1)

    gparams1 = _lane_vec([jnp.concatenate([c_ibias[0].reshape(-1), jnp.zeros((2 * c_heads,), F32)]),
                          jnp.concatenate([jnp.zeros((2 * c_heads,), F32), c_fbias[0].reshape(-1)])])
    r3 = lambda a, n: a.reshape(bsz, n, a.shape[-1])
    yc1 = _mlstm(r3(qc, n_ctx), r3(kc, n_ctx), r3(vc, n_ctx), r3(gc, n_ctx),
                 r3(ql, seq), r3(kl, seq), r3(vl, seq), r3(gl, seq), r3(og, seq), gparams1, c_norm[0], c_heads, dqk)
    yd1 = _pool(dl, d_w[0], d_scale[0], rows)

    x3, h3 = _outproj(xl, yc1.reshape(bsz * seq, c_width), yd1, od_w_out[0], norm_ffn[1], tab1, row_lat)
    out = _peer(h3, x3, peer_wq[1], peer_keys[1], peer_u[1], peer_v[1], tab1, row_lat, final_norm, True)
    return out.reshape(bsz, GRID_W, rows, dm).transpose(0, 2, 1, 3).reshape(bsz, seq, dm)
```

```python
import functools
import math

import jax
import jax.numpy as jnp
import numpy as np
from jax import lax
from jax.experimental import pallas as pl
from jax.experimental.pallas import tpu as pltpu

F32 = jnp.float32
BF16 = jnp.bfloat16
HI = lax.Precision.HIGHEST

EPS = 1e-6
GRID_W = 64
CHUNK = 64
LANES = 128
SUBLANES = 8
HALO = 8
CONV_W = 4
LRU_C = 8.0
PEER_TOPK = 16
N_KEYS = 128
POOL_SIZES = (2, 4, 8, 16)
TOK_BLK = 256
PEER_BLK = 2 * TOK_BLK
ROW_BLK = 256
EXP_BLK = 2048
SELECT_HEADS = 2
NEG_BIG = -1e30
VMEM_LIMIT = 56 * 1024 * 1024


def _cparams(sem):
    return pltpu.CompilerParams(dimension_semantics=sem, vmem_limit_bytes=VMEM_LIMIT)


def _nt(a, b, precision=None):
    return lax.dot_general(a, b, (((1,), (1,)), ((), ())), precision=precision, preferred_element_type=F32)


def _tn(a, b, precision=None):
    return lax.dot_general(a, b, (((0,), (0,)), ((), ())), precision=precision, preferred_element_type=F32)


def _mm(a, b, precision=None):
    return jnp.dot(a, b, precision=precision, preferred_element_type=F32)


def _silu(x):
    return x * jax.nn.sigmoid(x)


def _softplus(x):
    return jnp.maximum(x, 0.0) + jnp.log1p(jnp.exp(-jnp.abs(x)))


def _gelu_tanh(x):
    return 0.5 * x * (1.0 + jnp.tanh(math.sqrt(2.0 / math.pi) * (x + 0.044715 * (x * x * x))))


def _iota(shape, dim):
    return lax.broadcasted_iota(jnp.int32, shape, dim)


def _ada_kernel(c_ref, w_ref, b_ref, o_ref):
    o_ref[...] = _mm(_silu(c_ref[...]), w_ref[...], HI) + b_ref[...]


def _ada(cc, w, b):
    r, d = cc.shape
    n = w.shape[1]
    tn = 1536
    return pl.pallas_call(
        _ada_kernel,
        grid=(n // tn,),
        in_specs=[pl.BlockSpec((r, d), lambda j: (0, 0)),
                  pl.BlockSpec((d, tn), lambda j: (0, j)),
                  pl.BlockSpec((1, tn), lambda j: (0, j))],
        out_specs=pl.BlockSpec((r, tn), lambda j: (0, j)),
        out_shape=jax.ShapeDtypeStruct((r, n), F32),
        compiler_params=_cparams(("arbitrary",)),
        name="ada_mod",
    )(cc, w, b.reshape(1, n))


def _mod_table(c, c_ctx, w, b):
    bsz, d = c.shape
    rows = ((bsz + 1 + SUBLANES - 1) // SUBLANES) * SUBLANES
    cc = jnp.zeros((rows, d), F32).at[:bsz].set(c).at[bsz].set(c_ctx)
    m = _ada(cc, w, b).reshape(rows, 6, d)
    tab = jnp.stack([jnp.broadcast_to(m[bsz], (bsz, 6, d)), m[:bsz]], axis=1)
    return tab.reshape(bsz * 2 * 6, 1, d)


def _nmm_kernel(x_ref, g_ref, sh_ref, sc_ref, w_ref, *o_refs, widths):
    x = x_ref[...]
    y = x * lax.rsqrt(jnp.mean(x * x, axis=-1, keepdims=True) + EPS) * g_ref[...]
    h = (y * (1.0 + sc_ref[0]) + sh_ref[0]).astype(BF16)
    off = 0
    for o_ref, n in zip(o_refs, widths):
        o_ref[...] = _mm(h, w_ref[:, off:off + n])
        off += n


def _norm_mod_proj(x, gain, tab, rowfn, w, widths):
    t, d = x.shape
    n = w.shape[1]
    tm = TOK_BLK
    return pl.pallas_call(
        functools.partial(_nmm_kernel, widths=widths),
        grid=(t // tm,),
        in_specs=[pl.BlockSpec((tm, d), lambda i: (i, 0)),
                  pl.BlockSpec((1, d), lambda i: (0, 0)),
                  pl.BlockSpec((1, 1, d), lambda i: (rowfn(i), 0, 0)),
                  pl.BlockSpec((1, 1, d), lambda i: (rowfn(i) + 1, 0, 0)),
                  pl.BlockSpec((d, n), lambda i: (0, 0))],
        out_specs=[pl.BlockSpec((tm, wd), lambda i: (i, 0)) for wd in widths],
        out_shape=[jax.ShapeDtypeStruct((t, wd), F32) for wd in widths],
        compiler_params=_cparams(("arbitrary",)),
        name="norm_mod_proj",
    )(x, gain.reshape(1, d), tab, tab, w)


def _conv_chunk(pad_ref, pbase, w, rows=CHUNK):
    blk = pad_ref[pl.ds(pbase - HALO, rows + 2 * HALO), :]
    n = rows + 2 * HALO
    lo, hi = HALO, HALO + rows
    xm1 = pltpu.roll(blk, 1, 0)[lo:hi]
    x0 = blk[lo:hi]
    xp1 = pltpu.roll(blk, n - 1, 0)[lo:hi]
    xp2 = pltpu.roll(blk, n - 2, 0)[lo:hi]
    return w[0:1] * xm1 + w[1:2] * x0 + w[2:3] * xp1 + w[3:4] * xp2


def _fill_padded(pad_ref, src_ref, n_ctx, n_lat):
    width = pad_ref.shape[1]
    z = jnp.zeros((HALO, width), F32)
    pad_ref[0:HALO, :] = z
    pad_ref[HALO + n_ctx:2 * HALO + n_ctx, :] = z
    pad_ref[2 * HALO + n_ctx + n_lat:3 * HALO + n_ctx + n_lat, :] = z
    pad_ref[HALO:HALO + n_ctx, :] = src_ref[0, 0:n_ctx, :]
    pad_ref[2 * HALO + n_ctx:2 * HALO + n_ctx + n_lat, :] = src_ref[0, n_ctx:n_ctx + n_lat, :]


def _padded_base(c, ncc, rows=CHUNK):
    return pl.multiple_of(HALO + c * rows + jnp.where(c >= ncc, HALO, 0), SUBLANES)


def _lane_col(g, lane_idx):
    lane = _iota(g.shape, 1)
    return jnp.sum(jnp.where(lane == lane_idx, g, 0.0), axis=1, keepdims=True)


def _tri(d):
    r = _iota((CHUNK, CHUNK), 0)
    c = _iota((CHUNK, CHUNK), 1)
    incl = (r >= c) if d == 0 else (r <= c)
    strict = (r > c) if d == 0 else (r < c)
    return incl, strict


def _bwd_chunk(s, ncc, nct):
    return jnp.where(s < ncc, ncc - 1 - s, nct + ncc - 1 - s)


def _split3(x):
    x1 = x.astype(BF16)
    r1 = x - x1.astype(F32)
    x2 = r1.astype(BF16)
    x3 = (r1 - x2.astype(F32)).astype(BF16)
    return x1, x2, x3


def _mask_mm(mask, x):
    x1, x2, x3 = _split3(x)
    return _mm(mask, x1) + (_mm(mask, x2) + _mm(mask, x3))


def _mask_tn(x, mask):
    x1, x2, x3 = _split3(x)
    return _tn(x1, mask) + (_tn(x2, mask) + _tn(x3, mask))


def _gate_kernel(ab_ref, gp_ref, gc_ref, gr_ref, *, n_chunks, n_heads, kind):
    r = _iota((CHUNK, CHUNK), 0)
    c = _iota((CHUNK, CHUNK), 1)
    lower = (r >= c).astype(BF16)
    upper = (r <= c).astype(BF16)
    eye = (r == c).astype(BF16)
    lane = _iota((CHUNK, LANES), 1)
    rowg = _iota((LANES, CHUNK), 0)
    if kind == "gdn":
        fwd_lo, rev_lo, rev_hi = 0, n_heads, 2 * n_heads
    else:
        fwd_lo, rev_lo, rev_hi = 2 * n_heads, 3 * n_heads, 4 * n_heads
    p0 = gp_ref[0:1, :]
    p1 = gp_ref[1:2, :]

    def body(ci, carry):
        r0 = pl.multiple_of(ci * CHUNK, CHUNK)
        raw = ab_ref[0, pl.ds(r0, CHUNK), :]
        if kind == "gdn":
            g = jnp.where(lane < 2 * n_heads, -jnp.exp(p0) * _softplus(raw + p1), jax.nn.sigmoid(raw))
        else:
            g = jnp.where(lane < 2 * n_heads, raw + p0, -_softplus(-(raw + p1)))
        is_fwd = (lane >= fwd_lo) & (lane < rev_lo)
        is_rev = (lane >= rev_lo) & (lane < rev_hi)
        gc_ref[0, pl.ds(r0, CHUNK), :] = jnp.where(is_fwd, _mask_mm(lower, g), jnp.where(is_rev, _mask_mm(upper, g), g))
        row_fwd = (rowg >= fwd_lo) & (rowg < rev_lo)
        row_rev = (rowg >= rev_lo) & (rowg < rev_hi)
        rows = jnp.where(row_fwd, _mask_tn(g, upper), jnp.where(row_rev, _mask_tn(g, lower), _mask_tn(g, eye)))
        gr_ref[0, ci] = rows[0:2 * SUBLANES, :]
        return carry

    lax.fori_loop(0, n_chunks, body, 0)


def _gates(ab, gparams, n_heads, kind):
    bsz, L, _ = ab.shape
    nct = L // CHUNK
    assert 4 * n_heads <= 2 * SUBLANES
    return pl.pallas_call(
        functools.partial(_gate_kernel, n_chunks=nct, n_heads=n_heads, kind=kind),
        grid=(bsz,),
        in_specs=[pl.BlockSpec((1, L, LANES), lambda b: (b, 0, 0)),
                  pl.BlockSpec((SUBLANES, LANES), lambda b: (0, 0))],
        out_specs=[pl.BlockSpec((1, L, LANES), lambda b: (b, 0, 0)),
                   pl.BlockSpec((1, nct, 2 * SUBLANES, CHUNK), lambda b: (b, 0, 0, 0))],
        out_shape=[jax.ShapeDtypeStruct((bsz, L, LANES), F32),
                   jax.ShapeDtypeStruct((bsz, nct, 2 * SUBLANES, CHUNK), F32)],
        compiler_params=_cparams(("arbitrary",)),
        name=kind + "_gates",
    )(ab, gparams)


def _mm3(a, b):
    ah = a.astype(BF16)
    al = (a - ah.astype(F32)).astype(BF16)
    bh = b.astype(BF16)
    bl = (b - bh.astype(F32)).astype(BF16)
    return _mm(ah, bh) + (_mm(ah, bl) + _mm(al, bh))


def _gdn_kernel(q_ref, k_ref, v_ref, z_ref, gc_ref, gr_ref, cwq_ref, cwk_ref, cwv_ref, nw_ref, o_ref,
                pad_ref, qs_ref, ks_ref, vs_ref, acc_ref, u_ref, w_ref, qd_ref, kd_ref, qk_ref, gl_ref,
                *, n_ctx, n_lat, n_heads):
    h = pl.program_id(1)
    ncc = n_ctx // CHUNK
    nct = (n_ctx + n_lat) // CHUNK
    dk = qs_ref.shape[1]

    def prep(src_ref, cw_ref, dst_ref, mode):
        _fill_padded(pad_ref, src_ref, n_ctx, n_lat)
        w = cw_ref[...]

        def body(c, carry):
            y = _silu(_conv_chunk(pad_ref, _padded_base(c, n_ctx // ROW_BLK, ROW_BLK), w, ROW_BLK))
            if mode != "v":
                y = y * lax.rsqrt(jnp.sum(y * y, axis=-1, keepdims=True) + EPS)
            if mode == "q":
                y = y * (dk ** -0.5)
            dst_ref[pl.ds(pl.multiple_of(c * ROW_BLK, ROW_BLK), ROW_BLK), :] = y
            return carry

        lax.fori_loop(0, (n_ctx + n_lat) // ROW_BLK, body, 0)

    prep(q_ref, cwq_ref, qs_ref, "q")
    prep(k_ref, cwk_ref, ks_ref, "k")
    prep(v_ref, cwv_ref, vs_ref, "v")

    n_chain = 4
    big = n_chain * CHUNK
    rr = _iota((big, big), 0)
    cc = _iota((big, big), 1)
    log_chunk = CHUNK.bit_length() - 1
    same_blk = lax.shift_right_logical(rr, log_chunk) == lax.shift_right_logical(cc, log_chunk)
    bwd_blk = (lax.shift_right_logical(rr, log_chunk) & 1) == 1
    fwd_blk = jnp.logical_not(bwd_blk)
    strict_bd = same_blk & ((bwd_blk & (rr < cc)) | (fwd_blk & (rr > cc)))
    incl_bd = same_blk & ((bwd_blk & (rr <= cc)) | (fwd_blk & (rr >= cc)))

    def group_setup(c0):
        ks, rhs, gcols, bcols, grows = [], [], [], [], []
        chains = []
        for j in range(2):
            c = c0 + j
            r0 = pl.multiple_of(c * CHUNK, CHUNK)
            acc_ref[pl.ds(r0, CHUNK), :] = jnp.zeros((CHUNK, dk), F32)
            q = qs_ref[pl.ds(r0, CHUNK), :]
            k = ks_ref[pl.ds(r0, CHUNK), :]
            v = vs_ref[pl.ds(r0, CHUNK), :]
            gates = gc_ref[0, pl.ds(r0, CHUNK), :]
            grt = gr_ref[0, c]
            for d in range(2):
                gcol = _lane_col(gates, d * n_heads + h)
                bcol = _lane_col(gates, 2 * n_heads + d * n_heads + h)
                grow = jnp.sum(jnp.where(_iota(grt.shape, 0) == d * n_heads + h, grt, 0.0), axis=0, keepdims=True)
                gtot = gcol[CHUNK - 1:CHUNK, :] if d == 0 else gcol[0:1, :]
                eg = jnp.exp(gcol)
                ks.append(k)
                gcols.append(gcol)
                bcols.append(bcol)
                grows.append(grow)
                rhs.append(jnp.concatenate([bcol * v, (bcol * eg) * k], axis=1))
                chains.append((c, d, r0, q, k, gcol, grow, gtot, eg))
        kst = jnp.concatenate(ks, axis=0).astype(BF16)
        gam = jnp.where(incl_bd, jnp.exp(jnp.minimum(jnp.concatenate(gcols, axis=0) - jnp.concatenate(grows, axis=1),
                                                      0.0)), 0.0)
        m_bd = jnp.where(strict_bd, jnp.concatenate(bcols, axis=0) * _nt(kst, kst) * gam, 0.0)
        return chains, m_bd, jnp.concatenate(rhs, axis=0)

    def group_store(chains, sol):
        for b, (c, d, r0, q, k, gcol, grow, gtot, eg) in enumerate(chains):
            incl, _ = _tri(d)
            gamma = jnp.where(incl, jnp.exp(jnp.minimum(gcol - grow, 0.0)), 0.0)
            u_ref[d, pl.ds(r0, CHUNK), :] = sol[b * CHUNK:(b + 1) * CHUNK, :dk]
            w_ref[d, pl.ds(r0, CHUNK), :] = sol[b * CHUNK:(b + 1) * CHUNK, dk:].astype(BF16)
            qd_ref[d, pl.ds(r0, CHUNK), :] = (q * eg).astype(BF16)
            kd_ref[d, pl.ds(r0, CHUNK), :] = (k * jnp.exp(gtot - gcol)).astype(BF16)
            qk_ref[d, c] = (_nt(q.astype(BF16), k.astype(BF16)) * gamma).astype(BF16)
            gl_ref[d, c] = jnp.broadcast_to(jnp.exp(gtot), (SUBLANES, LANES))

    def prepare_body(c4, carry):
        groups = [group_setup(4 * c4), group_setup(4 * c4 + 2)]
        idx = range(len(groups))
        ms = [g[1] for g in groups]
        sols = [g[2] for g in groups]
        sols = [sols[i] - _mm3(ms[i], sols[i]) for i in idx]
        ps = [m.astype(BF16) for m in ms]
        for it in range(5):
            ps = [_mm(p, p) for p in ps]
            ps = [p.astype(BF16) for p in ps]
            sols = [sols[i] + _mm(ps[i], sols[i].astype(BF16)) for i in idx]
        for i in idx:
            group_store(groups[i][0], sols[i])
        return carry

    lax.fori_loop(0, nct // 4, prepare_body, 0)

    def step(s, carry):
        cs = (s, _bwd_chunk(s, ncc, nct))
        r0 = [pl.multiple_of(c * CHUNK, CHUNK) for c in cs]
        sb = [st.astype(BF16) for st in carry]
        ws = [_mm(w_ref[d, pl.ds(r0[d], CHUNK), :], sb[d]) for d in range(2)]
        qs = [_mm(qd_ref[d, pl.ds(r0[d], CHUNK), :], sb[d]) for d in range(2)]
        vb = [(u_ref[d, pl.ds(r0[d], CHUNK), :] - ws[d]).astype(BF16) for d in range(2)]
        kv = [_tn(kd_ref[d, pl.ds(r0[d], CHUNK), :], vb[d]) for d in range(2)]
        ov = [_mm(qk_ref[d, cs[d]], vb[d]) for d in range(2)]
        for d in range(2):
            acc_ref[pl.ds(r0[d], CHUNK), :] += qs[d] + ov[d]
        return tuple(carry[d] * gl_ref[d, cs[d]][0:1, :] + kv[d] for d in range(2))

    zero = jnp.zeros((dk, dk), F32)
    lax.fori_loop(0, nct, step, (zero, zero))

    nw = nw_ref[...]

    def out_body(c, carry):
        r0 = pl.multiple_of(c * ROW_BLK, ROW_BLK)
        o = acc_ref[pl.ds(r0, ROW_BLK), :]
        y = o * lax.rsqrt(jnp.mean(o * o, axis=-1, keepdims=True) + EPS) * nw
        o_ref[0, pl.ds(r0, ROW_BLK), :] = y * _silu(z_ref[0, pl.ds(r0, ROW_BLK), :])
        return carry

    lax.fori_loop(0, (n_ctx + n_lat) // ROW_BLK, out_body, 0)


def _gdn(qkv, z, ab, conv_w, gparams, norm_w, n_ctx, n_lat, n_heads):
    bsz, L, _ = qkv.shape
    dk = LANES
    seq = lambda off: pl.BlockSpec((1, L, dk), lambda b, h: (b, 0, off + h))
    cw = lambda off: pl.BlockSpec((CONV_W, dk), lambda b, h: (0, off + h))
    lp = L + 3 * HALO
    nct = L // CHUNK
    assert nct % 4 == 0 and n_ctx % ROW_BLK == 0 and n_lat % ROW_BLK == 0
    gc, gr = _gates(ab, gparams, n_heads, "gdn")
    return pl.pallas_call(
        functools.partial(_gdn_kernel, n_ctx=n_ctx, n_lat=n_lat, n_heads=n_heads),
        grid=(bsz, n_heads),
        in_specs=[seq(0), seq(n_heads), seq(2 * n_heads),
                  pl.BlockSpec((1, L, dk), lambda b, h: (b, 0, h)),
                  pl.BlockSpec((1, L, LANES), lambda b, h: (b, 0, 0)),
                  pl.BlockSpec((1, nct, 2 * SUBLANES, CHUNK), lambda b, h: (b, 0, 0, 0)),
                  cw(0), cw(n_heads), cw(2 * n_heads),
                  pl.BlockSpec((1, dk), lambda b, h: (0, 0))],
        out_specs=pl.BlockSpec((1, L, dk), lambda b, h: (b, 0, h)),
        out_shape=jax.ShapeDtypeStruct((bsz, L, n_heads * dk), F32),
        scratch_shapes=[pltpu.VMEM((lp, dk), F32)] + [pltpu.VMEM((L, dk), F32) for _ in range(4)]
        + [pltpu.VMEM((2, L, dk), F32)] + [pltpu.VMEM((2, L, dk), BF16) for _ in range(3)]
        + [pltpu.VMEM((2, nct, CHUNK, CHUNK), BF16), pltpu.VMEM((2, nct, SUBLANES, LANES), F32)],
        compiler_params=_cparams(("arbitrary", "arbitrary")),
        name="gdn_mixer",
    )(qkv, qkv, qkv, z, gc, gr, conv_w, conv_w, conv_w, norm_w.reshape(1, dk))


def _lru_kernel(x_ref, gt_ref, cw_ref, cb_ref, wa_ref, wx_ref, pv_ref, o_ref,
                pad_ref, xc_ref, a_ref, b_ref, acc_ref, *, n_ctx, n_lat):
    ncc = n_ctx // CHUNK
    nct = (n_ctx + n_lat) // CHUNK
    L = n_ctx + n_lat
    width = xc_ref.shape[1]
    _fill_padded(pad_ref, x_ref, n_ctx, n_lat)
    w = cw_ref[...]
    cb = cb_ref[...]

    def conv_body(c, carry):
        r0 = pl.multiple_of(c * CHUNK, CHUNK)
        xc_ref[pl.ds(r0, CHUNK), :] = _conv_chunk(pad_ref, _padded_base(c, ncc), w) + cb
        acc_ref[pl.ds(r0, CHUNK), :] = jnp.zeros((CHUNK, width), F32)
        return carry

    lax.fori_loop(0, nct, conv_body, 0)

    row = _iota((SUBLANES, width), 0)
    bias_a = [pv_ref[d, 0:1, :] for d in range(2)]
    bias_x = [pv_ref[d, 1:2, :] for d in range(2)]
    decay = [-LRU_C * _softplus(-pv_ref[d, 2:3, :]) for d in range(2)]

    def coef_body(c, carry):
        r0 = pl.multiple_of(c * ROW_BLK, ROW_BLK)
        x = xc_ref[pl.ds(r0, ROW_BLK), :]
        xb = x.astype(BF16)
        for d in range(2):
            r = jax.nn.sigmoid(_mm(xb, wa_ref[d, 0]) + bias_a[d])
            i = jax.nn.sigmoid(_mm(xb, wx_ref[d, 0]) + bias_x[d])
            log_a = decay[d] * r
            a_ref[d, pl.ds(r0, ROW_BLK), :] = jnp.exp(log_a)
            th = jnp.tanh(log_a)
            b_ref[d, pl.ds(r0, ROW_BLK), :] = jnp.sqrt(-2.0 * th / (1.0 - th)) * (i * x)
        return carry

    lax.fori_loop(0, L // ROW_BLK, coef_body, 0)

    nt_ctx = n_ctx // SUBLANES
    nt_all = L // SUBLANES

    def scan_tile(t, d, hc):
        r0 = pl.multiple_of(t * SUBLANES, SUBLANES)
        a = a_ref[d, pl.ds(r0, SUBLANES), :]
        b = b_ref[d, pl.ds(r0, SUBLANES), :]
        for sh in (1, 2, 4):
            if d == 0:
                a_s = pltpu.roll(a, sh, 0)
                b_s = pltpu.roll(b, sh, 0)
                m = row >= sh
            else:
                a_s = pltpu.roll(a, SUBLANES - sh, 0)
                b_s = pltpu.roll(b, SUBLANES - sh, 0)
                m = row < SUBLANES - sh
            b = jnp.where(m, a * b_s + b, b)
            a = jnp.where(m, a * a_s, a)
        hcur = b + a * hc
        acc_ref[pl.ds(r0, SUBLANES), :] += hcur
        last = hcur[SUBLANES - 1:SUBLANES, :] if d == 0 else hcur[0:1, :]
        return jnp.broadcast_to(last, (SUBLANES, width))

    def scan_body(s, carry):
        hf, hb = carry
        tb = jnp.where(s < nt_ctx, nt_ctx - 1 - s, nt_all + nt_ctx - 1 - s)
        return scan_tile(s, 0, hf), scan_tile(tb, 1, hb)

    zero = jnp.zeros((SUBLANES, width), F32)
    lax.fori_loop(0, nt_all, scan_body, (zero, zero))


    def out_body(c, carry):
        r0 = pl.multiple_of(c * CHUNK, CHUNK)
        o_ref[0, pl.ds(r0, CHUNK), :] = acc_ref[pl.ds(r0, CHUNK), :] * _gelu_tanh(gt_ref[0, pl.ds(r0, CHUNK), :])
        return carry

    lax.fori_loop(0, nct, out_body, 0)


def _lru(xb, gate, conv_w, conv_b, wa, wx, pvec, n_ctx, n_lat):
    bsz, L, width = xb.shape
    ng = width // LANES
    lp = L + 3 * HALO
    assert L % ROW_BLK == 0
    seq = pl.BlockSpec((1, L, LANES), lambda b, j: (b, 0, j))
    return pl.pallas_call(
        functools.partial(_lru_kernel, n_ctx=n_ctx, n_lat=n_lat),
        grid=(bsz, ng),
        in_specs=[seq, seq,
                  pl.BlockSpec((CONV_W, LANES), lambda b, j: (0, j)),
                  pl.BlockSpec((1, LANES), lambda b, j: (0, j)),
                  pl.BlockSpec((2, 1, LANES, LANES), lambda b, j: (0, j, 0, 0)),
                  pl.BlockSpec((2, 1, LANES, LANES), lambda b, j: (0, j, 0, 0)),
                  pl.BlockSpec((2, SUBLANES, LANES), lambda b, j: (0, 0, j))],
        out_specs=seq,
        out_shape=jax.ShapeDtypeStruct((bsz, L, width), F32),
        scratch_shapes=[pltpu.VMEM((lp, LANES), F32), pltpu.VMEM((L, LANES), F32), pltpu.VMEM((2, L, LANES), F32),
                        pltpu.VMEM((2, L, LANES), F32), pltpu.VMEM((L, LANES), F32)],
        compiler_params=_cparams(("arbitrary", "arbitrary")),
        name="lru_mixer",
    )(xb, gate, conv_w, conv_b.reshape(1, width), wa, wx, pvec)


def _mlstm_kernel(qc_ref, kc_ref, vc_ref, gcc_ref, grc_ref, ql_ref, kl_ref, vl_ref, gcl_ref, grl_ref, og_ref, nw_ref,
                  o_ref, acc_ref, cc_ref, ms_ref, p2_ref, mi_ref, bc_ref, *, n_ctx, n_lat, n_heads, dqk):
    h = pl.program_id(1)
    ncc = n_ctx // CHUNK
    ncl = n_lat // CHUNK
    dv = LANES
    lane = _iota((CHUNK, LANES), 1)
    ones_blk = jnp.ones((CHUNK, LANES), F32)

    def prepare(refs, slot0, chains, with_out):
        q_ref, k_ref, v_ref, gcol_ref, grow_ref = refs
        idx = range(len(chains))
        r0 = [pl.multiple_of(c * CHUNK, CHUNK) for c, _ in chains]
        k = [k_ref[0, pl.ds(r0[i], CHUNK), :] * (dqk ** -0.5) for i in idx]
        v_ext = [jnp.concatenate([v_ref[0, pl.ds(r0[i], CHUNK), :], ones_blk], axis=1).astype(BF16) for i in idx]
        gates = [gcol_ref[0, pl.ds(r0[i], CHUNK), :] for i in idx]
        li = [_lane_col(gates[i], chains[i][1] * n_heads + h) for i in idx]
        bcum = [_lane_col(gates[i], 2 * n_heads + chains[i][1] * n_heads + h) for i in idx]
        btot = [bcum[i][CHUNK - 1:CHUNK, :] if chains[i][1] == 0 else bcum[i][0:1, :] for i in idx]
        w_end = [btot[i] - bcum[i] + li[i] for i in idx]
        m_chunk = [jnp.max(w_end[i], axis=0, keepdims=True) for i in idx]
        c_chunk = [_tn((k[i] * jnp.exp(w_end[i] - m_chunk[i])).astype(BF16), v_ext[i]) for i in idx]
        if with_out:
            qk = [_nt(q_ref[0, pl.ds(r0[i], CHUNK), :].astype(BF16), k[i].astype(BF16)) for i in idx]
            s_qk, m_intra = [], []
            for i in idx:
                c, d = chains[i]
                grt = grow_ref[0, c]
                rowi = _iota(grt.shape, 0)
                li_row = jnp.sum(jnp.where(rowi == d * n_heads + h, grt, 0.0), axis=0, keepdims=True)
                b_row = jnp.sum(jnp.where(rowi == 2 * n_heads + d * n_heads + h, grt, 0.0), axis=0, keepdims=True)
                incl, _ = _tri(d)
                log_d = jnp.where(incl, bcum[i] + (li_row - b_row), NEG_BIG)
                m_intra.append(jnp.max(log_d, axis=1, keepdims=True))
                s_qk.append((qk[i] * jnp.exp(log_d - m_intra[i])).astype(BF16))
            p2 = [_mm(s_qk[i], v_ext[i]) for i in idx]
        row8 = _iota((SUBLANES, LANES), 0)
        for i in idx:
            c, d = chains[i]
            cc_ref[d, slot0 + c] = c_chunk[i]
            ms_ref[d, slot0 + c] = jnp.where(row8 == 0, btot[i], m_chunk[i])
            if with_out:
                p2_ref[d, pl.ds(r0[i], CHUNK), :] = p2[i]
                mi_ref[d, pl.ds(r0[i], CHUNK), :] = jnp.broadcast_to(m_intra[i], (CHUNK, LANES))
                bc_ref[d, pl.ds(r0[i], CHUNK), :] = jnp.broadcast_to(bcum[i], (CHUNK, LANES))

    ctx_refs = (qc_ref, kc_ref, vc_ref, gcc_ref, grc_ref)
    lat_refs = (ql_ref, kl_ref, vl_ref, gcl_ref, grl_ref)

    def ctx_prep(c2, carry):
        prepare(ctx_refs, 0, [(2 * c2 + j, d) for j in range(2) for d in range(2)], False)
        return carry

    lax.fori_loop(0, ncc // 2, ctx_prep, 0)

    def lat_prep(c4, carry):
        for j in range(4):
            acc_ref[pl.ds(pl.multiple_of((4 * c4 + j) * CHUNK, CHUNK), CHUNK), :] = jnp.zeros((CHUNK, dv), F32)
        prepare(lat_refs, ncc, [(4 * c4 + j, d) for j in range(4) for d in range(2)], True)
        return carry

    lax.fori_loop(0, ncl // 4, lat_prep, 0)

    def advance(states, slots):
        out = []
        for d in range(2):
            cx, m_s = states[d]
            ms = ms_ref[d, slots[d]]
            btot = ms[0:1, :]
            m_chunk = ms[1:2, :]
            m_new = jnp.maximum(btot + m_s, m_chunk)
            f_old = jnp.exp(btot + m_s - m_new)
            f_new = jnp.exp(m_chunk - m_new)
            wide = lambda f: jnp.concatenate([f, f], axis=1)
            out.append((wide(f_old) * cx + wide(f_new) * cc_ref[d, slots[d]], m_new))
        return tuple(out)

    def ctx_step(s, carry):
        return advance(carry, (s, ncc - 1 - s))

    zero = (jnp.zeros((LANES, 2 * dv), F32), jnp.zeros((1, LANES), F32))
    carry = lax.fori_loop(0, ncc, ctx_step, (zero, zero))

    def lat_step(s, carry):
        cs = (s, ncl - 1 - s)
        r0 = [pl.multiple_of(c * CHUNK, CHUNK) for c in cs]
        p1 = [_mm(ql_ref[0, pl.ds(r0[d], CHUNK), :].astype(BF16), carry[d][0].astype(BF16)) for d in range(2)]
        for d in range(2):
            m_intra = mi_ref[d, pl.ds(r0[d], CHUNK), :]
            m_inter = bc_ref[d, pl.ds(r0[d], CHUNK), :] + carry[d][1]
            m_tot = jnp.maximum(m_inter, m_intra)
            w_inter = jnp.exp(m_inter - m_tot)
            w_intra = jnp.exp(m_intra - m_tot)
            p2 = p2_ref[d, pl.ds(r0[d], CHUNK), :]
            num = w_inter * p1[d][:, :dv] + w_intra * p2[:, :dv]
            den = w_inter * p1[d][:, dv:] + w_intra * p2[:, dv:]
            acc_ref[pl.ds(r0[d], CHUNK), :] += num / jnp.maximum(jnp.abs(den), jnp.exp(-m_tot))
        return advance(carry, (ncc + cs[0], ncc + cs[1]))

    lax.fori_loop(0, ncl, lat_step, carry)

    nw = nw_ref[...]

    def out_body(c, carry):
        r0 = pl.multiple_of(c * ROW_BLK, ROW_BLK)
        hs = acc_ref[pl.ds(r0, ROW_BLK), :]
        y = hs * lax.rsqrt(jnp.mean(hs * hs, axis=-1, keepdims=True) + EPS) * nw
        o_ref[0, pl.ds(r0, ROW_BLK), :] = y * jax.nn.sigmoid(og_ref[0, pl.ds(r0, ROW_BLK), :])
        return carry

    lax.fori_loop(0, n_lat // ROW_BLK, out_body, 0)


def _mlstm(qc, kc, vc, gc, ql, kl, vl, gl, og, gparams, norm_w, n_heads, dqk):
    bsz, n_ctx, _ = qc.shape
    n_lat = ql.shape[1]
    cs = pl.BlockSpec((1, n_ctx, LANES), lambda b, h: (b, 0, h))
    ls = pl.BlockSpec((1, n_lat, LANES), lambda b, h: (b, 0, h))
    assert n_ctx % (2 * CHUNK) == 0 and n_lat % (4 * CHUNK) == 0 and n_lat % ROW_BLK == 0
    gcc, grc = _gates(gc, gparams, n_heads, "mlstm")
    gcl, grl = _gates(gl, gparams, n_heads, "mlstm")
    row_spec = lambda n: pl.BlockSpec((1, n // CHUNK, 2 * SUBLANES, CHUNK), lambda b, h: (b, 0, 0, 0))
    return pl.pallas_call(
        functools.partial(_mlstm_kernel, n_ctx=n_ctx, n_lat=n_lat, n_heads=n_heads, dqk=dqk),
        grid=(bsz, n_heads),
        in_specs=[cs, cs, cs, pl.BlockSpec((1, n_ctx, LANES), lambda b, h: (b, 0, 0)), row_spec(n_ctx),
                  ls, ls, ls, pl.BlockSpec((1, n_lat, LANES), lambda b, h: (b, 0, 0)), row_spec(n_lat),
                  ls,
                  pl.BlockSpec((1, LANES), lambda b, h: (0, h))],
        out_specs=ls,
        out_shape=jax.ShapeDtypeStruct((bsz, n_lat, n_heads * LANES), F32),
        scratch_shapes=[pltpu.VMEM((n_lat, LANES), F32),
                        pltpu.VMEM((2, (n_ctx + n_lat) // CHUNK, LANES, 2 * LANES), F32),
                        pltpu.VMEM((2, (n_ctx + n_lat) // CHUNK, SUBLANES, LANES), F32),
                        pltpu.VMEM((2, n_lat, 2 * LANES), F32),
                        pltpu.VMEM((2, n_lat, LANES), F32),
                        pltpu.VMEM((2, n_lat, LANES), F32)],
        compiler_params=_cparams(("arbitrary", "arbitrary")),
        name="mlstm_mixer",
    )(qc, kc, vc, gcc, grc, ql, kl, vl, gcl, grl, og, norm_w.reshape(1, -1))


def _pool_kernel(x_ref, w_ref, sc_ref, o_ref, *, seg):
    tm = x_ref.shape[0]
    r = _iota((tm, tm), 0)
    c = _iota((tm, tm), 1)
    sh = seg.bit_length() - 1
    same = lax.shift_right_logical(r, sh) == lax.shift_right_logical(c, sh)
    t = r & (seg - 1)
    s = c & (seg - 1)
    for gi, wsz in enumerate(POOL_SIZES):
        lo = jnp.maximum(t - wsz // 2, 0)
        hi = jnp.minimum(t - wsz // 2 + wsz, seg)
        inwin = same & (s >= lo) & (s < hi)
        pmat = jnp.where(inwin, 1.0 / (hi - lo).astype(F32), 0.0) - jnp.where(r == c, 1.0, 0.0)
        x = x_ref[:, gi * LANES:(gi + 1) * LANES]
        pooled = _mm3(pmat, x)
        y = _mm(pooled.astype(BF16), w_ref[gi].astype(BF16))
        o_ref[:, gi * LANES:(gi + 1) * LANES] = y * sc_ref[:, gi * LANES:(gi + 1) * LANES]


def _pool(x, w_grp, scale, seg):
    t, width = x.shape
    tm = TOK_BLK
    assert seg & (seg - 1) == 0 and tm % seg == 0
    return pl.pallas_call(
        functools.partial(_pool_kernel, seg=seg),
        grid=(t // tm,),
        in_specs=[pl.BlockSpec((tm, width), lambda i: (i, 0)),
                  pl.BlockSpec(w_grp.shape, lambda i: (0, 0, 0)),
                  pl.BlockSpec((1, width), lambda i: (0, 0))],
        out_specs=pl.BlockSpec((tm, width), lambda i: (i, 0)),
        out_shape=jax.ShapeDtypeStruct((t, width), F32),
        compiler_params=_cparams(("arbitrary",)),
        name="pool_mixer",
    )(x, w_grp, scale.reshape(1, width))


def _outproj_kernel(x_ref, ya_ref, yb_ref, wa_ref, wb_ref, g1_ref, gn_ref, sh_ref, sc_ref, xo_ref, h_ref):
    y = _mm(ya_ref[...].astype(BF16), wa_ref[...]) + _mm(yb_ref[...].astype(BF16), wb_ref[...])
    xn = x_ref[...] + g1_ref[0] * y
    xo_ref[...] = xn
    hn = xn * lax.rsqrt(jnp.mean(xn * xn, axis=-1, keepdims=True) + EPS) * gn_ref[...]
    h_ref[...] = (hn * (1.0 + sc_ref[0]) + sh_ref[0]).astype(BF16)


def _outproj(x, ya, yb, w_out, gain, tab, rowfn):
    t, d = x.shape
    wa_n = ya.shape[1]
    tm = TOK_BLK
    w = w_out.astype(BF16)
    mod = lambda k: pl.BlockSpec((1, 1, d), lambda i: (rowfn(i) + k, 0, 0))
    return pl.pallas_call(
        _outproj_kernel,
        grid=(t // tm,),
        in_specs=[pl.BlockSpec((tm, d), lambda i: (i, 0)),
                  pl.BlockSpec((tm, wa_n), lambda i: (i, 0)),
                  pl.BlockSpec((tm, yb.shape[1]), lambda i: (i, 0)),
                  pl.BlockSpec((wa_n, d), lambda i: (0, 0)),
                  pl.BlockSpec((yb.shape[1], d), lambda i: (0, 0)),
                  mod(2),
                  pl.BlockSpec((1, d), lambda i: (0, 0)),
                  mod(3), mod(4)],
        out_specs=[pl.BlockSpec((tm, d), lambda i: (i, 0)), pl.BlockSpec((tm, d), lambda i: (i, 0))],
        out_shape=[jax.ShapeDtypeStruct((t, d), F32), jax.ShapeDtypeStruct((t, d), BF16)],
        compiler_params=_cparams(("arbitrary",)),
        name="out_proj",
    )(x, ya, yb, w[:wa_n], w[wa_n:], tab, gain.reshape(1, d), tab, tab)


def _hyperbola_pairs():
    return [(j1, j2) for j1 in range(PEER_TOPK) for j2 in range(PEER_TOPK) if (j1 + 1) * (j2 + 1) <= PEER_TOPK]


def _topk_rows(s, k, break_ties):
    n, t = s.shape
    rowi = _iota((n, t), 0).astype(F32)
    rank = jnp.full((n, t), float(2 * k), F32)
    cur = s
    vals = []
    for j in range(k):
        m = jnp.max(cur, axis=0, keepdims=True)
        sel = cur == m
        if break_ties:
            sel = rowi == jnp.min(jnp.where(sel, rowi, float(n)), axis=0, keepdims=True)
        rank = jnp.where(sel, float(j), rank)
        cur = jnp.where(sel, -jnp.inf, cur)
        vals.append(m)
    return vals, rank


def _candidate_stage(v1, v2, break_ties):
    pairs = _hyperbola_pairs()
    cand = jnp.concatenate([v1[j1] + v2[j2] for j1, j2 in pairs], axis=0)
    _, crank = _topk_rows(cand, PEER_TOPK, break_ties)
    chosen = crank < PEER_TOPK
    zsum = jnp.sum(jnp.where(chosen, jnp.exp(cand - (v1[0] + v2[0])), 0.0), axis=0, keepdims=True)
    counts = jnp.where(chosen, 1.0, 0.0)
    n_by_rank = []
    row = 0
    for j1 in range(PEER_TOPK):
        width = PEER_TOPK // (j1 + 1)
        n_by_rank.append(jnp.sum(counts[row:row + width], axis=0, keepdims=True))
        row += width
    return n_by_rank, zsum, jnp.sum(counts, axis=0, keepdims=True)


def _route(s1, s2):
    v1, rank1 = _topk_rows(s1, PEER_TOPK, True)
    v2, rank2 = _topk_rows(s2, PEER_TOPK, True)
    n_by_rank, zsum, _ = _candidate_stage(v1, v2, True)
    n1 = jnp.zeros_like(rank1)
    for j1 in range(PEER_TOPK):
        n1 = jnp.where(rank1 == float(j1), n_by_rank[j1], n1)
    e2 = jnp.where(rank2 < PEER_TOPK, jnp.exp(s2 - v2[0]), 0.0)
    e1 = jnp.where(rank1 < PEER_TOPK, jnp.exp(s1 - v1[0]), 0.0) / zsum
    return rank2, e2, n1, e1


def _sorted_top(s):
    n = s.shape[0] // SUBLANES
    a = [s[SUBLANES * j:SUBLANES * (j + 1), :] for j in range(n)]

    def cex(i, l):
        a[i], a[l] = jnp.maximum(a[i], a[l]), jnp.minimum(a[i], a[l])

    k = 2
    while k <= n:
        j = k // 2
        while j >= 1:
            for i in range(n):
                l = i ^ j
                if l > i:
                    if (i & k) == 0:
                        cex(i, l)
                    else:
                        cex(l, i)
            j //= 2
        k *= 2
    for shift in (4, 2, 1):
        a = [jnp.maximum(a[j], pltpu.roll(a[n - 1 - j], shift, 0)) for j in range(n)]
        j = n // 2
        while j >= 1:
            for i in range(n):
                l = i ^ j
                if l > i:
                    cex(i, l)
            j //= 2
    return a


def _by_rank(bits, table):
    level = list(table)
    for g in reversed(bits):
        level = [jnp.where(g, level[2 * i + 1], level[2 * i]) for i in range(len(level) // 2)]
    return level[0]


def _rank_bits(x, v):
    g8 = v[7] > x
    g4 = jnp.where(g8, v[11], v[3]) > x
    g2 = _by_rank([g8, g4], [v[1], v[5], v[9], v[13]]) > x
    g1 = _by_rank([g8, g4, g2], [v[2 * i] for i in range(8)]) > x
    return [g8, g4, g2, g1]


def _route_untied(s1, s2):
    v1 = _sorted_top(s1)
    v2 = _sorted_top(s2)
    n_by_rank, zsum, _ = _candidate_stage([v[0:1, :] for v in v1], [v[0:1, :] for v in v2], True)
    shape = v1[0].shape
    n_tab = [jnp.broadcast_to(nj, shape) for nj in n_by_rank]
    inv_z = jnp.broadcast_to(1.0 / zsum, shape)
    rank2, e2, n1, e1 = [], [], [], []
    cnt1 = jnp.zeros(shape, F32)
    cnt2 = jnp.zeros(shape, F32)
    for r in range(s1.shape[0] // SUBLANES):
        x1 = s1[SUBLANES * r:SUBLANES * (r + 1), :]
        x2 = s2[SUBLANES * r:SUBLANES * (r + 1), :]
        in1 = x1 >= v1[PEER_TOPK - 1]
        in2 = x2 >= v2[PEER_TOPK - 1]
        n1.append(jnp.where(in1, _by_rank(_rank_bits(x1, v1), n_tab), 0.0))
        e1.append(jnp.where(in1, jnp.exp(x1 - v1[0]) * inv_z, 0.0))
        g8, g4, g2, g1 = _rank_bits(x2, v2)
        rk = (jnp.where(g8, 8.0, 0.0) + jnp.where(g4, 4.0, 0.0)) + (jnp.where(g2, 2.0, 0.0) + jnp.where(g1, 1.0, 0.0))
        rank2.append(jnp.where(in2, rk, float(2 * PEER_TOPK)))
        e2.append(jnp.where(in2, jnp.exp(x2 - v2[0]), 0.0))
        cnt1 = cnt1 + jnp.where(in1, 1.0, 0.0)
        cnt2 = cnt2 + jnp.where(in2, 1.0, 0.0)
    tied = jnp.zeros((1, shape[1]), F32)
    for cnt in (cnt1, cnt2):
        tied = tied + jnp.abs(jnp.sum(cnt, axis=0, keepdims=True) - float(PEER_TOPK))
    for v in (v1, v2):
        for j in range(PEER_TOPK - 1):
            tied = tied + jnp.where(v[j][0:1, :] == v[j + 1][0:1, :], 1.0, 0.0)
    cat = lambda parts: jnp.concatenate(parts, axis=0)
    return cat(rank2), cat(e2), cat(n1), cat(e1), tied


def _peer_select_kernel(h_ref, wq_ref, keys_ref, r2_ref, e2_ref, n1_ref, e1_ref):
    hb = h_ref[...]
    heads = range(wq_ref.shape[0])
    scores = []
    for j in heads:
        q = _mm(hb, wq_ref[j])
        scores.append((_nt(keys_ref[j, 0], q[:, :N_KEYS], HI),
                       _nt(keys_ref[j, 1], q[:, N_KEYS:], HI)))

    def emit(j, rank2, e2, n1, e1):
        r2_ref[j] = rank2.astype(r2_ref.dtype)
        e2_ref[j] = e2.astype(e2_ref.dtype)
        n1_ref[j] = n1
        e1_ref[j] = e1

    tied = []
    for j in heads:
        rank2, e2, n1, e1, t = _route_untied(*scores[j])
        emit(j, rank2, e2, n1, e1)
        tied.append(jnp.max(t))

    for j in heads:
        @pl.when(tied[j] > 0.0)
        def _(j=j):
            emit(j, *_route(*scores[j]))


def _peer_select(h, wq, keys):
    t, d = h.shape
    nh = keys.shape[0]
    qd = wq.shape[2]
    tm = PEER_BLK
    hps = SELECT_HEADS
    assert nh % hps == 0
    ospec = pl.BlockSpec((hps, N_KEYS, tm), lambda i, hh: (hh, 0, i))
    return pl.pallas_call(
        _peer_select_kernel,
        grid=(t // tm, nh // hps),
        in_specs=[pl.BlockSpec((tm, d), lambda i, hh: (i, 0)),
                  pl.BlockSpec((hps, d, qd), lambda i, hh: (hh, 0, 0)),
                  pl.BlockSpec((hps, 2, N_KEYS, qd // 2), lambda i, hh: (hh, 0, 0, 0))],
        out_specs=[ospec, ospec, ospec, ospec],
        out_shape=[jax.ShapeDtypeStruct((nh, N_KEYS, t), dt) for dt in (BF16, BF16, F32, F32)],
        compiler_params=_cparams(("arbitrary", "arbitrary")),
        name="peer_select",
    )(h, wq, keys)


def _peer_dense_kernel(h_ref, u_ref, vt_ref, r2_ref, e2_ref, n1_ref, e1_ref, x_ref, g2a_ref, g2b_ref, fn_ref, o_ref,
                       acc_ref, wact_ref, *, n_heads, final):
    eb = pl.program_id(1)
    n_eb = pl.num_programs(1) - 1
    cur = eb & 1
    prev = 1 - cur

    @pl.when(eb == 0)
    def _():
        acc_ref[...] = jnp.zeros_like(acc_ref)
        wact_ref[1] = jnp.zeros(wact_ref.shape[1:], BF16)

    @pl.when(eb < n_eb)
    def _():
        hb = h_ref[...]
        part_rows = 2 * N_KEYS
        n_part = EXP_BLK // part_rows
        acts = [_nt(u_ref[p * part_rows:(p + 1) * part_rows, :], hb) for p in range(n_part)]
        upd = _mm(vt_ref[0], wact_ref[prev])
        rows_per = EXP_BLK // N_KEYS
        i1_base = pl.multiple_of(eb * rows_per, rows_per)
        n1_tiles = [n1_ref[hh, pl.ds(i1_base, rows_per), :] for hh in range(n_heads)]
        e1_tiles = [e1_ref[hh, pl.ds(i1_base, rows_per), :] for hh in range(n_heads)]
        zero = jnp.zeros((), BF16)
        for r in range(rows_per):
            wgt = None
            for hh in range(n_heads):
                shape = r2_ref.shape[1:]
                n1row = jnp.broadcast_to(n1_tiles[hh][r:r + 1, :], shape).astype(BF16)
                e1row = jnp.broadcast_to(e1_tiles[hh][r:r + 1, :], shape).astype(BF16)
                term = jnp.where(r2_ref[hh] < n1row, e2_ref[hh], zero) * e1row
                wgt = term if wgt is None else wgt + term
            off = (r * N_KEYS) % part_rows
            a = acts[(r * N_KEYS) // part_rows][off:off + N_KEYS, :]
            wact_ref[cur, r * N_KEYS:(r + 1) * N_KEYS, :] = _gelu_tanh(a.astype(BF16)) * wgt
        acc_ref[...] += upd

    @pl.when(eb == n_eb)
    def _():
        acc_ref[...] += _mm(vt_ref[0], wact_ref[prev])
        for part, g2_ref in enumerate((g2a_ref, g2b_ref)):
            rows = slice(part * TOK_BLK, (part + 1) * TOK_BLK)
            y = x_ref[rows, :] + g2_ref[0] * acc_ref[:, rows].T
            if final:
                y = y * lax.rsqrt(jnp.mean(y * y, axis=-1, keepdims=True) + EPS) * fn_ref[...]
            o_ref[rows, :] = y


def _peer_dense(h, u, vt, r2, e2, n1, e1, x, tab, rowfn, final_gain, final):
    t, d = h.shape
    n_eb = u.shape[0] // EXP_BLK
    nh = r2.shape[0]
    tm = PEER_BLK
    sel = pl.BlockSpec((nh, N_KEYS, tm), lambda i, e: (0, 0, i))
    g2 = lambda part: pl.BlockSpec((1, 1, d), lambda i, e: (rowfn(2 * i + part) + 5, 0, 0))
    return pl.pallas_call(
        functools.partial(_peer_dense_kernel, n_heads=nh, final=final),
        grid=(t // tm, n_eb + 1),
        in_specs=[pl.BlockSpec((tm, d), lambda i, e: (i, 0)),
                  pl.BlockSpec((EXP_BLK, d), lambda i, e: (jnp.minimum(e, n_eb - 1), 0)),
                  pl.BlockSpec((1, d, EXP_BLK), lambda i, e: (jnp.maximum(e - 1, 0), 0, 0)),
                  sel, sel, sel, sel,
                  pl.BlockSpec((tm, d), lambda i, e: (i, 0)),
                  g2(0), g2(1),
                  pl.BlockSpec((1, d), lambda i, e: (0, 0))],
        out_specs=pl.BlockSpec((tm, d), lambda i, e: (i, 0)),
        out_shape=jax.ShapeDtypeStruct((t, d), F32),
        scratch_shapes=[pltpu.VMEM((d, tm), F32), pltpu.VMEM((2, EXP_BLK, tm), BF16)],
        compiler_params=_cparams(("arbitrary", "arbitrary")),
        name="peer_dense",
    )(h, u, vt, r2, e2, n1, e1, x, tab, tab, final_gain.reshape(1, d))


def _peer(h, x, wq, keys, u_tab, v_tab, tab, rowfn, final_gain, final):
    d = h.shape[1]
    nh = keys.shape[0]
    wq_h = wq.reshape(d, nh, -1).transpose(1, 0, 2).astype(BF16)
    vt = v_tab.reshape(-1, EXP_BLK, d).transpose(0, 2, 1).astype(BF16)
    r2, e2, n1, e1 = _peer_select(h, wq_h, keys)
    return _peer_dense(h, u_tab.astype(BF16), vt, r2, e2, n1, e1, x, tab, rowfn, final_gain, final)


def _lane_vec(vals, rows=SUBLANES):
    out = jnp.zeros((rows, LANES), F32)
    for r, v in enumerate(vals):
        out = out.at[r, :v.shape[0]].set(v.astype(F32))
    return out


def _block_diag_groups(w):
    two, nb, bd, _ = w.shape
    per = LANES // bd
    wg = w.reshape(two, nb // per, per, bd, bd)
    eye = jnp.eye(per, dtype=w.dtype)
    return jnp.einsum("dgpij,pq->dgpiqj", wg, eye).reshape(two, nb // per, LANES, LANES)


def kernel(x, c, ctx, c_ctx, ada_w, ada_b, norm_mix, norm_ffn, final_norm, peer_wq, peer_keys, peer_u, peer_v,
           ev_w_in, ev_w_out, a_conv, a_alog, a_dtb, a_norm, b_conv_w, b_conv_b, b_wa, b_ba, b_wx, b_bx, b_lam,
           od_w_in, od_w_out, c_ibias, c_fbias, c_norm, d_w, d_scale):
    bsz, seq, dm = x.shape
    n_ctx = ctx.shape[1]
    L = n_ctx + seq
    rows = seq // GRID_W
    blk_per = L // TOK_BLK

    tab0 = _mod_table(c, c_ctx, ada_w[0], ada_b[0])
    row0 = lambda i: ((i // blk_per) * 2 + jnp.minimum(i % blk_per, 1)) * 6
    xcat = jnp.concatenate([ctx, x], axis=1).reshape(bsz * L, dm)

    a_heads = a_alog.shape[-1]
    a_width = a_heads * LANES
    a_qkv = 3 * a_width
    b_width = b_lam.shape[-1] * b_lam.shape[-2]
    w_in = ev_w_in[0]
    a_cols = a_qkv + a_width + 4 * a_heads
    w_ab = jnp.pad(w_in[:, a_qkv + a_width:a_cols], ((0, 0), (0, LANES - 4 * a_heads)))
    w0 = jnp.concatenate([w_in[:, :a_qkv], w_in[:, a_qkv:a_qkv + a_width], w_in[:, a_cols:a_cols + b_width],
                          w_in[:, a_cols + b_width:], w_ab], axis=1).astype(BF16)
    qkv, z, xb, gate, ab = _norm_mod_proj(xcat, norm_mix[0], tab0, row0, w0, (a_qkv, a_width, b_width, b_width, LANES))

    gparams = _lane_vec([a_alog[0].reshape(-1), a_dtb[0].reshape(-1)])
    ya = _gdn(qkv.reshape(bsz, L, a_qkv), z.reshape(bsz, L, a_width), ab.reshape(bsz, L, LANES),
              a_conv[0], gparams, a_norm[0], n_ctx, seq, a_heads)

    pvec = jnp.zeros((2, SUBLANES, b_width), F32)
    pvec = pvec.at[:, 0].set(b_ba[0].reshape(2, b_width)).at[:, 1].set(b_bx[0].reshape(2, b_width))
    pvec = pvec.at[:, 2].set(b_lam[0].reshape(2, b_width))
    yb = _lru(xb.reshape(bsz, L, b_width), gate.reshape(bsz, L, b_width), b_conv_w[0], b_conv_b[0],
              _block_diag_groups(b_wa[0]).astype(BF16), _block_diag_groups(b_wx[0]).astype(BF16), pvec, n_ctx, seq)

    x1, h1 = _outproj(xcat, ya.reshape(bsz * L, a_width), yb.reshape(bsz * L, b_width), ev_w_out[0], norm_ffn[0],
                      tab0, row0)
    x2 = _peer(h1, x1, peer_wq[0], peer_keys[0], peer_u[0], peer_v[0], tab0, row0, final_norm, False)
    x2 = x2.reshape(bsz, L, dm)

    tab1 = _mod_table(c, c_ctx, ada_w[1], ada_b[1])
    xc = x2[:, :n_ctx].reshape(bsz * n_ctx, dm)
    xl = x2[:, n_ctx:].reshape(bsz, rows, GRID_W, dm).transpose(0, 2, 1, 3).reshape(bsz * seq, dm)
    lat_per = seq // TOK_BLK
    ctx_per = n_ctx // TOK_BLK
    row_lat = lambda i: ((i // lat_per) * 2 + 1) * 6
    row_ctx = lambda i: ((i // ctx_per) * 2) * 6

    c_heads = c_ibias.shape[-1]
    c_width = c_norm.shape[-1]
    d_width = d_scale.shape[-1]
    w_in1 = od_w_in[0]
    n_state = w_in1.shape[1] - c_width - d_width
    c_qk = (n_state - c_width - 4 * c_heads) // 2
    dqk = c_qk // c_heads

    def pad_heads(wcols):
        return jnp.pad(wcols.reshape(dm, c_heads, dqk), ((0, 0), (0, 0), (0, LANES - dqk))).reshape(dm, c_heads * LANES)

    w_gate = jnp.pad(w_in1[:, 2 * c_qk + c_width:n_state], ((0, 0), (0, LANES - 4 * c_heads)))
    w1 = jnp.concatenate([pad_heads(w_in1[:, :c_qk]), pad_heads(w_in1[:, c_qk:2 * c_qk]),
                          w_in1[:, 2 * c_qk:2 * c_qk + c_width], w_in1[:, n_state:n_state + c_width],
                          w_in1[:, n_state + c_width:], w_gate], axis=1).astype(BF16)
    hw = c_heads * LANES
    widths1 = (hw, hw, c_width, c_width, d_width, LANES)
    ql, kl, vl, og, dl, gl = _norm_mod_proj(xl, norm_mix[1], tab1, row_lat, w1, widths1)
    qc, kc, vc, _, _, gc = _norm_mod_proj(xc, norm_mix[1], tab1, row_ctx, w1, widths1)

    gparams1 = _lane_vec([jnp.concatenate([c_ibias[0].reshape(-1), jnp.zeros((2 * c_heads,), F32)]),
                          jnp.concatenate([jnp.zeros((2 * c_heads,), F32), c_fbias[0].reshape(-1)])])
    r3 = lambda a, n: a.reshape(bsz, n, a.shape[-1])
    yc1 = _mlstm(r3(qc, n_ctx), r3(kc, n_ctx), r3(vc, n_ctx), r3(gc, n_ctx),
                 r3(ql, seq), r3(kl, seq), r3(vl, seq), r3(gl, seq), r3(og, seq), gparams1, c_norm[0], c_heads, dqk)
    yd1 = _pool(dl, d_w[0], d_scale[0], rows)

    x3, h3 = _outproj(xl, yc1.reshape(bsz * seq, c_width), yd1, od_w_out[0], norm_ffn[1], tab1, row_lat)
    out = _peer(h3, x3, peer_wq[1], peer_keys[1], peer_u[1], peer_v[1], tab1, row_lat, final_norm, True)
    return out.reshape(bsz, GRID_W, rows, dm).transpose(0, 2, 1, 3).reshape(bsz, seq, dm)
```

```python
import functools
import math

import jax
import jax.numpy as jnp
import numpy as np
from jax import lax
from jax.experimental import pallas as pl
from jax.experimental.pallas import tpu as pltpu

F32 = jnp.float32
BF16 = jnp.bfloat16
HI = lax.Precision.HIGHEST

EPS = 1e-6
GRID_W = 64
CHUNK = 64
LANES = 128
SUBLANES = 8
HALO = 8
CONV_W = 4
LRU_C = 8.0
PEER_TOPK = 16
N_KEYS = 128
POOL_SIZES = (2, 4, 8, 16)
TOK_BLK = 256
PEER_BLK = 2 * TOK_BLK
ROW_BLK = 256
EXP_BLK = 2048
SELECT_HEADS = 2
NEG_BIG = -1e30
VMEM_LIMIT = 60 * 1024 * 1024
TABLE_BUFFERS = 3


def _cparams(sem):
    return pltpu.CompilerParams(dimension_semantics=sem, vmem_limit_bytes=VMEM_LIMIT)


def _nt(a, b, precision=None):
    return lax.dot_general(a, b, (((1,), (1,)), ((), ())), precision=precision, preferred_element_type=F32)


def _tn(a, b, precision=None):
    return lax.dot_general(a, b, (((0,), (0,)), ((), ())), precision=precision, preferred_element_type=F32)


def _mm(a, b, precision=None):
    return jnp.dot(a, b, precision=precision, preferred_element_type=F32)


def _silu(x):
    return x * jax.nn.sigmoid(x)


def _softplus(x):
    return jnp.maximum(x, 0.0) + jnp.log1p(jnp.exp(-jnp.abs(x)))


def _gelu_tanh(x):
    return 0.5 * x * (1.0 + jnp.tanh(math.sqrt(2.0 / math.pi) * (x + 0.044715 * (x * x * x))))


def _iota(shape, dim):
    return lax.broadcasted_iota(jnp.int32, shape, dim)


def _ada_kernel(c_ref, w_ref, b_ref, o_ref):
    o_ref[...] = _mm(_silu(c_ref[...]), w_ref[...], HI) + b_ref[...]


def _ada(cc, w, b):
    r, d = cc.shape
    n = w.shape[1]
    tn = 1536
    return pl.pallas_call(
        _ada_kernel,
        grid=(n // tn,),
        in_specs=[pl.BlockSpec((r, d), lambda j: (0, 0)),
                  pl.BlockSpec((d, tn), lambda j: (0, j)),
                  pl.BlockSpec((1, tn), lambda j: (0, j))],
        out_specs=pl.BlockSpec((r, tn), lambda j: (0, j)),
        out_shape=jax.ShapeDtypeStruct((r, n), F32),
        compiler_params=_cparams(("arbitrary",)),
        name="ada_mod",
    )(cc, w, b.reshape(1, n))


def _mod_table(c, c_ctx, w, b):
    bsz, d = c.shape
    rows = ((bsz + 1 + SUBLANES - 1) // SUBLANES) * SUBLANES
    cc = jnp.zeros((rows, d), F32).at[:bsz].set(c).at[bsz].set(c_ctx)
    m = _ada(cc, w, b).reshape(rows, 6, d)
    tab = jnp.stack([jnp.broadcast_to(m[bsz], (bsz, 6, d)), m[:bsz]], axis=1)
    return tab.reshape(bsz * 2 * 6, 1, d)


def _nmm_kernel(x_ref, g_ref, sh_ref, sc_ref, w_ref, *o_refs, widths):
    x = x_ref[...]
    y = x * lax.rsqrt(jnp.mean(x * x, axis=-1, keepdims=True) + EPS) * g_ref[...]
    h = (y * (1.0 + sc_ref[0]) + sh_ref[0]).astype(BF16)
    off = 0
    for o_ref, n in zip(o_refs, widths):
        o_ref[...] = _mm(h, w_ref[:, off:off + n])
        off += n


def _norm_mod_proj(x, gain, tab, rowfn, w, widths):
    t, d = x.shape
    n = w.shape[1]
    tm = TOK_BLK
    return pl.pallas_call(
        functools.partial(_nmm_kernel, widths=widths),
        grid=(t // tm,),
        in_specs=[pl.BlockSpec((tm, d), lambda i: (i, 0)),
                  pl.BlockSpec((1, d), lambda i: (0, 0)),
                  pl.BlockSpec((1, 1, d), lambda i: (rowfn(i), 0, 0)),
                  pl.BlockSpec((1, 1, d), lambda i: (rowfn(i) + 1, 0, 0)),
                  pl.BlockSpec((d, n), lambda i: (0, 0))],
        out_specs=[pl.BlockSpec((tm, wd), lambda i: (i, 0)) for wd in widths],
        out_shape=[jax.ShapeDtypeStruct((t, wd), F32) for wd in widths],
        compiler_params=_cparams(("arbitrary",)),
        name="norm_mod_proj",
    )(x, gain.reshape(1, d), tab, tab, w)


def _conv_chunk(pad_ref, pbase, w, rows=CHUNK):
    blk = pad_ref[pl.ds(pbase - HALO, rows + 2 * HALO), :]
    n = rows + 2 * HALO
    lo, hi = HALO, HALO + rows
    xm1 = pltpu.roll(blk, 1, 0)[lo:hi]
    x0 = blk[lo:hi]
    xp1 = pltpu.roll(blk, n - 1, 0)[lo:hi]
    xp2 = pltpu.roll(blk, n - 2, 0)[lo:hi]
    return w[0:1] * xm1 + w[1:2] * x0 + w[2:3] * xp1 + w[3:4] * xp2


def _fill_padded(pad_ref, src_ref, n_ctx, n_lat):
    width = pad_ref.shape[1]
    z = jnp.zeros((HALO, width), F32)
    pad_ref[0:HALO, :] = z
    pad_ref[HALO + n_ctx:2 * HALO + n_ctx, :] = z
    pad_ref[2 * HALO + n_ctx + n_lat:3 * HALO + n_ctx + n_lat, :] = z
    pad_ref[HALO:HALO + n_ctx, :] = src_ref[0, 0:n_ctx, :]
    pad_ref[2 * HALO + n_ctx:2 * HALO + n_ctx + n_lat, :] = src_ref[0, n_ctx:n_ctx + n_lat, :]


def _padded_base(c, ncc, rows=CHUNK):
    return pl.multiple_of(HALO + c * rows + jnp.where(c >= ncc, HALO, 0), SUBLANES)


def _lane_col(g, lane_idx):
    lane = _iota(g.shape, 1)
    return jnp.sum(jnp.where(lane == lane_idx, g, 0.0), axis=1, keepdims=True)


def _tri(d):
    r = _iota((CHUNK, CHUNK), 0)
    c = _iota((CHUNK, CHUNK), 1)
    incl = (r >= c) if d == 0 else (r <= c)
    strict = (r > c) if d == 0 else (r < c)
    return incl, strict


def _bwd_chunk(s, ncc, nct):
    return jnp.where(s < ncc, ncc - 1 - s, nct + ncc - 1 - s)


def _split3(x):
    x1 = x.astype(BF16)
    r1 = x - x1.astype(F32)
    x2 = r1.astype(BF16)
    x3 = (r1 - x2.astype(F32)).astype(BF16)
    return x1, x2, x3


def _mask_mm(mask, x):
    x1, x2, x3 = _split3(x)
    return _mm(mask, x1) + (_mm(mask, x2) + _mm(mask, x3))


def _mask_tn(x, mask):
    x1, x2, x3 = _split3(x)
    return _tn(x1, mask) + (_tn(x2, mask) + _tn(x3, mask))


def _gate_kernel(ab_ref, gp_ref, gc_ref, gr_ref, *, n_chunks, n_heads, kind):
    r = _iota((CHUNK, CHUNK), 0)
    c = _iota((CHUNK, CHUNK), 1)
    lower = (r >= c).astype(BF16)
    upper = (r <= c).astype(BF16)
    eye = (r == c).astype(BF16)
    lane = _iota((CHUNK, LANES), 1)
    rowg = _iota((LANES, CHUNK), 0)
    if kind == "gdn":
        fwd_lo, rev_lo, rev_hi = 0, n_heads, 2 * n_heads
    else:
        fwd_lo, rev_lo, rev_hi = 2 * n_heads, 3 * n_heads, 4 * n_heads
    p0 = gp_ref[0:1, :]
    p1 = gp_ref[1:2, :]

    def body(ci, carry):
        r0 = pl.multiple_of(ci * CHUNK, CHUNK)
        raw = ab_ref[0, pl.ds(r0, CHUNK), :]
        if kind == "gdn":
            g = jnp.where(lane < 2 * n_heads, -jnp.exp(p0) * _softplus(raw + p1), jax.nn.sigmoid(raw))
        else:
            g = jnp.where(lane < 2 * n_heads, raw + p0, -_softplus(-(raw + p1)))
        is_fwd = (lane >= fwd_lo) & (lane < rev_lo)
        is_rev = (lane >= rev_lo) & (lane < rev_hi)
        gc_ref[0, pl.ds(r0, CHUNK), :] = jnp.where(is_fwd, _mask_mm(lower, g), jnp.where(is_rev, _mask_mm(upper, g), g))
        row_fwd = (rowg >= fwd_lo) & (rowg < rev_lo)
        row_rev = (rowg >= rev_lo) & (rowg < rev_hi)
        rows = jnp.where(row_fwd, _mask_tn(g, upper), jnp.where(row_rev, _mask_tn(g, lower), _mask_tn(g, eye)))
        gr_ref[0, ci] = rows[0:2 * SUBLANES, :]
        return carry

    lax.fori_loop(0, n_chunks, body, 0)


def _gates(ab, gparams, n_heads, kind):
    bsz, L, _ = ab.shape
    nct = L // CHUNK
    assert 4 * n_heads <= 2 * SUBLANES
    return pl.pallas_call(
        functools.partial(_gate_kernel, n_chunks=nct, n_heads=n_heads, kind=kind),
        grid=(bsz,),
        in_specs=[pl.BlockSpec((1, L, LANES), lambda b: (b, 0, 0)),
                  pl.BlockSpec((SUBLANES, LANES), lambda b: (0, 0))],
        out_specs=[pl.BlockSpec((1, L, LANES), lambda b: (b, 0, 0)),
                   pl.BlockSpec((1, nct, 2 * SUBLANES, CHUNK), lambda b: (b, 0, 0, 0))],
        out_shape=[jax.ShapeDtypeStruct((bsz, L, LANES), F32),
                   jax.ShapeDtypeStruct((bsz, nct, 2 * SUBLANES, CHUNK), F32)],
        compiler_params=_cparams(("arbitrary",)),
        name=kind + "_gates",
    )(ab, gparams)


def _mm3(a, b):
    ah = a.astype(BF16)
    al = (a - ah.astype(F32)).astype(BF16)
    bh = b.astype(BF16)
    bl = (b - bh.astype(F32)).astype(BF16)
    return _mm(ah, bh) + (_mm(ah, bl) + _mm(al, bh))


def _gdn_kernel(q_ref, k_ref, v_ref, z_ref, gc_ref, gr_ref, cwq_ref, cwk_ref, cwv_ref, nw_ref, o_ref,
                pad_ref, qs_ref, ks_ref, vs_ref, acc_ref, u_ref, w_ref, qd_ref, kd_ref, qk_ref, gl_ref,
                *, n_ctx, n_lat, n_heads):
    h = pl.program_id(1)
    ncc = n_ctx // CHUNK
    nct = (n_ctx + n_lat) // CHUNK
    dk = qs_ref.shape[1]

    def prep(src_ref, cw_ref, dst_ref, mode):
        _fill_padded(pad_ref, src_ref, n_ctx, n_lat)
        w = cw_ref[...]

        def body(c, carry):
            y = _silu(_conv_chunk(pad_ref, _padded_base(c, n_ctx // ROW_BLK, ROW_BLK), w, ROW_BLK))
            if mode != "v":
                y = y * lax.rsqrt(jnp.sum(y * y, axis=-1, keepdims=True) + EPS)
            if mode == "q":
                y = y * (dk ** -0.5)
            dst_ref[pl.ds(pl.multiple_of(c * ROW_BLK, ROW_BLK), ROW_BLK), :] = y
            return carry

        lax.fori_loop(0, (n_ctx + n_lat) // ROW_BLK, body, 0)

    prep(q_ref, cwq_ref, qs_ref, "q")
    prep(k_ref, cwk_ref, ks_ref, "k")
    prep(v_ref, cwv_ref, vs_ref, "v")

    n_chain = 4
    big = n_chain * CHUNK
    rr = _iota((big, big), 0)
    cc = _iota((big, big), 1)
    log_chunk = CHUNK.bit_length() - 1
    same_blk = lax.shift_right_logical(rr, log_chunk) == lax.shift_right_logical(cc, log_chunk)
    bwd_blk = (lax.shift_right_logical(rr, log_chunk) & 1) == 1
    fwd_blk = jnp.logical_not(bwd_blk)
    strict_bd = same_blk & ((bwd_blk & (rr < cc)) | (fwd_blk & (rr > cc)))
    incl_bd = same_blk & ((bwd_blk & (rr <= cc)) | (fwd_blk & (rr >= cc)))

    def group_setup(c0):
        ks, rhs, gcols, bcols, grows = [], [], [], [], []
        chains = []
        for j in range(2):
            c = c0 + j
            r0 = pl.multiple_of(c * CHUNK, CHUNK)
            acc_ref[pl.ds(r0, CHUNK), :] = jnp.zeros((CHUNK, dk), F32)
            q = qs_ref[pl.ds(r0, CHUNK), :]
            k = ks_ref[pl.ds(r0, CHUNK), :]
            v = vs_ref[pl.ds(r0, CHUNK), :]
            gates = gc_ref[0, pl.ds(r0, CHUNK), :]
            grt = gr_ref[0, c]
            for d in range(2):
                gcol = _lane_col(gates, d * n_heads + h)
                bcol = _lane_col(gates, 2 * n_heads + d * n_heads + h)
                grow = jnp.sum(jnp.where(_iota(grt.shape, 0) == d * n_heads + h, grt, 0.0), axis=0, keepdims=True)
                gtot = gcol[CHUNK - 1:CHUNK, :] if d == 0 else gcol[0:1, :]
                eg = jnp.exp(gcol)
                ks.append(k)
                gcols.append(gcol)
                bcols.append(bcol)
                grows.append(grow)
                rhs.append(jnp.concatenate([bcol * v, (bcol * eg) * k], axis=1))
                chains.append((c, d, r0, q, k, gcol, grow, gtot, eg))
        kst = jnp.concatenate(ks, axis=0).astype(BF16)
        gam = jnp.where(incl_bd, jnp.exp(jnp.minimum(jnp.concatenate(gcols, axis=0) - jnp.concatenate(grows, axis=1),
                                                      0.0)), 0.0)
        m_bd = jnp.where(strict_bd, jnp.concatenate(bcols, axis=0) * _nt(kst, kst) * gam, 0.0)
        return chains, m_bd, jnp.concatenate(rhs, axis=0)

    def group_store(chains, sol):
        for b, (c, d, r0, q, k, gcol, grow, gtot, eg) in enumerate(chains):
            incl, _ = _tri(d)
            gamma = jnp.where(incl, jnp.exp(jnp.minimum(gcol - grow, 0.0)), 0.0)
            u_ref[d, pl.ds(r0, CHUNK), :] = sol[b * CHUNK:(b + 1) * CHUNK, :dk]
            w_ref[d, pl.ds(r0, CHUNK), :] = sol[b * CHUNK:(b + 1) * CHUNK, dk:].astype(BF16)
            qd_ref[d, pl.ds(r0, CHUNK), :] = (q * eg).astype(BF16)
            kd_ref[d, pl.ds(r0, CHUNK), :] = (k * jnp.exp(gtot - gcol)).astype(BF16)
            qk_ref[d, c] = (_nt(q.astype(BF16), k.astype(BF16)) * gamma).astype(BF16)
            gl_ref[d, c] = jnp.broadcast_to(jnp.exp(gtot), (SUBLANES, LANES))

    def prepare_body(c4, carry):
        groups = [group_setup(4 * c4), group_setup(4 * c4 + 2)]
        idx = range(len(groups))
        ms = [g[1] for g in groups]
        sols = [g[2] for g in groups]
        sols = [sols[i] - _mm3(ms[i], sols[i]) for i in idx]
        ps = [m.astype(BF16) for m in ms]
        for it in range(5):
            ps = [_mm(p, p) for p in ps]
            ps = [p.astype(BF16) for p in ps]
            sols = [sols[i] + _mm(ps[i], sols[i].astype(BF16)) for i in idx]
        for i in idx:
            group_store(groups[i][0], sols[i])
        return carry

    lax.fori_loop(0, nct // 4, prepare_body, 0)

    def step(s, carry):
        cs = (s, _bwd_chunk(s, ncc, nct))
        r0 = [pl.multiple_of(c * CHUNK, CHUNK) for c in cs]
        sb = [st.astype(BF16) for st in carry]
        ws = [_mm(w_ref[d, pl.ds(r0[d], CHUNK), :], sb[d]) for d in range(2)]
        qs = [_mm(qd_ref[d, pl.ds(r0[d], CHUNK), :], sb[d]) for d in range(2)]
        vb = [(u_ref[d, pl.ds(r0[d], CHUNK), :] - ws[d]).astype(BF16) for d in range(2)]
        kv = [_tn(kd_ref[d, pl.ds(r0[d], CHUNK), :], vb[d]) for d in range(2)]
        ov = [_mm(qk_ref[d, cs[d]], vb[d]) for d in range(2)]
        for d in range(2):
            acc_ref[pl.ds(r0[d], CHUNK), :] += qs[d] + ov[d]
        return tuple(carry[d] * gl_ref[d, cs[d]][0:1, :] + kv[d] for d in range(2))

    zero = jnp.zeros((dk, dk), F32)
    lax.fori_loop(0, nct, step, (zero, zero))

    nw = nw_ref[...]

    def out_body(c, carry):
        r0 = pl.multiple_of(c * ROW_BLK, ROW_BLK)
        o = acc_ref[pl.ds(r0, ROW_BLK), :]
        y = o * lax.rsqrt(jnp.mean(o * o, axis=-1, keepdims=True) + EPS) * nw
        o_ref[0, pl.ds(r0, ROW_BLK), :] = y * _silu(z_ref[0, pl.ds(r0, ROW_BLK), :])
        return carry

    lax.fori_loop(0, (n_ctx + n_lat) // ROW_BLK, out_body, 0)


def _gdn(qkv, z, ab, conv_w, gparams, norm_w, n_ctx, n_lat, n_heads):
    bsz, L, _ = qkv.shape
    dk = LANES
    seq = lambda off: pl.BlockSpec((1, L, dk), lambda b, h: (b, 0, off + h))
    cw = lambda off: pl.BlockSpec((CONV_W, dk), lambda b, h: (0, off + h))
    lp = L + 3 * HALO
    nct = L // CHUNK
    assert nct % 4 == 0 and n_ctx % ROW_BLK == 0 and n_lat % ROW_BLK == 0
    gc, gr = _gates(ab, gparams, n_heads, "gdn")
    return pl.pallas_call(
        functools.partial(_gdn_kernel, n_ctx=n_ctx, n_lat=n_lat, n_heads=n_heads),
        grid=(bsz, n_heads),
        in_specs=[seq(0), seq(n_heads), seq(2 * n_heads),
                  pl.BlockSpec((1, L, dk), lambda b, h: (b, 0, h)),
                  pl.BlockSpec((1, L, LANES), lambda b, h: (b, 0, 0)),
                  pl.BlockSpec((1, nct, 2 * SUBLANES, CHUNK), lambda b, h: (b, 0, 0, 0)),
                  cw(0), cw(n_heads), cw(2 * n_heads),
                  pl.BlockSpec((1, dk), lambda b, h: (0, 0))],
        out_specs=pl.BlockSpec((1, L, dk), lambda b, h: (b, 0, h)),
        out_shape=jax.ShapeDtypeStruct((bsz, L, n_heads * dk), F32),
        scratch_shapes=[pltpu.VMEM((lp, dk), F32)] + [pltpu.VMEM((L, dk), F32) for _ in range(4)]
        + [pltpu.VMEM((2, L, dk), F32)] + [pltpu.VMEM((2, L, dk), BF16) for _ in range(3)]
        + [pltpu.VMEM((2, nct, CHUNK, CHUNK), BF16), pltpu.VMEM((2, nct, SUBLANES, LANES), F32)],
        compiler_params=_cparams(("arbitrary", "arbitrary")),
        name="gdn_mixer",
    )(qkv, qkv, qkv, z, gc, gr, conv_w, conv_w, conv_w, norm_w.reshape(1, dk))


def _lru_kernel(x_ref, gt_ref, cw_ref, cb_ref, wa_ref, wx_ref, pv_ref, o_ref,
                pad_ref, xc_ref, a_ref, b_ref, acc_ref, *, n_ctx, n_lat):
    ncc = n_ctx // CHUNK
    nct = (n_ctx + n_lat) // CHUNK
    L = n_ctx + n_lat
    width = xc_ref.shape[1]
    _fill_padded(pad_ref, x_ref, n_ctx, n_lat)
    w = cw_ref[...]
    cb = cb_ref[...]

    def conv_body(c, carry):
        r0 = pl.multiple_of(c * CHUNK, CHUNK)
        xc_ref[pl.ds(r0, CHUNK), :] = _conv_chunk(pad_ref, _padded_base(c, ncc), w) + cb
        acc_ref[pl.ds(r0, CHUNK), :] = jnp.zeros((CHUNK, width), F32)
        return carry

    lax.fori_loop(0, nct, conv_body, 0)

    row = _iota((SUBLANES, width), 0)
    bias_a = [pv_ref[d, 0:1, :] for d in range(2)]
    bias_x = [pv_ref[d, 1:2, :] for d in range(2)]
    decay = [-LRU_C * _softplus(-pv_ref[d, 2:3, :]) for d in range(2)]

    def coef_body(c, carry):
        r0 = pl.multiple_of(c * ROW_BLK, ROW_BLK)
        x = xc_ref[pl.ds(r0, ROW_BLK), :]
        xb = x.astype(BF16)
        for d in range(2):
            r = jax.nn.sigmoid(_mm(xb, wa_ref[d, 0]) + bias_a[d])
            i = jax.nn.sigmoid(_mm(xb, wx_ref[d, 0]) + bias_x[d])
            log_a = decay[d] * r
            a_ref[d, pl.ds(r0, ROW_BLK), :] = jnp.exp(log_a)
            th = jnp.tanh(log_a)
            b_ref[d, pl.ds(r0, ROW_BLK), :] = jnp.sqrt(-2.0 * th / (1.0 - th)) * (i * x)
        return carry

    lax.fori_loop(0, L // ROW_BLK, coef_body, 0)

    nt_ctx = n_ctx // SUBLANES
    nt_all = L // SUBLANES

    def scan_tile(t, d, hc):
        r0 = pl.multiple_of(t * SUBLANES, SUBLANES)
        a = a_ref[d, pl.ds(r0, SUBLANES), :]
        b = b_ref[d, pl.ds(r0, SUBLANES), :]
        for sh in (1, 2, 4):
            if d == 0:
                a_s = pltpu.roll(a, sh, 0)
                b_s = pltpu.roll(b, sh, 0)
                m = row >= sh
            else:
                a_s = pltpu.roll(a, SUBLANES - sh, 0)
                b_s = pltpu.roll(b, SUBLANES - sh, 0)
                m = row < SUBLANES - sh
            b = jnp.where(m, a * b_s + b, b)
            a = jnp.where(m, a * a_s, a)
        hcur = b + a * hc
        acc_ref[pl.ds(r0, SUBLANES), :] += hcur
        last = hcur[SUBLANES - 1:SUBLANES, :] if d == 0 else hcur[0:1, :]
        return jnp.broadcast_to(last, (SUBLANES, width))

    def scan_body(s, carry):
        hf, hb = carry
        tb = jnp.where(s < nt_ctx, nt_ctx - 1 - s, nt_all + nt_ctx - 1 - s)
        return scan_tile(s, 0, hf), scan_tile(tb, 1, hb)

    zero = jnp.zeros((SUBLANES, width), F32)
    lax.fori_loop(0, nt_all, scan_body, (zero, zero))


    def out_body(c, carry):
        r0 = pl.multiple_of(c * CHUNK, CHUNK)
        o_ref[0, pl.ds(r0, CHUNK), :] = acc_ref[pl.ds(r0, CHUNK), :] * _gelu_tanh(gt_ref[0, pl.ds(r0, CHUNK), :])
        return carry

    lax.fori_loop(0, nct, out_body, 0)


def _lru(xb, gate, conv_w, conv_b, wa, wx, pvec, n_ctx, n_lat):
    bsz, L, width = xb.shape
    ng = width // LANES
    lp = L + 3 * HALO
    assert L % ROW_BLK == 0
    seq = pl.BlockSpec((1, L, LANES), lambda b, j: (b, 0, j))
    return pl.pallas_call(
        functools.partial(_lru_kernel, n_ctx=n_ctx, n_lat=n_lat),
        grid=(bsz, ng),
        in_specs=[seq, seq,
                  pl.BlockSpec((CONV_W, LANES), lambda b, j: (0, j)),
                  pl.BlockSpec((1, LANES), lambda b, j: (0, j)),
                  pl.BlockSpec((2, 1, LANES, LANES), lambda b, j: (0, j, 0, 0)),
                  pl.BlockSpec((2, 1, LANES, LANES), lambda b, j: (0, j, 0, 0)),
                  pl.BlockSpec((2, SUBLANES, LANES), lambda b, j: (0, 0, j))],
        out_specs=seq,
        out_shape=jax.ShapeDtypeStruct((bsz, L, width), F32),
        scratch_shapes=[pltpu.VMEM((lp, LANES), F32), pltpu.VMEM((L, LANES), F32), pltpu.VMEM((2, L, LANES), F32),
                        pltpu.VMEM((2, L, LANES), F32), pltpu.VMEM((L, LANES), F32)],
        compiler_params=_cparams(("arbitrary", "arbitrary")),
        name="lru_mixer",
    )(xb, gate, conv_w, conv_b.reshape(1, width), wa, wx, pvec)


def _mlstm_kernel(qc_ref, kc_ref, vc_ref, gcc_ref, grc_ref, ql_ref, kl_ref, vl_ref, gcl_ref, grl_ref, og_ref, nw_ref,
                  o_ref, acc_ref, cc_ref, ms_ref, p2_ref, mi_ref, bc_ref, *, n_ctx, n_lat, n_heads, dqk):
    h = pl.program_id(1)
    ncc = n_ctx // CHUNK
    ncl = n_lat // CHUNK
    dv = LANES
    lane = _iota((CHUNK, LANES), 1)
    ones_blk = jnp.ones((CHUNK, LANES), F32)

    def prepare(refs, slot0, chains, with_out):
        q_ref, k_ref, v_ref, gcol_ref, grow_ref = refs
        idx = range(len(chains))
        r0 = [pl.multiple_of(c * CHUNK, CHUNK) for c, _ in chains]
        k = [k_ref[0, pl.ds(r0[i], CHUNK), :] * (dqk ** -0.5) for i in idx]
        v_ext = [jnp.concatenate([v_ref[0, pl.ds(r0[i], CHUNK), :], ones_blk], axis=1).astype(BF16) for i in idx]
        gates = [gcol_ref[0, pl.ds(r0[i], CHUNK), :] for i in idx]
        li = [_lane_col(gates[i], chains[i][1] * n_heads + h) for i in idx]
        bcum = [_lane_col(gates[i], 2 * n_heads + chains[i][1] * n_heads + h) for i in idx]
        btot = [bcum[i][CHUNK - 1:CHUNK, :] if chains[i][1] == 0 else bcum[i][0:1, :] for i in idx]
        w_end = [btot[i] - bcum[i] + li[i] for i in idx]
        m_chunk = [jnp.max(w_end[i], axis=0, keepdims=True) for i in idx]
        c_chunk = [_tn((k[i] * jnp.exp(w_end[i] - m_chunk[i])).astype(BF16), v_ext[i]) for i in idx]
        if with_out:
            qk = [_nt(q_ref[0, pl.ds(r0[i], CHUNK), :].astype(BF16), k[i].astype(BF16)) for i in idx]
            s_qk, m_intra = [], []
            for i in idx:
                c, d = chains[i]
                grt = grow_ref[0, c]
                rowi = _iota(grt.shape, 0)
                li_row = jnp.sum(jnp.where(rowi == d * n_heads + h, grt, 0.0), axis=0, keepdims=True)
                b_row = jnp.sum(jnp.where(rowi == 2 * n_heads + d * n_heads + h, grt, 0.0), axis=0, keepdims=True)
                incl, _ = _tri(d)
                log_d = jnp.where(incl, bcum[i] + (li_row - b_row), NEG_BIG)
                m_intra.append(jnp.max(log_d, axis=1, keepdims=True))
                s_qk.append((qk[i] * jnp.exp(log_d - m_intra[i])).astype(BF16))
            p2 = [_mm(s_qk[i], v_ext[i]) for i in idx]
        row8 = _iota((SUBLANES, LANES), 0)
        for i in idx:
            c, d = chains[i]
            cc_ref[d, slot0 + c] = c_chunk[i]
            ms_ref[d, slot0 + c] = jnp.where(row8 == 0, btot[i], m_chunk[i])
            if with_out:
                p2_ref[d, pl.ds(r0[i], CHUNK), :] = p2[i]
                mi_ref[d, pl.ds(r0[i], CHUNK), :] = jnp.broadcast_to(m_intra[i], (CHUNK, LANES))
                bc_ref[d, pl.ds(r0[i], CHUNK), :] = jnp.broadcast_to(bcum[i], (CHUNK, LANES))

    ctx_refs = (qc_ref, kc_ref, vc_ref, gcc_ref, grc_ref)
    lat_refs = (ql_ref, kl_ref, vl_ref, gcl_ref, grl_ref)

    def ctx_prep(c2, carry):
        prepare(ctx_refs, 0, [(2 * c2 + j, d) for j in range(2) for d in range(2)], False)
        return carry

    lax.fori_loop(0, ncc // 2, ctx_prep, 0)

    def lat_prep(c4, carry):
        for j in range(4):
            acc_ref[pl.ds(pl.multiple_of((4 * c4 + j) * CHUNK, CHUNK), CHUNK), :] = jnp.zeros((CHUNK, dv), F32)
        prepare(lat_refs, ncc, [(4 * c4 + j, d) for j in range(4) for d in range(2)], True)
        return carry

    lax.fori_loop(0, ncl // 4, lat_prep, 0)

    def advance(states, slots):
        out = []
        for d in range(2):
            cx, m_s = states[d]
            ms = ms_ref[d, slots[d]]
            btot = ms[0:1, :]
            m_chunk = ms[1:2, :]
            m_new = jnp.maximum(btot + m_s, m_chunk)
            f_old = jnp.exp(btot + m_s - m_new)
            f_new = jnp.exp(m_chunk - m_new)
            wide = lambda f: jnp.concatenate([f, f], axis=1)
            out.append((wide(f_old) * cx + wide(f_new) * cc_ref[d, slots[d]], m_new))
        return tuple(out)

    def ctx_step(s, carry):
        return advance(carry, (s, ncc - 1 - s))

    zero = (jnp.zeros((LANES, 2 * dv), F32), jnp.zeros((1, LANES), F32))
    carry = lax.fori_loop(0, ncc, ctx_step, (zero, zero))

    def lat_step(s, carry):
        cs = (s, ncl - 1 - s)
        r0 = [pl.multiple_of(c * CHUNK, CHUNK) for c in cs]
        p1 = [_mm(ql_ref[0, pl.ds(r0[d], CHUNK), :].astype(BF16), carry[d][0].astype(BF16)) for d in range(2)]
        for d in range(2):
            m_intra = mi_ref[d, pl.ds(r0[d], CHUNK), :]
            m_inter = bc_ref[d, pl.ds(r0[d], CHUNK), :] + carry[d][1]
            m_tot = jnp.maximum(m_inter, m_intra)
            w_inter = jnp.exp(m_inter - m_tot)
            w_intra = jnp.exp(m_intra - m_tot)
            p2 = p2_ref[d, pl.ds(r0[d], CHUNK), :]
            num = w_inter * p1[d][:, :dv] + w_intra * p2[:, :dv]
            den = w_inter * p1[d][:, dv:] + w_intra * p2[:, dv:]
            acc_ref[pl.ds(r0[d], CHUNK), :] += num / jnp.maximum(jnp.abs(den), jnp.exp(-m_tot))
        return advance(carry, (ncc + cs[0], ncc + cs[1]))

    lax.fori_loop(0, ncl, lat_step, carry)

    nw = nw_ref[...]

    def out_body(c, carry):
        r0 = pl.multiple_of(c * ROW_BLK, ROW_BLK)
        hs = acc_ref[pl.ds(r0, ROW_BLK), :]
        y = hs * lax.rsqrt(jnp.mean(hs * hs, axis=-1, keepdims=True) + EPS) * nw
        o_ref[0, pl.ds(r0, ROW_BLK), :] = y * jax.nn.sigmoid(og_ref[0, pl.ds(r0, ROW_BLK), :])
        return carry

    lax.fori_loop(0, n_lat // ROW_BLK, out_body, 0)


def _mlstm(qc, kc, vc, gc, ql, kl, vl, gl, og, gparams, norm_w, n_heads, dqk):
    bsz, n_ctx, _ = qc.shape
    n_lat = ql.shape[1]
    cs = pl.BlockSpec((1, n_ctx, LANES), lambda b, h: (b, 0, h))
    ls = pl.BlockSpec((1, n_lat, LANES), lambda b, h: (b, 0, h))
    assert n_ctx % (2 * CHUNK) == 0 and n_lat % (4 * CHUNK) == 0 and n_lat % ROW_BLK == 0
    gcc, grc = _gates(gc, gparams, n_heads, "mlstm")
    gcl, grl = _gates(gl, gparams, n_heads, "mlstm")
    row_spec = lambda n: pl.BlockSpec((1, n // CHUNK, 2 * SUBLANES, CHUNK), lambda b, h: (b, 0, 0, 0))
    return pl.pallas_call(
        functools.partial(_mlstm_kernel, n_ctx=n_ctx, n_lat=n_lat, n_heads=n_heads, dqk=dqk),
        grid=(bsz, n_heads),
        in_specs=[cs, cs, cs, pl.BlockSpec((1, n_ctx, LANES), lambda b, h: (b, 0, 0)), row_spec(n_ctx),
                  ls, ls, ls, pl.BlockSpec((1, n_lat, LANES), lambda b, h: (b, 0, 0)), row_spec(n_lat),
                  ls,
                  pl.BlockSpec((1, LANES), lambda b, h: (0, h))],
        out_specs=ls,
        out_shape=jax.ShapeDtypeStruct((bsz, n_lat, n_heads * LANES), F32),
        scratch_shapes=[pltpu.VMEM((n_lat, LANES), F32),
                        pltpu.VMEM((2, (n_ctx + n_lat) // CHUNK, LANES, 2 * LANES), F32),
                        pltpu.VMEM((2, (n_ctx + n_lat) // CHUNK, SUBLANES, LANES), F32),
                        pltpu.VMEM((2, n_lat, 2 * LANES), F32),
                        pltpu.VMEM((2, n_lat, LANES), F32),
                        pltpu.VMEM((2, n_lat, LANES), F32)],
        compiler_params=_cparams(("arbitrary", "arbitrary")),
        name="mlstm_mixer",
    )(qc, kc, vc, gcc, grc, ql, kl, vl, gcl, grl, og, norm_w.reshape(1, -1))


def _pool_kernel(x_ref, w_ref, sc_ref, o_ref, *, seg):
    tm = x_ref.shape[0]
    r = _iota((tm, tm), 0)
    c = _iota((tm, tm), 1)
    sh = seg.bit_length() - 1
    same = lax.shift_right_logical(r, sh) == lax.shift_right_logical(c, sh)
    t = r & (seg - 1)
    s = c & (seg - 1)
    for gi, wsz in enumerate(POOL_SIZES):
        lo = jnp.maximum(t - wsz // 2, 0)
        hi = jnp.minimum(t - wsz // 2 + wsz, seg)
        inwin = same & (s >= lo) & (s < hi)
        pmat = jnp.where(inwin, 1.0 / (hi - lo).astype(F32), 0.0) - jnp.where(r == c, 1.0, 0.0)
        x = x_ref[:, gi * LANES:(gi + 1) * LANES]
        pooled = _mm3(pmat, x)
        y = _mm(pooled.astype(BF16), w_ref[gi].astype(BF16))
        o_ref[:, gi * LANES:(gi + 1) * LANES] = y * sc_ref[:, gi * LANES:(gi + 1) * LANES]


def _pool(x, w_grp, scale, seg):
    t, width = x.shape
    tm = TOK_BLK
    assert seg & (seg - 1) == 0 and tm % seg == 0
    return pl.pallas_call(
        functools.partial(_pool_kernel, seg=seg),
        grid=(t // tm,),
        in_specs=[pl.BlockSpec((tm, width), lambda i: (i, 0)),
                  pl.BlockSpec(w_grp.shape, lambda i: (0, 0, 0)),
                  pl.BlockSpec((1, width), lambda i: (0, 0))],
        out_specs=pl.BlockSpec((tm, width), lambda i: (i, 0)),
        out_shape=jax.ShapeDtypeStruct((t, width), F32),
        compiler_params=_cparams(("arbitrary",)),
        name="pool_mixer",
    )(x, w_grp, scale.reshape(1, width))


def _outproj_kernel(x_ref, ya_ref, yb_ref, wa_ref, wb_ref, g1_ref, gn_ref, sh_ref, sc_ref, xo_ref, h_ref):
    y = _mm(ya_ref[...].astype(BF16), wa_ref[...]) + _mm(yb_ref[...].astype(BF16), wb_ref[...])
    xn = x_ref[...] + g1_ref[0] * y
    xo_ref[...] = xn
    hn = xn * lax.rsqrt(jnp.mean(xn * xn, axis=-1, keepdims=True) + EPS) * gn_ref[...]
    h_ref[...] = (hn * (1.0 + sc_ref[0]) + sh_ref[0]).astype(BF16)


def _outproj(x, ya, yb, w_out, gain, tab, rowfn):
    t, d = x.shape
    wa_n = ya.shape[1]
    tm = TOK_BLK
    w = w_out.astype(BF16)
    mod = lambda k: pl.BlockSpec((1, 1, d), lambda i: (rowfn(i) + k, 0, 0))
    return pl.pallas_call(
        _outproj_kernel,
        grid=(t // tm,),
        in_specs=[pl.BlockSpec((tm, d), lambda i: (i, 0)),
                  pl.BlockSpec((tm, wa_n), lambda i: (i, 0)),
                  pl.BlockSpec((tm, yb.shape[1]), lambda i: (i, 0)),
                  pl.BlockSpec((wa_n, d), lambda i: (0, 0)),
                  pl.BlockSpec((yb.shape[1], d), lambda i: (0, 0)),
                  mod(2),
                  pl.BlockSpec((1, d), lambda i: (0, 0)),
                  mod(3), mod(4)],
        out_specs=[pl.BlockSpec((tm, d), lambda i: (i, 0)), pl.BlockSpec((tm, d), lambda i: (i, 0))],
        out_shape=[jax.ShapeDtypeStruct((t, d), F32), jax.ShapeDtypeStruct((t, d), BF16)],
        compiler_params=_cparams(("arbitrary",)),
        name="out_proj",
    )(x, ya, yb, w[:wa_n], w[wa_n:], tab, gain.reshape(1, d), tab, tab)


def _hyperbola_pairs():
    return [(j1, j2) for j1 in range(PEER_TOPK) for j2 in range(PEER_TOPK) if (j1 + 1) * (j2 + 1) <= PEER_TOPK]


def _topk_rows(s, k, break_ties):
    n, t = s.shape
    rowi = _iota((n, t), 0).astype(F32)
    rank = jnp.full((n, t), float(2 * k), F32)
    cur = s
    vals = []
    for j in range(k):
        m = jnp.max(cur, axis=0, keepdims=True)
        sel = cur == m
        if break_ties:
            sel = rowi == jnp.min(jnp.where(sel, rowi, float(n)), axis=0, keepdims=True)
        rank = jnp.where(sel, float(j), rank)
        cur = jnp.where(sel, -jnp.inf, cur)
        vals.append(m)
    return vals, rank


def _candidate_stage(v1, v2, break_ties):
    pairs = _hyperbola_pairs()
    cand = jnp.concatenate([v1[j1] + v2[j2] for j1, j2 in pairs], axis=0)
    _, crank = _topk_rows(cand, PEER_TOPK, break_ties)
    chosen = crank < PEER_TOPK
    zsum = jnp.sum(jnp.where(chosen, jnp.exp(cand - (v1[0] + v2[0])), 0.0), axis=0, keepdims=True)
    counts = jnp.where(chosen, 1.0, 0.0)
    n_by_rank = []
    row = 0
    for j1 in range(PEER_TOPK):
        width = PEER_TOPK // (j1 + 1)
        n_by_rank.append(jnp.sum(counts[row:row + width], axis=0, keepdims=True))
        row += width
    return n_by_rank, zsum, jnp.sum(counts, axis=0, keepdims=True)


def _route(s1, s2):
    v1, rank1 = _topk_rows(s1, PEER_TOPK, True)
    v2, rank2 = _topk_rows(s2, PEER_TOPK, True)
    n_by_rank, zsum, _ = _candidate_stage(v1, v2, True)
    n1 = jnp.zeros_like(rank1)
    for j1 in range(PEER_TOPK):
        n1 = jnp.where(rank1 == float(j1), n_by_rank[j1], n1)
    e2 = jnp.where(rank2 < PEER_TOPK, jnp.exp(s2 - v2[0]), 0.0)
    e1 = jnp.where(rank1 < PEER_TOPK, jnp.exp(s1 - v1[0]), 0.0) / zsum
    return rank2, e2, n1, e1


def _sorted_top(s):
    n = s.shape[0] // SUBLANES
    a = [s[SUBLANES * j:SUBLANES * (j + 1), :] for j in range(n)]

    def cex(i, l):
        a[i], a[l] = jnp.maximum(a[i], a[l]), jnp.minimum(a[i], a[l])

    k = 2
    while k <= n:
        j = k // 2
        while j >= 1:
            for i in range(n):
                l = i ^ j
                if l > i:
                    if (i & k) == 0:
                        cex(i, l)
                    else:
                        cex(l, i)
            j //= 2
        k *= 2
    for shift in (4, 2, 1):
        a = [jnp.maximum(a[j], pltpu.roll(a[n - 1 - j], shift, 0)) for j in range(n)]
        j = n // 2
        while j >= 1:
            for i in range(n):
                l = i ^ j
                if l > i:
                    cex(i, l)
            j //= 2
    return a


def _by_rank(bits, table):
    level = list(table)
    for g in reversed(bits):
        level = [jnp.where(g, level[2 * i + 1], level[2 * i]) for i in range(len(level) // 2)]
    return level[0]


def _rank_bits(x, v):
    g8 = v[7] > x
    g4 = jnp.where(g8, v[11], v[3]) > x
    g2 = _by_rank([g8, g4], [v[1], v[5], v[9], v[13]]) > x
    g1 = _by_rank([g8, g4, g2], [v[2 * i] for i in range(8)]) > x
    return [g8, g4, g2, g1]


def _route_untied(s1, s2):
    v1 = _sorted_top(s1)
    v2 = _sorted_top(s2)
    n_by_rank, zsum, _ = _candidate_stage([v[0:1, :] for v in v1], [v[0:1, :] for v in v2], True)
    shape = v1[0].shape
    n_tab = [jnp.broadcast_to(nj, shape) for nj in n_by_rank]
    inv_z = jnp.broadcast_to(1.0 / zsum, shape)
    rank2, e2, n1, e1 = [], [], [], []
    cnt1 = jnp.zeros(shape, F32)
    cnt2 = jnp.zeros(shape, F32)
    for r in range(s1.shape[0] // SUBLANES):
        x1 = s1[SUBLANES * r:SUBLANES * (r + 1), :]
        x2 = s2[SUBLANES * r:SUBLANES * (r + 1), :]
        in1 = x1 >= v1[PEER_TOPK - 1]
        in2 = x2 >= v2[PEER_TOPK - 1]
        n1.append(jnp.where(in1, _by_rank(_rank_bits(x1, v1), n_tab), 0.0))
        e1.append(jnp.where(in1, jnp.exp(x1 - v1[0]) * inv_z, 0.0))
        g8, g4, g2, g1 = _rank_bits(x2, v2)
        rk = (jnp.where(g8, 8.0, 0.0) + jnp.where(g4, 4.0, 0.0)) + (jnp.where(g2, 2.0, 0.0) + jnp.where(g1, 1.0, 0.0))
        rank2.append(jnp.where(in2, rk, float(2 * PEER_TOPK)))
        e2.append(jnp.where(in2, jnp.exp(x2 - v2[0]), 0.0))
        cnt1 = cnt1 + jnp.where(in1, 1.0, 0.0)
        cnt2 = cnt2 + jnp.where(in2, 1.0, 0.0)
    tied = jnp.zeros((1, shape[1]), F32)
    for cnt in (cnt1, cnt2):
        tied = tied + jnp.abs(jnp.sum(cnt, axis=0, keepdims=True) - float(PEER_TOPK))
    for v in (v1, v2):
        for j in range(PEER_TOPK - 1):
            tied = tied + jnp.where(v[j][0:1, :] == v[j + 1][0:1, :], 1.0, 0.0)
    cat = lambda parts: jnp.concatenate(parts, axis=0)
    return cat(rank2), cat(e2), cat(n1), cat(e1), tied


def _peer_select_kernel(h_ref, wq_ref, keys_ref, r2_ref, e2_ref, n1_ref, e1_ref):
    hb = h_ref[...]
    heads = range(wq_ref.shape[0])
    scores = []
    for j in heads:
        q = _mm(hb, wq_ref[j])
        scores.append((_nt(keys_ref[j, 0], q[:, :N_KEYS], HI),
                       _nt(keys_ref[j, 1], q[:, N_KEYS:], HI)))

    def emit(j, rank2, e2, n1, e1):
        r2_ref[j] = rank2.astype(r2_ref.dtype)
        e2_ref[j] = e2.astype(e2_ref.dtype)
        n1_ref[j] = n1
        e1_ref[j] = e1

    tied = []
    for j in heads:
        rank2, e2, n1, e1, t = _route_untied(*scores[j])
        emit(j, rank2, e2, n1, e1)
        tied.append(jnp.max(t))

    for j in heads:
        @pl.when(tied[j] > 0.0)
        def _(j=j):
            emit(j, *_route(*scores[j]))


def _peer_select(h, wq, keys):
    t, d = h.shape
    nh = keys.shape[0]
    qd = wq.shape[2]
    tm = PEER_BLK
    hps = SELECT_HEADS
    assert nh % hps == 0
    ospec = pl.BlockSpec((hps, N_KEYS, tm), lambda i, hh: (hh, 0, i))
    return pl.pallas_call(
        _peer_select_kernel,
        grid=(t // tm, nh // hps),
        in_specs=[pl.BlockSpec((tm, d), lambda i, hh: (i, 0)),
                  pl.BlockSpec((hps, d, qd), lambda i, hh: (hh, 0, 0)),
                  pl.BlockSpec((hps, 2, N_KEYS, qd // 2), lambda i, hh: (hh, 0, 0, 0))],
        out_specs=[ospec, ospec, ospec, ospec],
        out_shape=[jax.ShapeDtypeStruct((nh, N_KEYS, t), dt) for dt in (BF16, BF16, F32, F32)],
        compiler_params=_cparams(("arbitrary", "arbitrary")),
        name="peer_select",
    )(h, wq, keys)


def _peer_dense_kernel(h_ref, u_hbm, vt_hbm, r2_ref, e2_ref, n1_ref, e1_ref, x_ref, g2a_ref, g2b_ref, fn_ref, o_ref,
                       acc_ref, wact_ref, ubuf_ref, vtbuf_ref, usem, vsem, *, n_heads, final):
    tb = pl.program_id(0)
    eb = pl.program_id(1)
    n_eb = pl.num_programs(1) - 1
    n_fetch = pl.num_programs(0) * n_eb
    ahead = TABLE_BUFFERS - 1
    cur = eb & 1
    prev = 1 - cur

    def u_copy(f):
        blk = lax.rem(f, n_eb)
        slot = lax.rem(f, TABLE_BUFFERS)
        return pltpu.make_async_copy(u_hbm.at[pl.ds(blk * EXP_BLK, EXP_BLK)], ubuf_ref.at[slot], usem.at[slot])

    def v_copy(f):
        slot = lax.rem(f, TABLE_BUFFERS)
        return pltpu.make_async_copy(vt_hbm.at[lax.rem(f, n_eb)], vtbuf_ref.at[slot], vsem.at[slot])

    @pl.when((tb == 0) & (eb == 0))
    def _():
        for f in range(ahead):
            u_copy(f).start()
            v_copy(f).start()

    f_u = tb * n_eb + eb
    f_v = tb * n_eb + eb - 1

    @pl.when(eb < n_eb)
    def _():
        u_copy(f_u).wait()

        @pl.when(f_u + ahead < n_fetch)
        def _():
            u_copy(f_u + ahead).start()

    @pl.when(eb >= 1)
    def _():
        v_copy(f_v).wait()

        @pl.when(f_v + ahead < n_fetch)
        def _():
            v_copy(f_v + ahead).start()

    u_slot = lax.rem(f_u, TABLE_BUFFERS)
    v_slot = lax.rem(jnp.maximum(f_v, 0), TABLE_BUFFERS)

    def gated_block(with_update):
        hb = h_ref[...]
        part_rows = 2 * N_KEYS
        n_part = EXP_BLK // part_rows
        acts = [_nt(ubuf_ref[u_slot, p * part_rows:(p + 1) * part_rows, :], hb) for p in range(n_part)]
        if with_update:
            upd = _mm(vtbuf_ref[v_slot], wact_ref[prev])
        rows_per = EXP_BLK // N_KEYS
        i1_base = pl.multiple_of(eb * rows_per, rows_per)
        n1_tiles = [n1_ref[hh, pl.ds(i1_base, rows_per), :] for hh in range(n_heads)]
        e1_tiles = [e1_ref[hh, pl.ds(i1_base, rows_per), :] for hh in range(n_heads)]
        zero = jnp.zeros((), BF16)
        for r in range(rows_per):
            wgt = None
            for hh in range(n_heads):
                shape = r2_ref.shape[1:]
                n1row = jnp.broadcast_to(n1_tiles[hh][r:r + 1, :], shape).astype(BF16)
                e1row = jnp.broadcast_to(e1_tiles[hh][r:r + 1, :], shape).astype(BF16)
                term = jnp.where(r2_ref[hh] < n1row, e2_ref[hh], zero) * e1row
                wgt = term if wgt is None else wgt + term
            off = (r * N_KEYS) % part_rows
            a = acts[(r * N_KEYS) // part_rows][off:off + N_KEYS, :]
            wact_ref[cur, r * N_KEYS:(r + 1) * N_KEYS, :] = _gelu_tanh(a.astype(BF16)) * wgt
        if with_update:
            acc_ref[...] += upd
        else:
            acc_ref[...] = jnp.zeros_like(acc_ref)

    @pl.when(eb == 0)
    def _():
        gated_block(False)

    @pl.when((eb > 0) & (eb < n_eb))
    def _():
        gated_block(True)

    @pl.when(eb == n_eb)
    def _():
        acc_ref[...] += _mm(vtbuf_ref[v_slot], wact_ref[prev])
        for part, g2_ref in enumerate((g2a_ref, g2b_ref)):
            rows = slice(part * TOK_BLK, (part + 1) * TOK_BLK)
            y = x_ref[rows, :] + g2_ref[0] * acc_ref[:, rows].T
            if final:
                y = y * lax.rsqrt(jnp.mean(y * y, axis=-1, keepdims=True) + EPS) * fn_ref[...]
            o_ref[rows, :] = y


def _peer_dense(h, u, vt, r2, e2, n1, e1, x, tab, rowfn, final_gain, final):
    t, d = h.shape
    n_eb = u.shape[0] // EXP_BLK
    nh = r2.shape[0]
    tm = PEER_BLK
    sel = pl.BlockSpec((nh, N_KEYS, tm), lambda i, e: (0, 0, i))
    g2 = lambda part: pl.BlockSpec((1, 1, d), lambda i, e: (rowfn(2 * i + part) + 5, 0, 0))
    return pl.pallas_call(
        functools.partial(_peer_dense_kernel, n_heads=nh, final=final),
        grid=(t // tm, n_eb + 1),
        in_specs=[pl.BlockSpec((tm, d), lambda i, e: (i, 0)),
                  pl.BlockSpec(memory_space=pl.ANY),
                  pl.BlockSpec(memory_space=pl.ANY),
                  sel, sel, sel, sel,
                  pl.BlockSpec((tm, d), lambda i, e: (i, 0)),
                  g2(0), g2(1),
                  pl.BlockSpec((1, d), lambda i, e: (0, 0))],
        out_specs=pl.BlockSpec((tm, d), lambda i, e: (i, 0)),
        out_shape=jax.ShapeDtypeStruct((t, d), F32),
        scratch_shapes=[pltpu.VMEM((d, tm), F32), pltpu.VMEM((2, EXP_BLK, tm), BF16),
                        pltpu.VMEM((TABLE_BUFFERS, EXP_BLK, d), BF16), pltpu.VMEM((TABLE_BUFFERS, d, EXP_BLK), BF16),
                        pltpu.SemaphoreType.DMA((TABLE_BUFFERS,)), pltpu.SemaphoreType.DMA((TABLE_BUFFERS,))],
        compiler_params=_cparams(("arbitrary", "arbitrary")),
        name="peer_dense",
    )(h, u, vt, r2, e2, n1, e1, x, tab, tab, final_gain.reshape(1, d))


def _peer(h, x, wq, keys, u_tab, v_tab, tab, rowfn, final_gain, final):
    d = h.shape[1]
    nh = keys.shape[0]
    wq_h = wq.reshape(d, nh, -1).transpose(1, 0, 2).astype(BF16)
    vt = v_tab.reshape(-1, EXP_BLK, d).transpose(0, 2, 1).astype(BF16)
    r2, e2, n1, e1 = _peer_select(h, wq_h, keys)
    return _peer_dense(h, u_tab.astype(BF16), vt, r2, e2, n1, e1, x, tab, rowfn, final_gain, final)


def _lane_vec(vals, rows=SUBLANES):
    out = jnp.zeros((rows, LANES), F32)
    for r, v in enumerate(vals):
        out = out.at[r, :v.shape[0]].set(v.astype(F32))
    return out


def _block_diag_groups(w):
    two, nb, bd, _ = w.shape
    per = LANES // bd
    wg = w.reshape(two, nb // per, per, bd, bd)
    eye = jnp.eye(per, dtype=w.dtype)
    return jnp.einsum("dgpij,pq->dgpiqj", wg, eye).reshape(two, nb // per, LANES, LANES)


def kernel(x, c, ctx, c_ctx, ada_w, ada_b, norm_mix, norm_ffn, final_norm, peer_wq, peer_keys, peer_u, peer_v,
           ev_w_in, ev_w_out, a_conv, a_alog, a_dtb, a_norm, b_conv_w, b_conv_b, b_wa, b_ba, b_wx, b_bx, b_lam,
           od_w_in, od_w_out, c_ibias, c_fbias, c_norm, d_w, d_scale):
    bsz, seq, dm = x.shape
    n_ctx = ctx.shape[1]
    L = n_ctx + seq
    rows = seq // GRID_W
    blk_per = L // TOK_BLK

    tab0 = _mod_table(c, c_ctx, ada_w[0], ada_b[0])
    row0 = lambda i: ((i // blk_per) * 2 + jnp.minimum(i % blk_per, 1)) * 6
    xcat = jnp.concatenate([ctx, x], axis=1).reshape(bsz * L, dm)

    a_heads = a_alog.shape[-1]
    a_width = a_heads * LANES
    a_qkv = 3 * a_width
    b_width = b_lam.shape[-1] * b_lam.shape[-2]
    w_in = ev_w_in[0]
    a_cols = a_qkv + a_width + 4 * a_heads
    w_ab = jnp.pad(w_in[:, a_qkv + a_width:a_cols], ((0, 0), (0, LANES - 4 * a_heads)))
    w0 = jnp.concatenate([w_in[:, :a_qkv], w_in[:, a_qkv:a_qkv + a_width], w_in[:, a_cols:a_cols + b_width],
                          w_in[:, a_cols + b_width:], w_ab], axis=1).astype(BF16)
    qkv, z, xb, gate, ab = _norm_mod_proj(xcat, norm_mix[0], tab0, row0, w0, (a_qkv, a_width, b_width, b_width, LANES))

    gparams = _lane_vec([a_alog[0].reshape(-1), a_dtb[0].reshape(-1)])
    ya = _gdn(qkv.reshape(bsz, L, a_qkv), z.reshape(bsz, L, a_width), ab.reshape(bsz, L, LANES),
              a_conv[0], gparams, a_norm[0], n_ctx, seq, a_heads)

    pvec = jnp.zeros((2, SUBLANES, b_width), F32)
    pvec = pvec.at[:, 0].set(b_ba[0].reshape(2, b_width)).at[:, 1].set(b_bx[0].reshape(2, b_width))
    pvec = pvec.at[:, 2].set(b_lam[0].reshape(2, b_width))
    yb = _lru(xb.reshape(bsz, L, b_width), gate.reshape(bsz, L, b_width), b_conv_w[0], b_conv_b[0],
              _block_diag_groups(b_wa[0]).astype(BF16), _block_diag_groups(b_wx[0]).astype(BF16), pvec, n_ctx, seq)

    x1, h1 = _outproj(xcat, ya.reshape(bsz * L, a_width), yb.reshape(bsz * L, b_width), ev_w_out[0], norm_ffn[0],
                      tab0, row0)
    x2 = _peer(h1, x1, peer_wq[0], peer_keys[0], peer_u[0], peer_v[0], tab0, row0, final_norm, False)
    x2 = x2.reshape(bsz, L, dm)

    tab1 = _mod_table(c, c_ctx, ada_w[1], ada_b[1])
    xc = x2[:, :n_ctx].reshape(bsz * n_ctx, dm)
    xl = x2[:, n_ctx:].reshape(bsz, rows, GRID_W, dm).transpose(0, 2, 1, 3).reshape(bsz * seq, dm)
    lat_per = seq // TOK_BLK
    ctx_per = n_ctx // TOK_BLK
    row_lat = lambda i: ((i // lat_per) * 2 + 1) * 6
    row_ctx = lambda i: ((i // ctx_per) * 2) * 6

    c_heads = c_ibias.shape[-1]
    c_width = c_norm.shape[-1]
    d_width = d_scale.shape[-1]
    w_in1 = od_w_in[0]
    n_state = w_in1.shape[1] - c_width - d_width
    c_qk = (n_state - c_width - 4 * c_heads) // 2
    dqk = c_qk // c_heads

    def pad_heads(wcols):
        return jnp.pad(wcols.reshape(dm, c_heads, dqk), ((0, 0), (0, 0), (0, LANES - dqk))).reshape(dm, c_heads * LANES)

    w_gate = jnp.pad(w_in1[:, 2 * c_qk + c_width:n_state], ((0, 0), (0, LANES - 4 * c_heads)))
    w1 = jnp.concatenate([pad_heads(w_in1[:, :c_qk]), pad_heads(w_in1[:, c_qk:2 * c_qk]),
                          w_in1[:, 2 * c_qk:2 * c_qk + c_width], w_in1[:, n_state:n_state + c_width],
                          w_in1[:, n_state + c_width:], w_gate], axis=1).astype(BF16)
    hw = c_heads * LANES
    widths1 = (hw, hw, c_width, c_width, d_width, LANES)
    ql, kl, vl, og, dl, gl = _norm_mod_proj(xl, norm_mix[1], tab1, row_lat, w1, widths1)
    qc, kc, vc, _, _, gc = _norm_mod_proj(xc, norm_mix[1], tab1, row_ctx, w1, widths1)

    gparams1 = _lane_vec([jnp.concatenate([c_ibias[0].reshape(-1), jnp.zeros((2 * c_heads,), F32)]),
                          jnp.concatenate([jnp.zeros((2 * c_heads,), F32), c_fbias[0].reshape(-1)])])
    r3 = lambda a, n: a.reshape(bsz, n, a.shape[-1])
    yc1 = _mlstm(r3(qc, n_ctx), r3(kc, n_ctx), r3(vc, n_ctx), r3(gc, n_ctx),
                 r3(ql, seq), r3(kl, seq), r3(vl, seq), r3(gl, seq), r3(og, seq), gparams1, c_norm[0], c_heads, dqk)
    yd1 = _pool(dl, d_w[0], d_scale[0], rows)

    x3, h3 = _outproj(xl, yc1.reshape(bsz * seq, c_width), yd1, od_w_out[0], norm_ffn[1], tab1, row_lat)
    out = _peer(h3, x3, peer_wq[1], peer_keys[1], peer_u[1], peer_v[1], tab1, row_lat, final_norm, True)
    return out.reshape(bsz, GRID_W, rows, dm).transpose(0, 2, 1, 3).reshape(bsz, seq, dm)
```
